```python
import math
import jax, jax.numpy as jnp
from jax import lax
import numpy as np

D_MODEL = 2048
BATCH = 2
SEQ = 8192
DEPTH = 1

D_FF = 5504
SSD_D_INNER = 2048
SSD_HEAD_DIM = 64
SSD_HEADS = SSD_D_INNER // SSD_HEAD_DIM
SSD_GROUPS = 4
SSD_HEADS_PER_GROUP = SSD_HEADS // SSD_GROUPS
SSD_STATE = 128
SSD_CONV = 4
SSD_CHUNK = 128
SSD_CONV_DIM = SSD_D_INNER + 2 * SSD_GROUPS * SSD_STATE
S5_WIDTH = 1024
S5_GROUP_SIZE = 16
S5_GROUPS = S5_WIDTH // S5_GROUP_SIZE
S5_STATE = 64
OFF_XBC = SSD_D_INNER
OFF_DT = OFF_XBC + SSD_CONV_DIM
OFF_U = OFF_DT + SSD_HEADS
OFF_GATES = OFF_U + S5_WIDTH
IN_COLS = OFF_GATES + 2 * D_MODEL
N_ADA = 9
EPS = 1e-6

kernel_name = "hybrid_ssd_s5_gated_macaron_block"


def rms_norm(x, g):
    xf = x.astype(jnp.float32)
    y = xf * lax.rsqrt(jnp.mean(xf * xf, axis=-1, keepdims=True) + EPS)
    return (y * g.astype(jnp.float32)).astype(x.dtype)


def modulate(x, g, shift, scale):
    return rms_norm(x, g) * (1 + scale) + shift


def swiglu(h, w_in, w_out):
    a, b = jnp.split(h @ w_in, 2, axis=-1)
    return (jax.nn.silu(a) * b) @ w_out


def causal_depthwise_conv(x, w, b):
    k_w = w.shape[0]
    s = x.shape[1]
    xp = jnp.pad(x, ((0, 0), (k_w - 1, 0), (0, 0)))
    out = b
    for k in range(k_w):
        out = out + xp[:, k:k + s] * w[k]
    return out


def ssd_mixer(xbc, z, dt_raw, conv_w, conv_b, dt_bias, a_log, d_skip, norm_w):
    f32 = jnp.float32
    bsz, s, _ = xbc.shape
    nc = s // SSD_CHUNK
    G, R, P, N, L = SSD_GROUPS, SSD_HEADS_PER_GROUP, SSD_HEAD_DIM, SSD_STATE, SSD_CHUNK
    xbc = jax.nn.silu(causal_depthwise_conv(xbc, conv_w, conv_b)).astype(f32)
    xs, bm, cm = jnp.split(xbc, [SSD_D_INNER, SSD_D_INNER + G * N], axis=-1)
    xs = xs.reshape(bsz, nc, L, G, R, P)
    bm = bm.reshape(bsz, nc, L, G, N)
    cm = cm.reshape(bsz, nc, L, G, N)
    dt = jax.nn.softplus(dt_raw.astype(f32) + dt_bias.astype(f32)).reshape(bsz, nc, L, G, R)
    a = -jnp.exp(a_log.astype(f32)).reshape(G, R)
    xdt = xs * dt[..., None]
    da_cs = jnp.cumsum(jnp.moveaxis(dt * a, 2, -1), axis=-1)
    causal = jnp.tril(jnp.ones((L, L), dtype=bool))
    seg = da_cs[..., :, None] - da_cs[..., None, :]
    decay_ls = jnp.exp(jnp.where(causal, seg, -jnp.inf))
    cb = jnp.einsum('bclgn,bcsgn->bcgls', cm, bm)
    y_diag = jnp.einsum('bcgls,bcgrls,bcsgrp->bclgrp', cb, decay_ls, xdt)
    decay_to_end = jnp.exp(da_cs[..., -1:] - da_cs)
    states = jnp.einsum('bclgn,bcgrl,bclgrp->bcgrpn', bm, decay_to_end, xdt)
    chunk_decay = jnp.exp(da_cs[..., -1])

    def step(h, inp):
        st, dec = inp
        return dec[..., None, None] * h + st, h

    h0 = jnp.zeros((bsz, G, R, P, N), f32)
    _, prev = lax.scan(step, h0, (jnp.moveaxis(states, 1, 0), jnp.moveaxis(chunk_decay, 1, 0)))
    prev = jnp.moveaxis(prev, 0, 1)
    y_off = jnp.einsum('bclgn,bcgrpn,bcgrl->bclgrp', cm, prev, jnp.exp(da_cs))
    y = y_diag + y_off + d_skip.astype(f32).reshape(G, R)[:, :, None] * xs
    y = y.reshape(bsz, s, G, R * P)
    yz = y * jax.nn.silu(z.astype(f32)).reshape(bsz, s, G, R * P)
    yz = yz * lax.rsqrt(jnp.mean(yz * yz, axis=-1, keepdims=True) + EPS)
    return (yz.reshape(bsz, s, SSD_D_INNER) * norm_w.astype(f32)).astype(z.dtype)


def s5_mixer(u, lambda_re, lambda_im, b_re, b_im, c_re, c_im, d_skip, log_dt):
    f32 = jnp.float32
    bsz, s, _ = u.shape
    uf = u.astype(f32).reshape(bsz, s, S5_GROUPS, S5_GROUP_SIZE)
    dt = jnp.exp(log_dt.astype(f32))[:, None]
    lr = jnp.minimum(lambda_re.astype(f32), -1e-4)
    li = lambda_im.astype(f32)
    mag = jnp.exp(lr * dt)
    ar = mag * jnp.cos(li * dt)
    ai = mag * jnp.sin(li * dt)
    den = lr * lr + li * li
    nr = ar - 1.0
    kr = (nr * lr + ai * li) / den
    ki = (ai * lr - nr * li) / den
    br = b_re.astype(f32)
    bi = b_im.astype(f32)
    bbar_re = kr[..., None] * br - ki[..., None] * bi
    bbar_im = kr[..., None] * bi + ki[..., None] * br
    bu_re = jnp.einsum('bsgi,gpi->bsgp', uf, bbar_re)
    bu_im = jnp.einsum('bsgi,gpi->bsgp', uf, bbar_im)
    a_re = jnp.broadcast_to(ar, bu_re.shape)
    a_im = jnp.broadcast_to(ai, bu_im.shape)

    def combine(e1, e2):
        a1r, a1i, b1r, b1i = e1
        a2r, a2i, b2r, b2i = e2
        return (a2r * a1r - a2i * a1i,
                a2r * a1i + a2i * a1r,
                a2r * b1r - a2i * b1i + b2r,
                a2r * b1i + a2i * b1r + b2i)

    _, _, s_re, s_im = lax.associative_scan(combine, (a_re, a_im, bu_re, bu_im), axis=1)
    y = (jnp.einsum('bsgp,gip->bsgi', s_re, c_re.astype(f32))
         - jnp.einsum('bsgp,gip->bsgi', s_im, c_im.astype(f32))
         + d_skip.astype(f32) * uf)
    return y.reshape(bsz, s, S5_WIDTH).astype(u.dtype)


def setup_inputs(seed: int = 0) -> dict:
    key = jax.random.key(seed)
    ks = jax.random.split(key, 40)
    f32 = jnp.float32

    def nrm(k, shape, scale):
        return jax.random.normal(k, shape, f32) * scale

    dt_ssd = jnp.exp(jax.random.uniform(ks[11], (DEPTH, SSD_HEADS), f32, math.log(1e-3), math.log(1e-1)))
    lam_im = (jnp.pi * jnp.arange(S5_STATE, dtype=f32))[None, None, :] + nrm(ks[16], (DEPTH, S5_GROUPS, S5_STATE), 0.01)
    return {
        "x": nrm(ks[0], (BATCH, SEQ, D_MODEL), 1.0),
        "c": nrm(ks[1], (BATCH, D_MODEL), 1.0),
        "w_ada": nrm(ks[2], (DEPTH, D_MODEL, N_ADA * D_MODEL), D_MODEL ** -0.5),
        "b_ada": nrm(ks[3], (DEPTH, N_ADA * D_MODEL), 0.01),
        "norm_ffn1": 1.0 + nrm(ks[4], (DEPTH, D_MODEL), 0.02),
        "w_ffn1_in": nrm(ks[5], (DEPTH, D_MODEL, 2 * D_FF), D_MODEL ** -0.5),
        "w_ffn1_out": nrm(ks[6], (DEPTH, D_FF, D_MODEL), D_FF ** -0.5),
        "norm_mix": 1.0 + nrm(ks[7], (DEPTH, D_MODEL), 0.02),
        "w_in": nrm(ks[8], (DEPTH, D_MODEL, IN_COLS), D_MODEL ** -0.5),
        "conv_w": nrm(ks[9], (DEPTH, SSD_CONV, SSD_CONV_DIM), SSD_CONV ** -0.5),
        "conv_b": nrm(ks[10], (DEPTH, SSD_CONV_DIM), 0.01),
        "dt_bias": dt_ssd + jnp.log(-jnp.expm1(-dt_ssd)),
        "a_log": jnp.log(jax.random.uniform(ks[12], (DEPTH, SSD_HEADS), f32, 1.0, 16.0)),
        "d_ssd": 1.0 + nrm(ks[13], (DEPTH, SSD_HEADS), 0.02),
        "ssd_norm_w": 1.0 + nrm(ks[14], (DEPTH, SSD_D_INNER), 0.02),
        "w_a_proj": nrm(ks[15], (DEPTH, SSD_D_INNER, D_MODEL), SSD_D_INNER ** -0.5),
        "s5_lambda_re": -0.5 + nrm(ks[17], (DEPTH, S5_GROUPS, S5_STATE), 0.01),
        "s5_lambda_im": lam_im,
        "s5_b_re": nrm(ks[18], (DEPTH, S5_GROUPS, S5_STATE, S5_GROUP_SIZE), (2 * S5_GROUP_SIZE) ** -0.5),
        "s5_b_im": nrm(ks[19], (DEPTH, S5_GROUPS, S5_STATE, S5_GROUP_SIZE), (2 * S5_GROUP_SIZE) ** -0.5),
        "s5_c_re": nrm(ks[20], (DEPTH, S5_GROUPS, S5_GROUP_SIZE, S5_STATE), (2 * S5_STATE) ** -0.5),
        "s5_c_im": nrm(ks[21], (DEPTH, S5_GROUPS, S5_GROUP_SIZE, S5_STATE), (2 * S5_STATE) ** -0.5),
        "s5_d": nrm(ks[22], (DEPTH, S5_GROUPS, S5_GROUP_SIZE), 1.0),
        "s5_log_dt": jax.random.uniform(ks[23], (DEPTH, S5_GROUPS), f32, math.log(1e-3), math.log(1e-1)),
        "w_b_glu": nrm(ks[24], (DEPTH, S5_WIDTH, 2 * D_MODEL), S5_WIDTH ** -0.5),
        "w_out": nrm(ks[25], (DEPTH, D_MODEL, D_MODEL), D_MODEL ** -0.5),
        "norm_ffn2": 1.0 + nrm(ks[26], (DEPTH, D_MODEL), 0.02),
        "w_ffn2_in": nrm(ks[27], (DEPTH, D_MODEL, 2 * D_FF), D_MODEL ** -0.5),
        "w_ffn2_out": nrm(ks[28], (DEPTH, D_FF, D_MODEL), D_FF ** -0.5),
        "norm_final": 1.0 + nrm(ks[29], (D_MODEL,), 0.02),
    }


def reference(x, c, w_ada, b_ada, norm_ffn1, w_ffn1_in, w_ffn1_out, norm_mix, w_in,
              conv_w, conv_b, dt_bias, a_log, d_ssd, ssd_norm_w, w_a_proj,
              s5_lambda_re, s5_lambda_im, s5_b_re, s5_b_im, s5_c_re, s5_c_im, s5_d, s5_log_dt,
              w_b_glu, w_out, norm_ffn2, w_ffn2_in, w_ffn2_out, norm_final):
    c_act = jax.nn.silu(c)
    for l in range(DEPTH):
        mods = (c_act @ w_ada[l] + b_ada[l])[:, None, :]
        (sh1, sc1, g1, sh2, sc2, g2, sh3, sc3, g3) = jnp.split(mods, N_ADA, axis=-1)

        h = modulate(x, norm_ffn1[l], sh1, sc1)
        x = x + 0.5 * g1 * swiglu(h, w_ffn1_in[l], w_ffn1_out[l])

        h = modulate(x, norm_mix[l], sh2, sc2)
        proj = h @ w_in[l]
        z, xbc, dt_raw, u, gates = jnp.split(proj, [OFF_XBC, OFF_DT, OFF_U, OFF_GATES], axis=-1)
        gate_a, gate_b = jnp.split(gates, 2, axis=-1)

        y_a = ssd_mixer(xbc, z, dt_raw, conv_w[l], conv_b[l], dt_bias[l], a_log[l], d_ssd[l], ssd_norm_w[l])
        p_a = y_a @ w_a_proj[l]

        y_b = s5_mixer(u, s5_lambda_re[l], s5_lambda_im[l], s5_b_re[l], s5_b_im[l],
                       s5_c_re[l], s5_c_im[l], s5_d[l], s5_log_dt[l])
        glu_a, glu_g = jnp.split(jax.nn.gelu(y_b) @ w_b_glu[l], 2, axis=-1)
        p_b = glu_a * jax.nn.sigmoid(glu_g)

        merged = jax.nn.sigmoid(gate_a) * p_a + jax.nn.sigmoid(gate_b) * p_b
        x = x + g2 * (merged @ w_out[l])

        h = modulate(x, norm_ffn2[l], sh3, sc3)
        x = x + 0.5 * g3 * swiglu(h, w_ffn2_in[l], w_ffn2_out[l])
    return rms_norm(x, norm_final)
```

```python
import functools
import math

import jax
import jax.numpy as jnp
from jax import lax
from jax.experimental import pallas as pl
from jax.experimental.pallas import tpu as pltpu

F32 = jnp.float32
BF16 = jnp.bfloat16
EPS = 1e-6

VMEM_LIMIT_BYTES = 56 * 1024 * 1024
LANES = 128
SUBLANES = 8

SSD_HEAD_DIM = 64
SSD_GROUPS = 4
SSD_STATE = 128
SSD_CONV = 4
SSD_CHUNK = 128
S5_GROUP_SIZE = 16
S5_STATE = 64
S5_SLAB = 256
S5_GROUPS_PER_SLAB = S5_SLAB // S5_GROUP_SIZE
S5_SLAB_STATE = S5_GROUPS_PER_SLAB * S5_STATE
S5_CHUNK = 8
N_ADA = 9


def _params(*sem):
    return pltpu.CompilerParams(dimension_semantics=sem, vmem_limit_bytes=VMEM_LIMIT_BYTES)


def _silu(v):
    return v * jax.nn.sigmoid(v)


def _dot(a, b):
    return jnp.dot(a, b, preferred_element_type=F32)


def _dot_nt(a, b):
    return lax.dot_general(a, b, (((1,), (1,)), ((), ())), preferred_element_type=F32)


def _mod_rms(x, nw, shift, scale):
    ms = jnp.mean(x * x, axis=-1, keepdims=True)
    return (x * lax.rsqrt(ms + EPS) * nw) * (1.0 + scale) + shift


def _mods_kernel(ct_ref, w_ref, b_ref, o_ref):
    ca = _silu(ct_ref[...])
    w = w_ref[...]
    rows = [jnp.sum(ca[:, b:b + 1] * w, axis=0, keepdims=True) for b in range(ca.shape[1])]
    o_ref[...] = jnp.concatenate(rows, axis=0) + b_ref[...]


def _mods(c, w_ada, b_ada):
    bsz, d = c.shape
    n = w_ada.shape[1]
    tn = 1024
    return pl.pallas_call(
        _mods_kernel,
        out_shape=jax.ShapeDtypeStruct((bsz, n), F32),
        grid=(n // tn,),
        in_specs=[pl.BlockSpec((d, bsz), lambda j: (0, 0)),
                  pl.BlockSpec((d, tn), lambda j: (0, j)),
                  pl.BlockSpec((1, tn), lambda j: (0, j))],
        out_specs=pl.BlockSpec((bsz, tn), lambda j: (0, j)),
        compiler_params=_params("arbitrary"),
        name="mods",
    )(c.T, w_ada, b_ada.reshape(1, n))


def _prenorm_kernel(x_ref, nw_ref, mods_ref, h_ref, *, k):
    h = _mod_rms(x_ref[...], nw_ref[...], mods_ref[k:k + 1, :], mods_ref[k + 1:k + 2, :])
    h_ref[...] = h.astype(h_ref.dtype)


def _prenorm(x2, nw, mods3, k, seq):
    m, d = x2.shape
    tm = min(512, seq)
    per_b = seq // tm
    return pl.pallas_call(
        functools.partial(_prenorm_kernel, k=k),
        out_shape=jax.ShapeDtypeStruct((m, d), BF16),
        grid=(m // tm,),
        in_specs=[pl.BlockSpec((tm, d), lambda i: (i, 0)),
                  pl.BlockSpec((1, d), lambda i: (0, 0)),
                  pl.BlockSpec((None, N_ADA, d), lambda i: (i // per_b, 0, 0))],
        out_specs=pl.BlockSpec((tm, d), lambda i: (i, 0)),
        compiler_params=_params("arbitrary"),
        name="prenorm",
    )(x2, nw.reshape(1, d), mods3)


def _ffn_kernel(h_ref, x_ref, wa_ref, wb_ref, wo_ref, mods_ref, nw_ref, *out_refs, k_gate, k_next):
    j = pl.program_id(1)
    xo_ref = out_refs[0]

    @pl.when(j == 0)
    def _():
        xo_ref[...] = jnp.zeros_like(xo_ref)

    h = h_ref[...]
    a = _dot(h, wa_ref[...])
    b = _dot(h, wb_ref[...])
    act = (_silu(a) * b).astype(BF16)
    xo_ref[...] += _dot(act, wo_ref[...])

    @pl.when(j == pl.num_programs(1) - 1)
    def _():
        xn = x_ref[...] + 0.5 * mods_ref[k_gate:k_gate + 1, :] * xo_ref[...]
        if k_next is None:
            ms = jnp.mean(xn * xn, axis=-1, keepdims=True)
            xo_ref[...] = xn * lax.rsqrt(ms + EPS) * nw_ref[...]
        else:
            xo_ref[...] = xn
            hn = _mod_rms(xn, nw_ref[...], mods_ref[k_next:k_next + 1, :],
                          mods_ref[k_next + 1:k_next + 2, :])
            out_refs[1][...] = hn.astype(BF16)


def _ffn(h, x2, wa, wb, wo, mods3, nw_next, k_gate, k_next, seq):
    m, d = x2.shape
    ffp = wa.shape[1]
    tm = min(512, seq)
    tf = 512
    per_b = seq // tm
    row = lambda i, j: (i, 0)
    out_shape = [jax.ShapeDtypeStruct((m, d), F32)]
    out_specs = [pl.BlockSpec((tm, d), row)]
    if k_next is not None:
        out_shape.append(jax.ShapeDtypeStruct((m, d), BF16))
        out_specs.append(pl.BlockSpec((tm, d), row))
    return pl.pallas_call(
        functools.partial(_ffn_kernel, k_gate=k_gate, k_next=k_next),
        out_shape=out_shape,
        grid=(m // tm, ffp // tf),
        in_specs=[pl.BlockSpec((tm, d), row),
                  pl.BlockSpec((tm, d), row),
                  pl.BlockSpec((d, tf), lambda i, j: (0, j)),
                  pl.BlockSpec((d, tf), lambda i, j: (0, j)),
                  pl.BlockSpec((tf, d), lambda i, j: (j, 0)),
                  pl.BlockSpec((None, N_ADA, d), lambda i, j: (i // per_b, 0, 0)),
                  pl.BlockSpec((1, d), lambda i, j: (0, 0))],
        out_specs=out_specs,
        compiler_params=_params("arbitrary", "arbitrary"),
        name="ffn",
    )(h, x2, wa, wb, wo, mods3, nw_next.reshape(1, d))


def _proj_kernel(h_ref, w_ref, o_ref):
    o_ref[...] = _dot(h_ref[...], w_ref[...]).astype(o_ref.dtype)


def _proj(h, w, tn, slabs=False):
    m, d = h.shape
    n = w.shape[1]
    tm = min(1024, m)
    if slabs:
        out_shape = jax.ShapeDtypeStruct((n // tn, m, tn), F32)
        out_spec = pl.BlockSpec((None, tm, tn), lambda i, j: (j, i, 0))
    else:
        out_shape = jax.ShapeDtypeStruct((m, n), F32)
        out_spec = pl.BlockSpec((tm, tn), lambda i, j: (i, j))
    return pl.pallas_call(
        _proj_kernel,
        out_shape=out_shape,
        grid=(m // tm, n // tn),
        in_specs=[pl.BlockSpec((tm, d), lambda i, j: (i, 0)),
                  pl.BlockSpec((d, tn), lambda i, j: (0, j))],
        out_specs=out_spec,
        compiler_params=_params("arbitrary", "arbitrary"),
        name="proj",
    )(h, w)


def _split3(v):
    hi = v.astype(BF16)
    r = v - hi.astype(F32)
    mid = r.astype(BF16)
    lo = (r - mid.astype(F32)).astype(BF16)
    return hi, mid, lo


def _pair(v, j, first):
    return jnp.where(first, v[:, 2 * j:2 * j + 1], v[:, 2 * j + 1:2 * j + 2])


def _ssd_kernel(xbc_ref, dt_ref, z_ref, cw_ref, cb_ref, dtb_ref, alog_ref, dsk_ref, nw_ref,
                y_ref, xpad, state):
    L = SSD_CHUNK
    d_inner = z_ref.shape[1]
    n_state = SSD_STATE
    gw = d_inner // SSD_GROUPS
    pairs_per_group = gw // LANES
    halo = SUBLANES

    @pl.when(pl.program_id(1) == 0)
    def _():
        xpad[0:halo, :] = jnp.zeros((halo, xpad.shape[1]), F32)
        state[...] = jnp.zeros_like(state)

    xcur = xbc_ref[...]
    xpad[halo:halo + L, :] = xcur
    conv = cb_ref[...] + cw_ref[SSD_CONV - 1:SSD_CONV, :] * xcur
    for k in range(SSD_CONV - 1):
        conv = conv + cw_ref[k:k + 1, :] * xpad[pl.ds(halo - (SSD_CONV - 1) + k, L), :]
    xpad[0:halo, :] = xcur[L - halo:, :]
    act = _silu(conv)
    xs = act[:, :d_inner]
    bm = act[:, d_inner:d_inner + SSD_GROUPS * n_state]
    cm = act[:, d_inner + SSD_GROUPS * n_state:]

    dtr = dt_ref[...] + dtb_ref[...]
    dt = jnp.maximum(dtr, 0.0) + jnp.log1p(jnp.exp(-jnp.abs(dtr)))
    da = dt * (-jnp.exp(alog_ref[...]))
    row = lax.broadcasted_iota(jnp.int32, (L, L), 0)
    col = lax.broadcasted_iota(jnp.int32, (L, L), 1)
    causal = row >= col
    tril = jnp.where(causal, 1.0, 0.0).astype(BF16)
    cs = sum(_dot(tril, part) for part in _split3(da))
    cs_t = cs.T
    cs_last = cs[L - 1:L, :]
    ecs = jnp.exp(cs)
    dte = jnp.exp(cs_last - cs)
    ecl = jnp.exp(cs_last)

    lane = lax.broadcasted_iota(jnp.int32, (L, LANES), 1)
    first = lane < SSD_HEAD_DIM
    first_row = first[0:1, :]
    dsk = dsk_ref[...]

    y_parts = []
    for g in range(SSD_GROUPS):
        bm_g = bm[:, g * n_state:(g + 1) * n_state]
        cm_g = cm[:, g * n_state:(g + 1) * n_state].astype(BF16)
        cb = _dot_nt(cm_g, bm_g.astype(BF16))
        prev = state[g]
        y_off = _dot(cm_g, prev.astype(BF16))
        w_parts = []
        cdec_parts = []
        for jj in range(pairs_per_group):
            j = g * pairs_per_group + jj
            h0, h1 = 2 * j, 2 * j + 1
            xs_p = xs[:, j * LANES:(j + 1) * LANES]
            xdt = xs_p * _pair(dt, j, first)
            dec0 = jnp.exp(jnp.where(causal, cs[:, h0:h0 + 1] - cs_t[h0:h0 + 1, :], -jnp.inf))
            dec1 = jnp.exp(jnp.where(causal, cs[:, h1:h1 + 1] - cs_t[h1:h1 + 1, :], -jnp.inf))
            lhs = jnp.concatenate([(cb * dec0).astype(BF16), (cb * dec1).astype(BF16)], axis=1)
            xb = xdt.astype(BF16)
            zero = jnp.zeros_like(xb)
            rhs = jnp.concatenate([jnp.where(first, xb, zero), jnp.where(first, zero, xb)], axis=0)
            y_p = (_dot(lhs, rhs)
                   + _pair(ecs, j, first) * y_off[:, jj * LANES:(jj + 1) * LANES]
                   + dsk[:, j * LANES:(j + 1) * LANES] * xs_p)
            y_parts.append(y_p)
            w_parts.append((xdt * _pair(dte, j, first)).astype(BF16))
            cdec_parts.append(_pair(ecl, j, first_row))
        w_g = jnp.concatenate(w_parts, axis=1)
        st_new = _dot(bm_g.T.astype(BF16), w_g)
        state[g] = jnp.concatenate(cdec_parts, axis=1) * prev + st_new

    y = jnp.concatenate(y_parts, axis=1)
    yz = y * _silu(z_ref[...])
    outs = []
    for g in range(SSD_GROUPS):
        seg = yz[:, g * gw:(g + 1) * gw]
        ms = jnp.mean(seg * seg, axis=-1, keepdims=True)
        outs.append(seg * lax.rsqrt(ms + EPS))
    y_ref[...] = (jnp.concatenate(outs, axis=1) * nw_ref[...]).astype(y_ref.dtype)


def _ssd(xbc, dtp, z, conv_w, conv_b, dt_bias, a_log, d_ssd, norm_w, bsz, seq):
    m, d_inner = z.shape
    conv_dim = xbc.shape[1]
    L = SSD_CHUNK
    nc = seq // L
    heads = d_inner // SSD_HEAD_DIM
    pad = LANES - heads
    row = lambda b, c: (b * nc + c, 0)
    const = lambda b, c: (0, 0)
    return pl.pallas_call(
        _ssd_kernel,
        out_shape=jax.ShapeDtypeStruct((m, d_inner), BF16),
        grid=(bsz, nc),
        in_specs=[pl.BlockSpec((L, conv_dim), row),
                  pl.BlockSpec((L, LANES), row),
                  pl.BlockSpec((L, d_inner), row),
                  pl.BlockSpec((SSD_CONV, conv_dim), const),
                  pl.BlockSpec((1, conv_dim), const),
                  pl.BlockSpec((1, LANES), const),
                  pl.BlockSpec((1, LANES), const),
                  pl.BlockSpec((1, d_inner), const),
                  pl.BlockSpec((1, d_inner), const)],
        out_specs=pl.BlockSpec((L, d_inner), row),
        scratch_shapes=[pltpu.VMEM((L + SUBLANES, conv_dim), F32),
                        pltpu.VMEM((SSD_GROUPS, SSD_STATE, d_inner // SSD_GROUPS), F32)],
        compiler_params=_params("arbitrary", "arbitrary"),
        name="ssd",
    )(xbc, dtp, z, conv_w, conv_b.reshape(1, conv_dim),
      jnp.pad(dt_bias, (0, pad)).reshape(1, LANES), jnp.pad(a_log, (0, pad)).reshape(1, LANES),
      jnp.repeat(d_ssd, SSD_HEAD_DIM).reshape(1, d_inner), norm_w.reshape(1, d_inner))


def _s5_prep_kernel(lre_ref, lim_ref, ldt_ref, btr_ref, bti_ref, cnr_ref, cni_ref,
                    wend_ref, wcorr_ref, krev_ref, al_ref):
    L = S5_CHUNK
    q = S5_SLAB
    ns = S5_SLAB_STATE
    lr = jnp.minimum(lre_ref[...], -1e-4)
    li = lim_ref[...]
    dt = jnp.exp(ldt_ref[...])

    def power(k):
        mag = jnp.exp(lr * dt * float(k))
        ang = li * dt * float(k)
        return mag * jnp.cos(ang), mag * jnp.sin(ang)

    ar, ai = power(1)
    den = lr * lr + li * li
    nr = ar - 1.0
    kr = (nr * lr + ai * li) / den
    ki = (ai * lr - nr * li) / den

    rows = lax.broadcasted_iota(jnp.int32, (q, ns), 0) // S5_GROUP_SIZE
    cols = lax.broadcasted_iota(jnp.int32, (q, ns), 1) // S5_STATE
    same = rows == cols

    def block_diag(ref):
        return jnp.where(same, jnp.concatenate([ref[...]] * (ns // LANES), axis=1), 0.0)

    btr, bti = block_diag(btr_ref), block_diag(bti_ref)
    bbr = btr * kr - bti * ki
    bbi = btr * ki + bti * kr
    cr, ci = block_diag(cnr_ref), block_diag(cni_ref)

    for k in range(L):
        pr, pi = power(k)
        blk = jnp.concatenate([bbr * pr - bbi * pi, bbr * pi + bbi * pr], axis=1)
        wend_ref[(L - 1 - k) * q:(L - k) * q, :] = blk.astype(BF16)
    for l in range(L):
        pr, pi = power(l + 1)
        blk = jnp.concatenate([cr * pr - ci * pi, -(cr * pi) - ci * pr], axis=1)
        wcorr_ref[l * q:(l + 1) * q, :] = blk.astype(BF16)
    cstack = jnp.concatenate([cr, -ci], axis=1).astype(BF16)
    krev_ref[...] = _dot_nt(wend_ref[...], cstack).astype(BF16)
    pr, pi = power(L)
    al_ref[...] = jnp.concatenate([pr, pi], axis=1)


def _s5_prep(lam_re, lam_im, log_dt, b_re, b_im, c_re, c_im):
    g, p = lam_re.shape
    width = g * S5_GROUP_SIZE
    n_slabs = width // S5_SLAB
    ns = S5_SLAB_STATE
    lq = S5_CHUNK * S5_SLAB
    row_vec = lambda v: v.reshape(1, g * p)
    twice = lambda v: jnp.concatenate([v, v], axis=1)
    bt = lambda v: twice(jnp.swapaxes(v, 1, 2).reshape(width, p))
    cn = lambda v: twice(v.reshape(width, p))
    vec = pl.BlockSpec((None, 1, ns), lambda s: (s, 0, 0))
    mat = pl.BlockSpec((S5_SLAB, LANES), lambda s: (s, 0))
    vec3 = lambda v: row_vec(v).reshape(n_slabs, 1, ns)
    return pl.pallas_call(
        _s5_prep_kernel,
        out_shape=[jax.ShapeDtypeStruct((n_slabs, lq, 2 * ns), BF16),
                   jax.ShapeDtypeStruct((n_slabs, lq, 2 * ns), BF16),
                   jax.ShapeDtypeStruct((n_slabs, lq, S5_SLAB), BF16),
                   jax.ShapeDtypeStruct((n_slabs, 1, 2 * ns), F32)],
        grid=(n_slabs,),
        in_specs=[vec, vec, vec, mat, mat, mat, mat],
        out_specs=[pl.BlockSpec((None, lq, 2 * ns), lambda s: (s, 0, 0)),
                   pl.BlockSpec((None, lq, 2 * ns), lambda s: (s, 0, 0)),
                   pl.BlockSpec((None, lq, S5_SLAB), lambda s: (s, 0, 0)),
                   pl.BlockSpec((None, 1, 2 * ns), lambda s: (s, 0, 0))],
        compiler_params=_params("arbitrary"),
        name="s5prep",
    )(vec3(lam_re), vec3(lam_im), vec3(jnp.repeat(log_dt, p)),
      bt(b_re), bt(b_im), cn(c_re), cn(c_im))


def _gelu_tanh(v):
    return 0.5 * v * (1.0 + jnp.tanh(math.sqrt(2.0 / math.pi) * (v + 0.044715 * (v * v * v))))


def _s5_kernel(u_ref, wend_ref, wcorr_ref, krev_ref, al_ref, d_ref, y_ref, e_scr, cin_scr, st_scr,
               *, blocks_per_seq):
    L = S5_CHUNK
    q = S5_SLAB
    ns = S5_SLAB_STATE
    rows = u_ref.shape[0]

    @pl.when(pl.program_id(1) % blocks_per_seq == 0)
    def _():
        st_scr[...] = jnp.zeros_like(st_scr)

    u = u_ref[...]
    ub = u.astype(BF16)
    e_scr[...] = _dot(ub, wend_ref[...])
    alr = al_ref[:, :ns]
    ali = al_ref[:, ns:]

    def tile_step(t, st):
        base = pl.multiple_of(t * SUBLANES, SUBLANES)
        e = e_scr[pl.ds(base, SUBLANES), :]
        sr, si = st
        carried = []
        for r in range(SUBLANES):
            carried.append(jnp.concatenate([sr, si], axis=1))
            er = e[r:r + 1, :ns]
            ei = e[r:r + 1, ns:]
            sr, si = alr * sr - ali * si + er, alr * si + ali * sr + ei
        cin_scr[pl.ds(base, SUBLANES), :] = jnp.concatenate(carried, axis=0)
        return sr, si

    sr, si = lax.fori_loop(0, rows // SUBLANES, tile_step, (st_scr[:, :ns], st_scr[:, ns:]))
    st_scr[...] = jnp.concatenate([sr, si], axis=1)

    corr = _dot_nt(cin_scr[...].astype(BF16), wcorr_ref[...])
    intra = [_dot(ub[:, :(l + 1) * q], krev_ref[(L - 1 - l) * q:, :]) for l in range(L)]
    y = jnp.concatenate(intra, axis=1) + corr + d_ref[...] * u
    y_ref[...] = _gelu_tanh(y).astype(y_ref.dtype)


def _s5(u_slabs, prep, d_skip, bsz, seq):
    wend, wcorr, krev, al = prep
    n_slabs, m, q = u_slabs.shape
    L = S5_CHUNK
    lq = L * q
    ns2 = 2 * S5_SLAB_STATE
    chunks = m // L
    chunks_per_seq = seq // L
    rb = min(256, chunks_per_seq)
    blocks_per_seq = chunks_per_seq // rb
    u2 = u_slabs.reshape(n_slabs, chunks, lq)
    d_t = jnp.tile(d_skip.reshape(n_slabs, 1, q), (1, 1, L))
    slab = lambda s, r: (s, 0, 0)
    blk = lambda s, r: (s, r, 0)
    y2 = pl.pallas_call(
        functools.partial(_s5_kernel, blocks_per_seq=blocks_per_seq),
        out_shape=jax.ShapeDtypeStruct((n_slabs, chunks, lq), BF16),
        grid=(n_slabs, chunks // rb),
        in_specs=[pl.BlockSpec((None, rb, lq), blk),
                  pl.BlockSpec((None, lq, ns2), slab),
                  pl.BlockSpec((None, lq, ns2), slab),
                  pl.BlockSpec((None, lq, q), slab),
                  pl.BlockSpec((None, 1, ns2), slab),
                  pl.BlockSpec((None, 1, lq), slab)],
        out_specs=pl.BlockSpec((None, rb, lq), blk),
        scratch_shapes=[pltpu.VMEM((rb, ns2), F32), pltpu.VMEM((rb, ns2), F32),
                        pltpu.VMEM((1, ns2), F32)],
        compiler_params=_params("arbitrary", "arbitrary"),
        name="s5",
    )(u2, wend, wcorr, krev, al, d_t)
    return y2.reshape(n_slabs, m, q)


def _merge_kernel(ya_ref, yb_ref, ga_ref, gb_ref, x_ref, wa_ref, wg_ref, wo_ref, mods_ref, nw_ref,
                  xo_ref, h_ref, *, k_gate, k_next):
    d = x_ref.shape[1]
    p_a = _dot(ya_ref[...], wa_ref[...])
    glu = sum(_dot(yb_ref[s], wg_ref[s]) for s in range(yb_ref.shape[0]))
    p_b = glu[:, :d] * jax.nn.sigmoid(glu[:, d:])
    merged = jax.nn.sigmoid(ga_ref[...]) * p_a + jax.nn.sigmoid(gb_ref[...]) * p_b
    xn = x_ref[...] + mods_ref[k_gate:k_gate + 1, :] * _dot(merged.astype(BF16), wo_ref[...])
    xo_ref[...] = xn
    hn = _mod_rms(xn, nw_ref[...], mods_ref[k_next:k_next + 1, :], mods_ref[k_next + 1:k_next + 2, :])
    h_ref[...] = hn.astype(BF16)


def _merge(ya, yb_slabs, ga, gb, x2, w_a, w_glu, w_o, mods3, nw_next, k_gate, k_next, seq):
    m, d = x2.shape
    n_slabs, _, q = yb_slabs.shape
    tm = min(256, seq)
    per_b = seq // tm
    row = lambda i: (i, 0)
    const = lambda i: (0, 0)
    once = pl.Buffered(1)
    return pl.pallas_call(
        functools.partial(_merge_kernel, k_gate=k_gate, k_next=k_next),
        out_shape=[jax.ShapeDtypeStruct((m, d), F32), jax.ShapeDtypeStruct((m, d), BF16)],
        grid=(m // tm,),
        in_specs=[pl.BlockSpec((tm, ya.shape[1]), row),
                  pl.BlockSpec((n_slabs, tm, q), lambda i: (0, i, 0)),
                  pl.BlockSpec((tm, d), row),
                  pl.BlockSpec((tm, d), row),
                  pl.BlockSpec((tm, d), row),
                  pl.BlockSpec(w_a.shape, const, pipeline_mode=once),
                  pl.BlockSpec(w_glu.shape, lambda i: (0, 0, 0), pipeline_mode=once),
                  pl.BlockSpec(w_o.shape, const, pipeline_mode=once),
                  pl.BlockSpec((None, N_ADA, d), lambda i: (i // per_b, 0, 0)),
                  pl.BlockSpec((1, d), const)],
        out_specs=[pl.BlockSpec((tm, d), row), pl.BlockSpec((tm, d), row)],
        compiler_params=_params("arbitrary"),
        name="merge",
    )(ya, yb_slabs, ga, gb, x2, w_a, w_glu, w_o, mods3, nw_next.reshape(1, d))


def _ffn_weights(w_in, w_out):
    ff = w_out.shape[0]
    ffp = -(-ff // 512) * 512
    wa = jnp.pad(w_in[:, :ff].astype(BF16), ((0, 0), (0, ffp - ff)))
    wb = jnp.pad(w_in[:, ff:].astype(BF16), ((0, 0), (0, ffp - ff)))
    wo = jnp.pad(w_out.astype(BF16), ((0, ffp - ff), (0, 0)))
    return wa, wb, wo


def kernel(x, c, w_ada, b_ada, norm_ffn1, w_ffn1_in, w_ffn1_out, norm_mix, w_in, conv_w, conv_b, dt_bias, a_log, d_ssd, ssd_norm_w, w_a_proj, s5_lambda_re, s5_lambda_im, s5_b_re, s5_b_im, s5_c_re, s5_c_im, s5_d, s5_log_dt, w_b_glu, w_out, norm_ffn2, w_ffn2_in, w_ffn2_out, norm_final):
    bsz, seq, d = x.shape
    depth = w_ada.shape[0]
    m = bsz * seq
    d_inner = ssd_norm_w.shape[1]
    conv_dim = conv_w.shape[2]
    heads = dt_bias.shape[1]
    s5_width = w_b_glu.shape[1]
    off_xbc = d_inner
    off_dt = off_xbc + conv_dim
    off_u = off_dt + heads
    off_g = off_u + s5_width

    assert depth == 1, "the epilogue fusion below is written for a single layer"
    l = 0
    x2 = x.reshape(m, d)
    mods3 = _mods(c, w_ada[l], b_ada[l]).reshape(bsz, N_ADA, d)
    h = _prenorm(x2, norm_ffn1[l], mods3, 0, seq)

    x2, h = _ffn(h, x2, *_ffn_weights(w_ffn1_in[l], w_ffn1_out[l]), mods3, norm_mix[l], 2, 3, seq)

    wi = w_in[l].astype(BF16)
    z = _proj(h, wi[:, :off_xbc], 1024)
    xbc = _proj(h, wi[:, off_xbc:off_dt], 1024)
    dtp = _proj(h, jnp.pad(wi[:, off_dt:off_u], ((0, 0), (0, LANES - heads))), LANES)
    u = _proj(h, wi[:, off_u:off_g], S5_SLAB, slabs=True)
    ga = _proj(h, wi[:, off_g:off_g + d], 1024)
    gb = _proj(h, wi[:, off_g + d:], 1024)

    ya = _ssd(xbc, dtp, z, conv_w[l], conv_b[l], dt_bias[l], a_log[l], d_ssd[l], ssd_norm_w[l],
              bsz, seq)
    prep = _s5_prep(s5_lambda_re[l], s5_lambda_im[l], s5_log_dt[l],
                    s5_b_re[l], s5_b_im[l], s5_c_re[l], s5_c_im[l])
    yb = _s5(u, prep, s5_d[l], bsz, seq)

    x2, h = _merge(ya, yb, ga, gb, x2, w_a_proj[l].astype(BF16),
                   w_b_glu[l].astype(BF16).reshape(s5_width // S5_SLAB, S5_SLAB, 2 * d),
                   w_out[l].astype(BF16), mods3, norm_ffn2[l], 5, 6, seq)

    (x2,) = _ffn(h, x2, *_ffn_weights(w_ffn2_in[l], w_ffn2_out[l]), mods3, norm_final, 8, None, seq)
    return x2.reshape(bsz, seq, d)
```

```python
import functools
import math

import jax
import jax.numpy as jnp
from jax import lax
from jax.experimental import pallas as pl
from jax.experimental.pallas import tpu as pltpu

F32 = jnp.float32
BF16 = jnp.bfloat16
EPS = 1e-6

VMEM_LIMIT_BYTES = 56 * 1024 * 1024
LANES = 128
SUBLANES = 8

SSD_HEAD_DIM = 64
SSD_GROUPS = 4
SSD_STATE = 128
SSD_CONV = 4
SSD_CHUNK = 128
S5_GROUP_SIZE = 16
S5_STATE = 64
S5_SLAB = 256
S5_GROUPS_PER_SLAB = S5_SLAB // S5_GROUP_SIZE
S5_SLAB_STATE = S5_GROUPS_PER_SLAB * S5_STATE
S5_CHUNK = 8
N_ADA = 9


def _params(*sem):
    return pltpu.CompilerParams(dimension_semantics=sem, vmem_limit_bytes=VMEM_LIMIT_BYTES)


def _silu(v):
    return v * jax.nn.sigmoid(v)


def _dot(a, b):
    return jnp.dot(a, b, preferred_element_type=F32)


def _dot_nt(a, b):
    return lax.dot_general(a, b, (((1,), (1,)), ((), ())), preferred_element_type=F32)


def _mod_rms(x, nw, shift, scale):
    ms = jnp.mean(x * x, axis=-1, keepdims=True)
    return (x * lax.rsqrt(ms + EPS) * nw) * (1.0 + scale) + shift


def _mods_kernel(ct_ref, w_ref, b_ref, o_ref):
    ca = _silu(ct_ref[...])
    w = w_ref[...]
    rows = [jnp.sum(ca[:, b:b + 1] * w, axis=0, keepdims=True) for b in range(ca.shape[1])]
    o_ref[...] = jnp.concatenate(rows, axis=0) + b_ref[...]


def _mods(c, w_ada, b_ada):
    bsz, d = c.shape
    n = w_ada.shape[1]
    tn = 1024
    return pl.pallas_call(
        _mods_kernel,
        out_shape=jax.ShapeDtypeStruct((bsz, n), F32),
        grid=(n // tn,),
        in_specs=[pl.BlockSpec((d, bsz), lambda j: (0, 0)),
                  pl.BlockSpec((d, tn), lambda j: (0, j)),
                  pl.BlockSpec((1, tn), lambda j: (0, j))],
        out_specs=pl.BlockSpec((bsz, tn), lambda j: (0, j)),
        compiler_params=_params("arbitrary"),
        name="mods",
    )(c.T, w_ada, b_ada.reshape(1, n))


def _prenorm_kernel(x_ref, nw_ref, mods_ref, h_ref, *, k):
    h = _mod_rms(x_ref[...], nw_ref[...], mods_ref[k:k + 1, :], mods_ref[k + 1:k + 2, :])
    h_ref[...] = h.astype(h_ref.dtype)


def _prenorm(x2, nw, mods3, k, seq):
    m, d = x2.shape
    tm = min(512, seq)
    per_b = seq // tm
    return pl.pallas_call(
        functools.partial(_prenorm_kernel, k=k),
        out_shape=jax.ShapeDtypeStruct((m, d), BF16),
        grid=(m // tm,),
        in_specs=[pl.BlockSpec((tm, d), lambda i: (i, 0)),
                  pl.BlockSpec((1, d), lambda i: (0, 0)),
                  pl.BlockSpec((None, N_ADA, d), lambda i: (i // per_b, 0, 0))],
        out_specs=pl.BlockSpec((tm, d), lambda i: (i, 0)),
        compiler_params=_params("arbitrary"),
        name="prenorm",
    )(x2, nw.reshape(1, d), mods3)


def _ffn_kernel(h_ref, x_ref, wa_ref, wb_ref, wo_ref, mods_ref, nw_ref, *out_refs, k_gate, k_next):
    j = pl.program_id(1)
    xo_ref = out_refs[0]

    @pl.when(j == 0)
    def _():
        xo_ref[...] = jnp.zeros_like(xo_ref)

    h = h_ref[...]
    a = _dot(h, wa_ref[...])
    b = _dot(h, wb_ref[...])
    act = (_silu(a) * b).astype(BF16)
    xo_ref[...] += _dot(act, wo_ref[...])

    @pl.when(j == pl.num_programs(1) - 1)
    def _():
        xn = x_ref[...] + 0.5 * mods_ref[k_gate:k_gate + 1, :] * xo_ref[...]
        if k_next is None:
            ms = jnp.mean(xn * xn, axis=-1, keepdims=True)
            xo_ref[...] = xn * lax.rsqrt(ms + EPS) * nw_ref[...]
        else:
            xo_ref[...] = xn
            hn = _mod_rms(xn, nw_ref[...], mods_ref[k_next:k_next + 1, :],
                          mods_ref[k_next + 1:k_next + 2, :])
            out_refs[1][...] = hn.astype(BF16)


def _ffn(h, x2, wa, wb, wo, mods3, nw_next, k_gate, k_next, seq):
    m, d = x2.shape
    ffp = wa.shape[1]
    tm = min(512, seq)
    tf = 512
    per_b = seq // tm
    row = lambda i, j: (i, 0)
    out_shape = [jax.ShapeDtypeStruct((m, d), F32)]
    out_specs = [pl.BlockSpec((tm, d), row)]
    if k_next is not None:
        out_shape.append(jax.ShapeDtypeStruct((m, d), BF16))
        out_specs.append(pl.BlockSpec((tm, d), row))
    return pl.pallas_call(
        functools.partial(_ffn_kernel, k_gate=k_gate, k_next=k_next),
        out_shape=out_shape,
        grid=(m // tm, ffp // tf),
        in_specs=[pl.BlockSpec((tm, d), row),
                  pl.BlockSpec((tm, d), row),
                  pl.BlockSpec((d, tf), lambda i, j: (0, j)),
                  pl.BlockSpec((d, tf), lambda i, j: (0, j)),
                  pl.BlockSpec((tf, d), lambda i, j: (j, 0)),
                  pl.BlockSpec((None, N_ADA, d), lambda i, j: (i // per_b, 0, 0)),
                  pl.BlockSpec((1, d), lambda i, j: (0, 0))],
        out_specs=out_specs,
        compiler_params=_params("arbitrary", "arbitrary"),
        name="ffn",
    )(h, x2, wa, wb, wo, mods3, nw_next.reshape(1, d))


PROJ_ROWS = 1024
PROJ_SUB = 256


def _proj_kernel(h_ref, w_ref, o_ref, *, act):
    y = _dot(h_ref[...], w_ref[...])
    if act:
        y = _silu(y)
    o_ref[...] = y.astype(o_ref.dtype)


def _proj(h, w, tn, act=False):
    m, d = h.shape
    n = w.shape[1]
    tm = min(PROJ_ROWS, m)
    return pl.pallas_call(
        functools.partial(_proj_kernel, act=act),
        out_shape=jax.ShapeDtypeStruct((m, n), F32),
        grid=(m // tm, n // tn),
        in_specs=[pl.BlockSpec((tm, d), lambda i, j: (i, 0)),
                  pl.BlockSpec((d, tn), lambda i, j: (0, j))],
        out_specs=pl.BlockSpec((tm, tn), lambda i, j: (i, j)),
        compiler_params=_params("arbitrary", "arbitrary"),
        name="proj",
    )(h, w)


def _proj_conv_kernel(h_ref, w_ref, cw_ref, cb_ref, o_ref, halo, *, tiles_per_seq):
    i = pl.program_id(0)
    j = pl.program_id(1)
    tm = h_ref.shape[0]
    sub = min(PROJ_SUB, tm)
    taps = SSD_CONV - 1

    @pl.when(i % tiles_per_seq == 0)
    def _():
        halo[j] = jnp.zeros(halo.shape[1:], F32)

    prev = halo[j]
    w = w_ref[...]
    cw = cw_ref[...]
    cb = cb_ref[...]
    for r in range(tm // sub):
        raw = _dot(h_ref[r * sub:(r + 1) * sub, :], w)
        ext = jnp.concatenate([prev, raw], axis=0)
        conv = cb + cw[taps:taps + 1, :] * raw
        for k in range(taps):
            lo = SUBLANES - taps + k
            conv = conv + cw[k:k + 1, :] * ext[lo:lo + sub, :]
        o_ref[r * sub:(r + 1) * sub, :] = _silu(conv).astype(o_ref.dtype)
        prev = raw[sub - SUBLANES:, :]
    halo[j] = prev


def _proj_conv(h, w, conv_w, conv_b, tn, seq, out_dtype):
    m, d = h.shape
    n = w.shape[1]
    tm = min(PROJ_ROWS, seq)
    return pl.pallas_call(
        functools.partial(_proj_conv_kernel, tiles_per_seq=seq // tm),
        out_shape=jax.ShapeDtypeStruct((m, n), out_dtype),
        grid=(m // tm, n // tn),
        in_specs=[pl.BlockSpec((tm, d), lambda i, j: (i, 0)),
                  pl.BlockSpec((d, tn), lambda i, j: (0, j)),
                  pl.BlockSpec((SSD_CONV, tn), lambda i, j: (0, j)),
                  pl.BlockSpec((1, tn), lambda i, j: (0, j))],
        out_specs=pl.BlockSpec((tm, tn), lambda i, j: (i, j)),
        scratch_shapes=[pltpu.VMEM((n // tn, SUBLANES, tn), F32)],
        compiler_params=_params("arbitrary", "arbitrary"),
        name="projconv",
    )(h, w, conv_w, conv_b.reshape(1, n))


def _proj_chunked_kernel(h_ref, w_ref, o_ref, scr):
    n_slabs, rows, lq = o_ref.shape
    q = S5_SLAB
    L = lq // q
    y = _dot(h_ref[...], w_ref[...])
    for c in range(scr.shape[0]):
        scr[c] = y[:, c * LANES:(c + 1) * LANES]
    for c in range(scr.shape[0]):
        s, off = divmod(c * LANES, q)
        for l in range(L):
            o_ref[s, :, l * q + off:l * q + off + LANES] = scr[c, pl.ds(l, rows, stride=L), :]


def _proj_chunked(h, w):
    m, d = h.shape
    n = w.shape[1]
    tm = min(PROJ_ROWS, m)
    L = S5_CHUNK
    n_slabs = n // S5_SLAB
    return pl.pallas_call(
        _proj_chunked_kernel,
        out_shape=jax.ShapeDtypeStruct((n_slabs, m // L, L * S5_SLAB), F32),
        grid=(m // tm,),
        in_specs=[pl.BlockSpec((tm, d), lambda i: (i, 0)),
                  pl.BlockSpec((d, n), lambda i: (0, 0))],
        out_specs=pl.BlockSpec((n_slabs, tm // L, L * S5_SLAB), lambda i: (0, i, 0)),
        scratch_shapes=[pltpu.VMEM((n // LANES, tm, LANES), F32)],
        compiler_params=_params("arbitrary"),
        name="projchunk",
    )(h, w)


def _split3(v):
    hi = v.astype(BF16)
    r = v - hi.astype(F32)
    mid = r.astype(BF16)
    lo = (r - mid.astype(F32)).astype(BF16)
    return hi, mid, lo


def _pair(v, j, first):
    return jnp.where(first, v[:, 2 * j:2 * j + 1], v[:, 2 * j + 1:2 * j + 2])


def _ssd_kernel(xs_ref, bc_ref, dt_ref, zs_ref, dtb_ref, alog_ref, dsk_ref, nw_ref, y_ref, state):
    L = SSD_CHUNK
    d_inner = xs_ref.shape[1]
    n_state = SSD_STATE
    gw = d_inner // SSD_GROUPS
    pairs_per_group = gw // LANES

    @pl.when(pl.program_id(1) == 0)
    def _():
        state[...] = jnp.zeros_like(state)

    dtr = dt_ref[...] + dtb_ref[...]
    dt = jnp.maximum(dtr, 0.0) + jnp.log1p(jnp.exp(-jnp.abs(dtr)))
    da = dt * (-jnp.exp(alog_ref[...]))
    row = lax.broadcasted_iota(jnp.int32, (L, L), 0)
    col = lax.broadcasted_iota(jnp.int32, (L, L), 1)
    causal = row >= col
    tril = jnp.where(causal, 1.0, 0.0).astype(BF16)
    cs = sum(_dot(tril, part) for part in _split3(da))
    cs_last = cs[L - 1:L, :]
    ecs = jnp.exp(cs)
    ecl = jnp.exp(cs_last)
    cs_t = cs.T
    dt_t = dt.T
    wst_t = (dt * jnp.exp(cs_last - cs)).T

    lane = lax.broadcasted_iota(jnp.int32, (L, LANES), 1)
    first = lane < SSD_HEAD_DIM
    first_row = first[0:1, :]

    y_parts = []
    for g in range(SSD_GROUPS):
        bm_g = bc_ref[:, g * n_state:(g + 1) * n_state]
        cm_g = bc_ref[:, (SSD_GROUPS + g) * n_state:(SSD_GROUPS + g + 1) * n_state]
        cb = _dot_nt(cm_g, bm_g)
        bm_t = bm_g.astype(F32).T
        y_off = _dot(cm_g, state[g].astype(BF16))
        for jj in range(pairs_per_group):
            j = g * pairs_per_group + jj
            h0, h1 = 2 * j, 2 * j + 1
            cols = slice(j * LANES, (j + 1) * LANES)
            gcols = slice(jj * LANES, (jj + 1) * LANES)
            xs_p = xs_ref[:, cols]
            xb = xs_p.astype(BF16)
            zero = jnp.zeros_like(xb)
            rhs = jnp.concatenate([jnp.where(first, xb, zero), jnp.where(first, zero, xb)], axis=0)
            lhs_y = []
            lhs_s = []
            for h in (h0, h1):
                dec = jnp.exp(jnp.where(causal, cs[:, h:h + 1] - cs_t[h:h + 1, :], -jnp.inf))
                lhs_y.append((cb * dec * dt_t[h:h + 1, :]).astype(BF16))
                lhs_s.append((bm_t * wst_t[h:h + 1, :]).astype(BF16))
            y_p = (_dot(jnp.concatenate(lhs_y, axis=1), rhs)
                   + _pair(ecs, j, first) * y_off[:, gcols]
                   + dsk_ref[:, cols] * xs_p)
            y_parts.append(y_p)
            st_new = _dot(jnp.concatenate(lhs_s, axis=1), rhs)
            state[g, :, gcols] = _pair(ecl, j, first_row) * state[g, :, gcols] + st_new

    yz = jnp.concatenate(y_parts, axis=1) * zs_ref[...]
    outs = []
    for g in range(SSD_GROUPS):
        seg = yz[:, g * gw:(g + 1) * gw]
        ms = jnp.mean(seg * seg, axis=-1, keepdims=True)
        outs.append(seg * lax.rsqrt(ms + EPS))
    y_ref[...] = (jnp.concatenate(outs, axis=1) * nw_ref[...]).astype(y_ref.dtype)


def _ssd(xs, bc, dtp, zs, dt_bias, a_log, d_ssd, norm_w, bsz, seq):
    m, d_inner = xs.shape
    L = SSD_CHUNK
    nc = seq // L
    heads = d_inner // SSD_HEAD_DIM
    pad = LANES - heads
    row = lambda b, c: (b * nc + c, 0)
    const = lambda b, c: (0, 0)
    return pl.pallas_call(
        _ssd_kernel,
        out_shape=jax.ShapeDtypeStruct((m, d_inner), BF16),
        grid=(bsz, nc),
        in_specs=[pl.BlockSpec((L, d_inner), row),
                  pl.BlockSpec((L, bc.shape[1]), row),
                  pl.BlockSpec((L, LANES), row),
                  pl.BlockSpec((L, d_inner), row),
                  pl.BlockSpec((1, LANES), const),
                  pl.BlockSpec((1, LANES), const),
                  pl.BlockSpec((1, d_inner), const),
                  pl.BlockSpec((1, d_inner), const)],
        out_specs=pl.BlockSpec((L, d_inner), row),
        scratch_shapes=[pltpu.VMEM((SSD_GROUPS, SSD_STATE, d_inner // SSD_GROUPS), F32)],
        compiler_params=_params("arbitrary", "arbitrary"),
        name="ssd",
    )(xs, bc, dtp, zs,
      jnp.pad(dt_bias, (0, pad)).reshape(1, LANES), jnp.pad(a_log, (0, pad)).reshape(1, LANES),
      jnp.repeat(d_ssd, SSD_HEAD_DIM).reshape(1, d_inner), norm_w.reshape(1, d_inner))


def _s5_prep_kernel(lre_ref, lim_ref, ldt_ref, btr_ref, bti_ref, cnr_ref, cni_ref,
                    wend_ref, wcorr_ref, krev_ref, al_ref):
    L = S5_CHUNK
    q = S5_SLAB
    ns = S5_SLAB_STATE
    lr = jnp.minimum(lre_ref[...], -1e-4)
    li = lim_ref[...]
    dt = jnp.exp(ldt_ref[...])

    def power(k):
        mag = jnp.exp(lr * dt * float(k))
        ang = li * dt * float(k)
        return mag * jnp.cos(ang), mag * jnp.sin(ang)

    ar, ai = power(1)
    den = lr * lr + li * li
    nr = ar - 1.0
    kr = (nr * lr + ai * li) / den
    ki = (ai * lr - nr * li) / den

    rows = lax.broadcasted_iota(jnp.int32, (q, ns), 0) // S5_GROUP_SIZE
    cols = lax.broadcasted_iota(jnp.int32, (q, ns), 1) // S5_STATE
    same = rows == cols

    def block_diag(ref):
        return jnp.where(same, jnp.concatenate([ref[...]] * (ns // LANES), axis=1), 0.0)

    btr, bti = block_diag(btr_ref), block_diag(bti_ref)
    bbr = btr * kr - bti * ki
    bbi = btr * ki + bti * kr
    cr, ci = block_diag(cnr_ref), block_diag(cni_ref)

    for k in range(L):
        pr, pi = power(k)
        blk = jnp.concatenate([bbr * pr - bbi * pi, bbr * pi + bbi * pr], axis=1)
        wend_ref[(L - 1 - k) * q:(L - k) * q, :] = blk.astype(BF16)
    for l in range(L):
        pr, pi = power(l + 1)
        blk = jnp.concatenate([cr * pr - ci * pi, -(cr * pi) - ci * pr], axis=1)
        wcorr_ref[l * q:(l + 1) * q, :] = blk.astype(BF16)
    cstack = jnp.concatenate([cr, -ci], axis=1).astype(BF16)
    krev_ref[...] = _dot_nt(wend_ref[...], cstack).astype(BF16)
    pr, pi = power(L)
    al_ref[...] = jnp.concatenate([pr, pi], axis=1)


def _s5_prep(lam_re, lam_im, log_dt, b_re, b_im, c_re, c_im):
    g, p = lam_re.shape
    width = g * S5_GROUP_SIZE
    n_slabs = width // S5_SLAB
    ns = S5_SLAB_STATE
    lq = S5_CHUNK * S5_SLAB
    row_vec = lambda v: v.reshape(1, g * p)
    twice = lambda v: jnp.concatenate([v, v], axis=1)
    bt = lambda v: twice(jnp.swapaxes(v, 1, 2).reshape(width, p))
    cn = lambda v: twice(v.reshape(width, p))
    vec = pl.BlockSpec((None, 1, ns), lambda s: (s, 0, 0))
    mat = pl.BlockSpec((S5_SLAB, LANES), lambda s: (s, 0))
    vec3 = lambda v: row_vec(v).reshape(n_slabs, 1, ns)
    return pl.pallas_call(
        _s5_prep_kernel,
        out_shape=[jax.ShapeDtypeStruct((n_slabs, lq, 2 * ns), BF16),
                   jax.ShapeDtypeStruct((n_slabs, lq, 2 * ns), BF16),
                   jax.ShapeDtypeStruct((n_slabs, lq, S5_SLAB), BF16),
                   jax.ShapeDtypeStruct((n_slabs, 1, 2 * ns), F32)],
        grid=(n_slabs,),
        in_specs=[vec, vec, vec, mat, mat, mat, mat],
        out_specs=[pl.BlockSpec((None, lq, 2 * ns), lambda s: (s, 0, 0)),
                   pl.BlockSpec((None, lq, 2 * ns), lambda s: (s, 0, 0)),
                   pl.BlockSpec((None, lq, S5_SLAB), lambda s: (s, 0, 0)),
                   pl.BlockSpec((None, 1, 2 * ns), lambda s: (s, 0, 0))],
        compiler_params=_params("arbitrary"),
        name="s5prep",
    )(vec3(lam_re), vec3(lam_im), vec3(jnp.repeat(log_dt, p)),
      bt(b_re), bt(b_im), cn(c_re), cn(c_im))


def _gelu_tanh(v):
    return 0.5 * v * (1.0 + jnp.tanh(math.sqrt(2.0 / math.pi) * (v + 0.044715 * (v * v * v))))


def _s5_kernel(u_ref, wend_ref, wcorr_ref, krev_ref, al_ref, d_ref, y_ref, e_scr, cin_scr, st_scr,
               tok_scr, *, blocks_per_seq):
    L = S5_CHUNK
    q = S5_SLAB
    ns = S5_SLAB_STATE
    rows = u_ref.shape[0]

    @pl.when(pl.program_id(1) % blocks_per_seq == 0)
    def _():
        st_scr[...] = jnp.zeros_like(st_scr)

    u = u_ref[...]
    ub = u.astype(BF16)
    e_scr[...] = _dot(ub, wend_ref[...])
    alr = al_ref[:, :ns]
    ali = al_ref[:, ns:]

    def tile_step(t, st):
        base = pl.multiple_of(t * SUBLANES, SUBLANES)
        e = e_scr[pl.ds(base, SUBLANES), :]
        sr, si = st
        carried = []
        for r in range(SUBLANES):
            carried.append(jnp.concatenate([sr, si], axis=1))
            er = e[r:r + 1, :ns]
            ei = e[r:r + 1, ns:]
            sr, si = alr * sr - ali * si + er, alr * si + ali * sr + ei
        cin_scr[pl.ds(base, SUBLANES), :] = jnp.concatenate(carried, axis=0)
        return sr, si

    sr, si = lax.fori_loop(0, rows // SUBLANES, tile_step, (st_scr[:, :ns], st_scr[:, ns:]))
    st_scr[...] = jnp.concatenate([sr, si], axis=1)

    corr = _dot_nt(cin_scr[...].astype(BF16), wcorr_ref[...])
    intra = [_dot(ub[:, :(l + 1) * q], krev_ref[(L - 1 - l) * q:, :]) for l in range(L)]
    y = _gelu_tanh(jnp.concatenate(intra, axis=1) + corr + d_ref[...] * u)
    for c in range(tok_scr.shape[0]):
        for l in range(L):
            lo = l * q + c * LANES
            tok_scr[c, pl.ds(l, rows, stride=L), :] = y[:, lo:lo + LANES]
        y_ref[:, c * LANES:(c + 1) * LANES] = tok_scr[c].astype(y_ref.dtype)


def _s5(u2, prep, d_skip, bsz, seq):
    wend, wcorr, krev, al = prep
    n_slabs, chunks, lq = u2.shape
    L = S5_CHUNK
    q = S5_SLAB
    ns2 = 2 * S5_SLAB_STATE
    chunks_per_seq = seq // L
    rb = min(256, chunks_per_seq)
    blocks_per_seq = chunks_per_seq // rb
    d_t = jnp.tile(d_skip.reshape(n_slabs, 1, q), (1, 1, L))
    slab = lambda s, r: (s, 0, 0)
    blk = lambda s, r: (s, r, 0)
    return pl.pallas_call(
        functools.partial(_s5_kernel, blocks_per_seq=blocks_per_seq),
        out_shape=jax.ShapeDtypeStruct((n_slabs, chunks * L, q), BF16),
        grid=(n_slabs, chunks // rb),
        in_specs=[pl.BlockSpec((None, rb, lq), blk),
                  pl.BlockSpec((None, lq, ns2), slab),
                  pl.BlockSpec((None, lq, ns2), slab),
                  pl.BlockSpec((None, lq, q), slab),
                  pl.BlockSpec((None, 1, ns2), slab),
                  pl.BlockSpec((None, 1, lq), slab)],
        out_specs=pl.BlockSpec((None, rb * L, q), blk),
        scratch_shapes=[pltpu.VMEM((rb, ns2), F32), pltpu.VMEM((rb, ns2), F32),
                        pltpu.VMEM((1, ns2), F32), pltpu.VMEM((q // LANES, rb * L, LANES), F32)],
        compiler_params=_params("arbitrary", "arbitrary"),
        name="s5",
    )(u2, wend, wcorr, krev, al, d_t)


def _merge_kernel(ya_ref, yb_ref, ga_ref, gb_ref, x_ref, wa_ref, wg_ref, wo_ref, mods_ref, nw_ref,
                  xo_ref, h_ref, *, k_gate, k_next):
    d = x_ref.shape[1]
    p_a = _dot(ya_ref[...], wa_ref[...])
    glu = sum(_dot(yb_ref[s], wg_ref[s]) for s in range(yb_ref.shape[0]))
    p_b = glu[:, :d] * jax.nn.sigmoid(glu[:, d:])
    merged = jax.nn.sigmoid(ga_ref[...]) * p_a + jax.nn.sigmoid(gb_ref[...]) * p_b
    xn = x_ref[...] + mods_ref[k_gate:k_gate + 1, :] * _dot(merged.astype(BF16), wo_ref[...])
    xo_ref[...] = xn
    hn = _mod_rms(xn, nw_ref[...], mods_ref[k_next:k_next + 1, :], mods_ref[k_next + 1:k_next + 2, :])
    h_ref[...] = hn.astype(BF16)


def _merge(ya, yb_slabs, ga, gb, x2, w_a, w_glu, w_o, mods3, nw_next, k_gate, k_next, seq):
    m, d = x2.shape
    n_slabs, _, q = yb_slabs.shape
    tm = min(256, seq)
    per_b = seq // tm
    row = lambda i: (i, 0)
    const = lambda i: (0, 0)
    once = pl.Buffered(1)
    return pl.pallas_call(
        functools.partial(_merge_kernel, k_gate=k_gate, k_next=k_next),
        out_shape=[jax.ShapeDtypeStruct((m, d), F32), jax.ShapeDtypeStruct((m, d), BF16)],
        grid=(m // tm,),
        in_specs=[pl.BlockSpec((tm, ya.shape[1]), row),
                  pl.BlockSpec((n_slabs, tm, q), lambda i: (0, i, 0)),
                  pl.BlockSpec((tm, d), row),
                  pl.BlockSpec((tm, d), row),
                  pl.BlockSpec((tm, d), row),
                  pl.BlockSpec(w_a.shape, const, pipeline_mode=once),
                  pl.BlockSpec(w_glu.shape, lambda i: (0, 0, 0), pipeline_mode=once),
                  pl.BlockSpec(w_o.shape, const, pipeline_mode=once),
                  pl.BlockSpec((None, N_ADA, d), lambda i: (i // per_b, 0, 0)),
                  pl.BlockSpec((1, d), const)],
        out_specs=[pl.BlockSpec((tm, d), row), pl.BlockSpec((tm, d), row)],
        compiler_params=_params("arbitrary"),
        name="merge",
    )(ya, yb_slabs, ga, gb, x2, w_a, w_glu, w_o, mods3, nw_next.reshape(1, d))


def _ffn_weights(w_in, w_out):
    ff = w_out.shape[0]
    ffp = -(-ff // 512) * 512
    wa = jnp.pad(w_in[:, :ff].astype(BF16), ((0, 0), (0, ffp - ff)))
    wb = jnp.pad(w_in[:, ff:].astype(BF16), ((0, 0), (0, ffp - ff)))
    wo = jnp.pad(w_out.astype(BF16), ((0, ffp - ff), (0, 0)))
    return wa, wb, wo


def kernel(x, c, w_ada, b_ada, norm_ffn1, w_ffn1_in, w_ffn1_out, norm_mix, w_in, conv_w, conv_b, dt_bias, a_log, d_ssd, ssd_norm_w, w_a_proj, s5_lambda_re, s5_lambda_im, s5_b_re, s5_b_im, s5_c_re, s5_c_im, s5_d, s5_log_dt, w_b_glu, w_out, norm_ffn2, w_ffn2_in, w_ffn2_out, norm_final):
    bsz, seq, d = x.shape
    depth = w_ada.shape[0]
    m = bsz * seq
    d_inner = ssd_norm_w.shape[1]
    conv_dim = conv_w.shape[2]
    heads = dt_bias.shape[1]
    s5_width = w_b_glu.shape[1]
    off_xbc = d_inner
    off_dt = off_xbc + conv_dim
    off_u = off_dt + heads
    off_g = off_u + s5_width

    assert depth == 1, "the epilogue fusion below is written for a single layer"
    l = 0
    x2 = x.reshape(m, d)
    mods3 = _mods(c, w_ada[l], b_ada[l]).reshape(bsz, N_ADA, d)
    h = _prenorm(x2, norm_ffn1[l], mods3, 0, seq)

    x2, h = _ffn(h, x2, *_ffn_weights(w_ffn1_in[l], w_ffn1_out[l]), mods3, norm_mix[l], 2, 3, seq)

    wi = w_in[l].astype(BF16)
    off_bc = off_xbc + d_inner
    zs = _proj(h, wi[:, :off_xbc], 1024, act=True)
    xs = _proj_conv(h, wi[:, off_xbc:off_bc], conv_w[l][:, :d_inner], conv_b[l][:d_inner],
                    1024, seq, F32)
    bc = _proj_conv(h, wi[:, off_bc:off_dt], conv_w[l][:, d_inner:], conv_b[l][d_inner:],
                    1024, seq, BF16)
    dtp = _proj(h, jnp.pad(wi[:, off_dt:off_u], ((0, 0), (0, LANES - heads))), LANES)
    u = _proj_chunked(h, wi[:, off_u:off_g])
    ga = _proj(h, wi[:, off_g:off_g + d], 1024)
    gb = _proj(h, wi[:, off_g + d:], 1024)

    ya = _ssd(xs, bc, dtp, zs, dt_bias[l], a_log[l], d_ssd[l], ssd_norm_w[l], bsz, seq)
    prep = _s5_prep(s5_lambda_re[l], s5_lambda_im[l], s5_log_dt[l],
                    s5_b_re[l], s5_b_im[l], s5_c_re[l], s5_c_im[l])
    yb = _s5(u, prep, s5_d[l], bsz, seq)

    x2, h = _merge(ya, yb, ga, gb, x2, w_a_proj[l].astype(BF16),
                   w_b_glu[l].astype(BF16).reshape(s5_width // S5_SLAB, S5_SLAB, 2 * d),
                   w_out[l].astype(BF16), mods3, norm_ffn2[l], 5, 6, seq)

    (x2,) = _ffn(h, x2, *_ffn_weights(w_ffn2_in[l], w_ffn2_out[l]), mods3, norm_final, 8, None, seq)
    return x2.reshape(bsz, seq, d)
```

```python
import functools
import math

import jax
import jax.numpy as jnp
from jax import lax
from jax.experimental import pallas as pl
from jax.experimental.pallas import tpu as pltpu

F32 = jnp.float32
BF16 = jnp.bfloat16
EPS = 1e-6

VMEM_LIMIT_BYTES = 56 * 1024 * 1024
LANES = 128
SUBLANES = 8

SSD_HEAD_DIM = 64
SSD_GROUPS = 4
SSD_STATE = 128
SSD_CONV = 4
SSD_CHUNK = 128
S5_GROUP_SIZE = 16
S5_STATE = 64
S5_SLAB = 256
S5_GROUPS_PER_SLAB = S5_SLAB // S5_GROUP_SIZE
S5_SLAB_STATE = S5_GROUPS_PER_SLAB * S5_STATE
S5_CHUNK = 8
N_ADA = 9


def _params(*sem):
    return pltpu.CompilerParams(dimension_semantics=sem, vmem_limit_bytes=VMEM_LIMIT_BYTES)


def _silu(v):
    return v * jax.nn.sigmoid(v)


def _dot(a, b):
    return jnp.dot(a, b, preferred_element_type=F32)


def _dot_nt(a, b):
    return lax.dot_general(a, b, (((1,), (1,)), ((), ())), preferred_element_type=F32)


def _mod_rms(x, nw, shift, scale):
    ms = jnp.mean(x * x, axis=-1, keepdims=True)
    return (x * lax.rsqrt(ms + EPS) * nw) * (1.0 + scale) + shift


def _mods_kernel(ct_ref, w_ref, b_ref, o_ref):
    ca = _silu(ct_ref[...])
    w = w_ref[...]
    rows = [jnp.sum(ca[:, b:b + 1] * w, axis=0, keepdims=True) for b in range(ca.shape[1])]
    o_ref[...] = jnp.concatenate(rows, axis=0) + b_ref[...]


def _mods(c, w_ada, b_ada):
    bsz, d = c.shape
    n = w_ada.shape[1]
    tn = 1024
    return pl.pallas_call(
        _mods_kernel,
        out_shape=jax.ShapeDtypeStruct((bsz, n), F32),
        grid=(n // tn,),
        in_specs=[pl.BlockSpec((d, bsz), lambda j: (0, 0)),
                  pl.BlockSpec((d, tn), lambda j: (0, j)),
                  pl.BlockSpec((1, tn), lambda j: (0, j))],
        out_specs=pl.BlockSpec((bsz, tn), lambda j: (0, j)),
        compiler_params=_params("arbitrary"),
        name="mods",
    )(c.T, w_ada, b_ada.reshape(1, n))


def _prenorm_kernel(x_ref, nw_ref, mods_ref, h_ref, *, k):
    h = _mod_rms(x_ref[...], nw_ref[...], mods_ref[k:k + 1, :], mods_ref[k + 1:k + 2, :])
    h_ref[...] = h.astype(h_ref.dtype)


def _prenorm(x2, nw, mods3, k, seq):
    m, d = x2.shape
    tm = min(512, seq)
    per_b = seq // tm
    return pl.pallas_call(
        functools.partial(_prenorm_kernel, k=k),
        out_shape=jax.ShapeDtypeStruct((m, d), BF16),
        grid=(m // tm,),
        in_specs=[pl.BlockSpec((tm, d), lambda i: (i, 0)),
                  pl.BlockSpec((1, d), lambda i: (0, 0)),
                  pl.BlockSpec((None, N_ADA, d), lambda i: (i // per_b, 0, 0))],
        out_specs=pl.BlockSpec((tm, d), lambda i: (i, 0)),
        compiler_params=_params("arbitrary"),
        name="prenorm",
    )(x2, nw.reshape(1, d), mods3)


FFN_ROWS = 1024
FFN_HIDDEN = 512
FFN_EPI_CHUNKS = 8


def _ffn_kernel(h_ref, x_ref, wa_ref, wb_ref, wo_ref, mods_ref, nw_ref, *refs, k_gate, k_next, n_tiles):
    *out_refs, acc = refs
    i = pl.program_id(0)
    j = pl.program_id(1)
    slot = i % 2

    @pl.when((i == 0) & (j == 0))
    def _():
        acc[...] = jnp.zeros_like(acc)

    def epilogue():
        rows = x_ref.shape[0]
        r0 = pl.multiple_of(jnp.minimum(j, FFN_EPI_CHUNKS - 1) * rows, rows)
        done = acc[1 - slot, pl.ds(r0, rows), :]
        xn = x_ref[...] + 0.5 * mods_ref[k_gate:k_gate + 1, :] * done
        if k_next is None:
            ms = jnp.mean(xn * xn, axis=-1, keepdims=True)
            out_refs[0][...] = xn * lax.rsqrt(ms + EPS) * nw_ref[...]
        else:
            out_refs[0][...] = xn
            hn = _mod_rms(xn, nw_ref[...], mods_ref[k_next:k_next + 1, :],
                          mods_ref[k_next + 1:k_next + 2, :])
            out_refs[1][...] = hn.astype(BF16)

    def matmuls():
        h = h_ref[...]
        a = _dot(h, wa_ref[...])
        b = _dot(h, wb_ref[...])
        act = (_silu(a) * b).astype(BF16)
        acc[slot] = jnp.where(j == 0, 0.0, acc[slot]) + _dot(act, wo_ref[...])

    @pl.when(i == 0)
    def _():
        matmuls()

    @pl.when((i > 0) & (i < n_tiles))
    def _():
        epilogue()
        matmuls()

    @pl.when(i == n_tiles)
    def _():
        epilogue()


def _ffn(h, x2, wa, wb, wo, mods3, nw_next, k_gate, k_next, seq):
    m, d = x2.shape
    ffp = wa.shape[1]
    tm = min(FFN_ROWS, seq)
    tf = FFN_HIDDEN
    n_tiles = m // tm
    n_hidden = ffp // tf
    ec = FFN_EPI_CHUNKS
    rows = tm // ec
    per_b = seq // tm
    assert n_hidden >= ec

    def done_chunk(i, j):
        return (jnp.where(i == 0, 0, (i - 1) * ec + jnp.minimum(j, ec - 1)), 0)

    def hidden(i, j):
        return jnp.where(i == n_tiles, n_hidden - 1, j)

    out_shape = [jax.ShapeDtypeStruct((m, d), F32)]
    out_specs = [pl.BlockSpec((rows, d), done_chunk)]
    if k_next is not None:
        out_shape.append(jax.ShapeDtypeStruct((m, d), BF16))
        out_specs.append(pl.BlockSpec((rows, d), done_chunk))
    return pl.pallas_call(
        functools.partial(_ffn_kernel, k_gate=k_gate, k_next=k_next, n_tiles=n_tiles),
        out_shape=out_shape,
        grid=(n_tiles + 1, n_hidden),
        in_specs=[pl.BlockSpec((tm, d), lambda i, j: (jnp.minimum(i, n_tiles - 1), 0)),
                  pl.BlockSpec((rows, d), done_chunk),
                  pl.BlockSpec((d, tf), lambda i, j: (0, hidden(i, j))),
                  pl.BlockSpec((d, tf), lambda i, j: (0, hidden(i, j))),
                  pl.BlockSpec((tf, d), lambda i, j: (hidden(i, j), 0)),
                  pl.BlockSpec((None, N_ADA, d), lambda i, j: (jnp.maximum(i - 1, 0) // per_b, 0, 0)),
                  pl.BlockSpec((1, d), lambda i, j: (0, 0))],
        out_specs=out_specs,
        scratch_shapes=[pltpu.VMEM((2, tm, d), F32)],
        compiler_params=_params("arbitrary", "arbitrary"),
        name="ffn",
    )(h, x2, wa, wb, wo, mods3, nw_next.reshape(1, d))


PROJ_ROWS = 1024
PROJ_SUB = 256


def _proj_kernel(h_ref, w_ref, o_ref, *, act):
    y = _dot(h_ref[...], w_ref[...])
    if act:
        y = _silu(y)
    o_ref[...] = y.astype(o_ref.dtype)


def _proj(h, w, tn, act=False):
    m, d = h.shape
    n = w.shape[1]
    tm = min(PROJ_ROWS, m)
    return pl.pallas_call(
        functools.partial(_proj_kernel, act=act),
        out_shape=jax.ShapeDtypeStruct((m, n), F32),
        grid=(m // tm, n // tn),
        in_specs=[pl.BlockSpec((tm, d), lambda i, j: (i, 0)),
                  pl.BlockSpec((d, tn), lambda i, j: (0, j))],
        out_specs=pl.BlockSpec((tm, tn), lambda i, j: (i, j)),
        compiler_params=_params("arbitrary", "arbitrary"),
        name="proj",
    )(h, w)


def _proj_conv_kernel(h_ref, w_ref, cw_ref, cb_ref, o_ref, halo, *, tiles_per_seq):
    i = pl.program_id(0)
    j = pl.program_id(1)
    tm = h_ref.shape[0]
    sub = min(PROJ_SUB, tm)
    taps = SSD_CONV - 1

    @pl.when(i % tiles_per_seq == 0)
    def _():
        halo[j] = jnp.zeros(halo.shape[1:], F32)

    prev = halo[j]
    w = w_ref[...]
    cw = cw_ref[...]
    cb = cb_ref[...]
    for r in range(tm // sub):
        raw = _dot(h_ref[r * sub:(r + 1) * sub, :], w)
        ext = jnp.concatenate([prev, raw], axis=0)
        conv = cb + cw[taps:taps + 1, :] * raw
        for k in range(taps):
            lo = SUBLANES - taps + k
            conv = conv + cw[k:k + 1, :] * ext[lo:lo + sub, :]
        o_ref[r * sub:(r + 1) * sub, :] = _silu(conv).astype(o_ref.dtype)
        prev = raw[sub - SUBLANES:, :]
    halo[j] = prev


def _proj_conv(h, w, conv_w, conv_b, tn, seq, out_dtype):
    m, d = h.shape
    n = w.shape[1]
    tm = min(PROJ_ROWS, seq)
    return pl.pallas_call(
        functools.partial(_proj_conv_kernel, tiles_per_seq=seq // tm),
        out_shape=jax.ShapeDtypeStruct((m, n), out_dtype),
        grid=(m // tm, n // tn),
        in_specs=[pl.BlockSpec((tm, d), lambda i, j: (i, 0)),
                  pl.BlockSpec((d, tn), lambda i, j: (0, j)),
                  pl.BlockSpec((SSD_CONV, tn), lambda i, j: (0, j)),
                  pl.BlockSpec((1, tn), lambda i, j: (0, j))],
        out_specs=pl.BlockSpec((tm, tn), lambda i, j: (i, j)),
        scratch_shapes=[pltpu.VMEM((n // tn, SUBLANES, tn), F32)],
        compiler_params=_params("arbitrary", "arbitrary"),
        name="projconv",
    )(h, w, conv_w, conv_b.reshape(1, n))


def _proj_chunked_kernel(h_ref, w_ref, o_ref, scr):
    n_slabs, rows, lq = o_ref.shape
    q = S5_SLAB
    L = lq // q
    y = _dot(h_ref[...], w_ref[...])
    for c in range(scr.shape[0]):
        scr[c] = y[:, c * LANES:(c + 1) * LANES]
    for c in range(scr.shape[0]):
        s, off = divmod(c * LANES, q)
        for l in range(L):
            o_ref[s, :, l * q + off:l * q + off + LANES] = scr[c, pl.ds(l, rows, stride=L), :]


def _proj_chunked(h, w):
    m, d = h.shape
    n = w.shape[1]
    tm = min(PROJ_ROWS, m)
    L = S5_CHUNK
    n_slabs = n // S5_SLAB
    return pl.pallas_call(
        _proj_chunked_kernel,
        out_shape=jax.ShapeDtypeStruct((n_slabs, m // L, L * S5_SLAB), F32),
        grid=(m // tm,),
        in_specs=[pl.BlockSpec((tm, d), lambda i: (i, 0)),
                  pl.BlockSpec((d, n), lambda i: (0, 0))],
        out_specs=pl.BlockSpec((n_slabs, tm // L, L * S5_SLAB), lambda i: (0, i, 0)),
        scratch_shapes=[pltpu.VMEM((n // LANES, tm, LANES), F32)],
        compiler_params=_params("arbitrary"),
        name="projchunk",
    )(h, w)


def _split3(v):
    hi = v.astype(BF16)
    r = v - hi.astype(F32)
    mid = r.astype(BF16)
    lo = (r - mid.astype(F32)).astype(BF16)
    return hi, mid, lo


def _pair(v, j, first):
    return jnp.where(first, v[:, 2 * j:2 * j + 1], v[:, 2 * j + 1:2 * j + 2])


def _ssd_kernel(xs_ref, bc_ref, dt_ref, zs_ref, dtb_ref, alog_ref, dsk_ref, nw_ref, y_ref, state):
    L = SSD_CHUNK
    d_inner = xs_ref.shape[1]
    n_state = SSD_STATE
    gw = d_inner // SSD_GROUPS
    pairs_per_group = gw // LANES

    @pl.when(pl.program_id(1) == 0)
    def _():
        state[...] = jnp.zeros_like(state)

    dtr = dt_ref[...] + dtb_ref[...]
    dt = jnp.maximum(dtr, 0.0) + jnp.log1p(jnp.exp(-jnp.abs(dtr)))
    da = dt * (-jnp.exp(alog_ref[...]))
    row = lax.broadcasted_iota(jnp.int32, (L, L), 0)
    col = lax.broadcasted_iota(jnp.int32, (L, L), 1)
    causal = row >= col
    tril = jnp.where(causal, 1.0, 0.0).astype(BF16)
    cs = sum(_dot(tril, part) for part in _split3(da))
    cs_last = cs[L - 1:L, :]
    ecs = jnp.exp(cs)
    ecl = jnp.exp(cs_last)
    cs_t = cs.T
    dt_t = dt.T
    wst_t = (dt * jnp.exp(cs_last - cs)).T

    lane = lax.broadcasted_iota(jnp.int32, (L, LANES), 1)
    first = lane < SSD_HEAD_DIM
    first_row = first[0:1, :]

    y_parts = []
    for g in range(SSD_GROUPS):
        bm_g = bc_ref[:, g * n_state:(g + 1) * n_state]
        cm_g = bc_ref[:, (SSD_GROUPS + g) * n_state:(SSD_GROUPS + g + 1) * n_state]
        cb = _dot_nt(cm_g, bm_g)
        bm_t = bm_g.astype(F32).T
        y_off = _dot(cm_g, state[g].astype(BF16))
        for jj in range(pairs_per_group):
            j = g * pairs_per_group + jj
            h0, h1 = 2 * j, 2 * j + 1
            cols = slice(j * LANES, (j + 1) * LANES)
            gcols = slice(jj * LANES, (jj + 1) * LANES)
            xs_p = xs_ref[:, cols]
            xb = xs_p.astype(BF16)
            zero = jnp.zeros_like(xb)
            rhs = jnp.concatenate([jnp.where(first, xb, zero), jnp.where(first, zero, xb)], axis=0)
            lhs_y = []
            lhs_s = []
            for h in (h0, h1):
                dec = jnp.exp(jnp.where(causal, cs[:, h:h + 1] - cs_t[h:h + 1, :], -jnp.inf))
                lhs_y.append((cb * dec * dt_t[h:h + 1, :]).astype(BF16))
                lhs_s.append((bm_t * wst_t[h:h + 1, :]).astype(BF16))
            y_p = (_dot(jnp.concatenate(lhs_y, axis=1), rhs)
                   + _pair(ecs, j, first) * y_off[:, gcols]
                   + dsk_ref[:, cols] * xs_p)
            y_parts.append(y_p)
            st_new = _dot(jnp.concatenate(lhs_s, axis=1), rhs)
            state[g, :, gcols] = _pair(ecl, j, first_row) * state[g, :, gcols] + st_new

    yz = jnp.concatenate(y_parts, axis=1) * zs_ref[...]
    outs = []
    for g in range(SSD_GROUPS):
        seg = yz[:, g * gw:(g + 1) * gw]
        ms = jnp.mean(seg * seg, axis=-1, keepdims=True)
        outs.append(seg * lax.rsqrt(ms + EPS))
    y_ref[...] = (jnp.concatenate(outs, axis=1) * nw_ref[...]).astype(y_ref.dtype)


def _ssd(xs, bc, dtp, zs, dt_bias, a_log, d_ssd, norm_w, bsz, seq):
    m, d_inner = xs.shape
    L = SSD_CHUNK
    nc = seq // L
    heads = d_inner // SSD_HEAD_DIM
    pad = LANES - heads
    row = lambda b, c: (b * nc + c, 0)
    const = lambda b, c: (0, 0)
    return pl.pallas_call(
        _ssd_kernel,
        out_shape=jax.ShapeDtypeStruct((m, d_inner), BF16),
        grid=(bsz, nc),
        in_specs=[pl.BlockSpec((L, d_inner), row),
                  pl.BlockSpec((L, bc.shape[1]), row),
                  pl.BlockSpec((L, LANES), row),
                  pl.BlockSpec((L, d_inner), row),
                  pl.BlockSpec((1, LANES), const),
                  pl.BlockSpec((1, LANES), const),
                  pl.BlockSpec((1, d_inner), const),
                  pl.BlockSpec((1, d_inner), const)],
        out_specs=pl.BlockSpec((L, d_inner), row),
        scratch_shapes=[pltpu.VMEM((SSD_GROUPS, SSD_STATE, d_inner // SSD_GROUPS), F32)],
        compiler_params=_params("arbitrary", "arbitrary"),
        name="ssd",
    )(xs, bc, dtp, zs,
      jnp.pad(dt_bias, (0, pad)).reshape(1, LANES), jnp.pad(a_log, (0, pad)).reshape(1, LANES),
      jnp.repeat(d_ssd, SSD_HEAD_DIM).reshape(1, d_inner), norm_w.reshape(1, d_inner))


def _s5_prep_kernel(lre_ref, lim_ref, ldt_ref, btr_ref, bti_ref, cnr_ref, cni_ref,
                    wend_ref, wcorr_ref, krev_ref, al_ref):
    L = S5_CHUNK
    q = S5_SLAB
    ns = S5_SLAB_STATE
    lr = jnp.minimum(lre_ref[...], -1e-4)
    li = lim_ref[...]
    dt = jnp.exp(ldt_ref[...])

    def power(k):
        mag = jnp.exp(lr * dt * float(k))
        ang = li * dt * float(k)
        return mag * jnp.cos(ang), mag * jnp.sin(ang)

    ar, ai = power(1)
    den = lr * lr + li * li
    nr = ar - 1.0
    kr = (nr * lr + ai * li) / den
    ki = (ai * lr - nr * li) / den

    rows = lax.broadcasted_iota(jnp.int32, (q, ns), 0) // S5_GROUP_SIZE
    cols = lax.broadcasted_iota(jnp.int32, (q, ns), 1) // S5_STATE
    same = rows == cols

    def block_diag(ref):
        return jnp.where(same, jnp.concatenate([ref[...]] * (ns // LANES), axis=1), 0.0)

    btr, bti = block_diag(btr_ref), block_diag(bti_ref)
    bbr = btr * kr - bti * ki
    bbi = btr * ki + bti * kr
    cr, ci = block_diag(cnr_ref), block_diag(cni_ref)

    for k in range(L):
        pr, pi = power(k)
        blk = jnp.concatenate([bbr * pr - bbi * pi, bbr * pi + bbi * pr], axis=1)
        wend_ref[(L - 1 - k) * q:(L - k) * q, :] = blk.astype(BF16)
    for l in range(L):
        pr, pi = power(l + 1)
        blk = jnp.concatenate([cr * pr - ci * pi, -(cr * pi) - ci * pr], axis=1)
        wcorr_ref[l * q:(l + 1) * q, :] = blk.astype(BF16)
    cstack = jnp.concatenate([cr, -ci], axis=1).astype(BF16)
    krev_ref[...] = _dot_nt(wend_ref[...], cstack).astype(BF16)
    pr, pi = power(L)
    al_ref[...] = jnp.concatenate([pr, pi], axis=1)


def _s5_prep(lam_re, lam_im, log_dt, b_re, b_im, c_re, c_im):
    g, p = lam_re.shape
    width = g * S5_GROUP_SIZE
    n_slabs = width // S5_SLAB
    ns = S5_SLAB_STATE
    lq = S5_CHUNK * S5_SLAB
    row_vec = lambda v: v.reshape(1, g * p)
    twice = lambda v: jnp.concatenate([v, v], axis=1)
    bt = lambda v: twice(jnp.swapaxes(v, 1, 2).reshape(width, p))
    cn = lambda v: twice(v.reshape(width, p))
    vec = pl.BlockSpec((None, 1, ns), lambda s: (s, 0, 0))
    mat = pl.BlockSpec((S5_SLAB, LANES), lambda s: (s, 0))
    vec3 = lambda v: row_vec(v).reshape(n_slabs, 1, ns)
    return pl.pallas_call(
        _s5_prep_kernel,
        out_shape=[jax.ShapeDtypeStruct((n_slabs, lq, 2 * ns), BF16),
                   jax.ShapeDtypeStruct((n_slabs, lq, 2 * ns), BF16),
                   jax.ShapeDtypeStruct((n_slabs, lq, S5_SLAB), BF16),
                   jax.ShapeDtypeStruct((n_slabs, 1, 2 * ns), F32)],
        grid=(n_slabs,),
        in_specs=[vec, vec, vec, mat, mat, mat, mat],
        out_specs=[pl.BlockSpec((None, lq, 2 * ns), lambda s: (s, 0, 0)),
                   pl.BlockSpec((None, lq, 2 * ns), lambda s: (s, 0, 0)),
                   pl.BlockSpec((None, lq, S5_SLAB), lambda s: (s, 0, 0)),
                   pl.BlockSpec((None, 1, 2 * ns), lambda s: (s, 0, 0))],
        compiler_params=_params("arbitrary"),
        name="s5prep",
    )(vec3(lam_re), vec3(lam_im), vec3(jnp.repeat(log_dt, p)),
      bt(b_re), bt(b_im), cn(c_re), cn(c_im))


def _gelu_tanh(v):
    return 0.5 * v * (1.0 + jnp.tanh(math.sqrt(2.0 / math.pi) * (v + 0.044715 * (v * v * v))))


def _s5_kernel(u_ref, wend_ref, wcorr_ref, krev_ref, al_ref, d_ref, y_ref, e_scr, cin_scr, st_scr,
               tok_scr, *, blocks_per_seq):
    L = S5_CHUNK
    q = S5_SLAB
    ns = S5_SLAB_STATE
    rows = u_ref.shape[0]

    @pl.when(pl.program_id(1) % blocks_per_seq == 0)
    def _():
        st_scr[...] = jnp.zeros_like(st_scr)

    u = u_ref[...]
    ub = u.astype(BF16)
    e_scr[...] = _dot(ub, wend_ref[...])
    alr = al_ref[:, :ns]
    ali = al_ref[:, ns:]

    def tile_step(t, st):
        base = pl.multiple_of(t * SUBLANES, SUBLANES)
        e = e_scr[pl.ds(base, SUBLANES), :]
        sr, si = st
        carried = []
        for r in range(SUBLANES):
            carried.append(jnp.concatenate([sr, si], axis=1))
            er = e[r:r + 1, :ns]
            ei = e[r:r + 1, ns:]
            sr, si = alr * sr - ali * si + er, alr * si + ali * sr + ei
        cin_scr[pl.ds(base, SUBLANES), :] = jnp.concatenate(carried, axis=0)
        return sr, si

    sr, si = lax.fori_loop(0, rows // SUBLANES, tile_step, (st_scr[:, :ns], st_scr[:, ns:]))
    st_scr[...] = jnp.concatenate([sr, si], axis=1)

    corr = _dot_nt(cin_scr[...].astype(BF16), wcorr_ref[...])
    intra = [_dot(ub[:, :(l + 1) * q], krev_ref[(L - 1 - l) * q:, :]) for l in range(L)]
    y = _gelu_tanh(jnp.concatenate(intra, axis=1) + corr + d_ref[...] * u)
    for c in range(tok_scr.shape[0]):
        for l in range(L):
            lo = l * q + c * LANES
            tok_scr[c, pl.ds(l, rows, stride=L), :] = y[:, lo:lo + LANES]
        y_ref[:, c * LANES:(c + 1) * LANES] = tok_scr[c].astype(y_ref.dtype)


def _s5(u2, prep, d_skip, bsz, seq):
    wend, wcorr, krev, al = prep
    n_slabs, chunks, lq = u2.shape
    L = S5_CHUNK
    q = S5_SLAB
    ns2 = 2 * S5_SLAB_STATE
    chunks_per_seq = seq // L
    rb = min(256, chunks_per_seq)
    blocks_per_seq = chunks_per_seq // rb
    d_t = jnp.tile(d_skip.reshape(n_slabs, 1, q), (1, 1, L))
    slab = lambda s, r: (s, 0, 0)
    blk = lambda s, r: (s, r, 0)
    return pl.pallas_call(
        functools.partial(_s5_kernel, blocks_per_seq=blocks_per_seq),
        out_shape=jax.ShapeDtypeStruct((n_slabs, chunks * L, q), BF16),
        grid=(n_slabs, chunks // rb),
        in_specs=[pl.BlockSpec((None, rb, lq), blk),
                  pl.BlockSpec((None, lq, ns2), slab),
                  pl.BlockSpec((None, lq, ns2), slab),
                  pl.BlockSpec((None, lq, q), slab),
                  pl.BlockSpec((None, 1, ns2), slab),
                  pl.BlockSpec((None, 1, lq), slab)],
        out_specs=pl.BlockSpec((None, rb * L, q), blk),
        scratch_shapes=[pltpu.VMEM((rb, ns2), F32), pltpu.VMEM((rb, ns2), F32),
                        pltpu.VMEM((1, ns2), F32), pltpu.VMEM((q // LANES, rb * L, LANES), F32)],
        compiler_params=_params("arbitrary", "arbitrary"),
        name="s5",
    )(u2, wend, wcorr, krev, al, d_t)


def _merge_kernel(ya_ref, yb_ref, ga_ref, gb_ref, x_ref, wa_ref, wg_ref, wo_ref, mods_ref, nw_ref,
                  xo_ref, h_ref, *, k_gate, k_next):
    d = x_ref.shape[1]
    p_a = _dot(ya_ref[...], wa_ref[...])
    glu = sum(_dot(yb_ref[s], wg_ref[s]) for s in range(yb_ref.shape[0]))
    p_b = glu[:, :d] * jax.nn.sigmoid(glu[:, d:])
    merged = jax.nn.sigmoid(ga_ref[...]) * p_a + jax.nn.sigmoid(gb_ref[...]) * p_b
    xn = x_ref[...] + mods_ref[k_gate:k_gate + 1, :] * _dot(merged.astype(BF16), wo_ref[...])
    xo_ref[...] = xn
    hn = _mod_rms(xn, nw_ref[...], mods_ref[k_next:k_next + 1, :], mods_ref[k_next + 1:k_next + 2, :])
    h_ref[...] = hn.astype(BF16)


def _merge(ya, yb_slabs, ga, gb, x2, w_a, w_glu, w_o, mods3, nw_next, k_gate, k_next, seq):
    m, d = x2.shape
    n_slabs, _, q = yb_slabs.shape
    tm = min(256, seq)
    per_b = seq // tm
    row = lambda i: (i, 0)
    const = lambda i: (0, 0)
    once = pl.Buffered(1)
    return pl.pallas_call(
        functools.partial(_merge_kernel, k_gate=k_gate, k_next=k_next),
        out_shape=[jax.ShapeDtypeStruct((m, d), F32), jax.ShapeDtypeStruct((m, d), BF16)],
        grid=(m // tm,),
        in_specs=[pl.BlockSpec((tm, ya.shape[1]), row),
                  pl.BlockSpec((n_slabs, tm, q), lambda i: (0, i, 0)),
                  pl.BlockSpec((tm, d), row),
                  pl.BlockSpec((tm, d), row),
                  pl.BlockSpec((tm, d), row),
                  pl.BlockSpec(w_a.shape, const, pipeline_mode=once),
                  pl.BlockSpec(w_glu.shape, lambda i: (0, 0, 0), pipeline_mode=once),
                  pl.BlockSpec(w_o.shape, const, pipeline_mode=once),
                  pl.BlockSpec((None, N_ADA, d), lambda i: (i // per_b, 0, 0)),
                  pl.BlockSpec((1, d), const)],
        out_specs=[pl.BlockSpec((tm, d), row), pl.BlockSpec((tm, d), row)],
        compiler_params=_params("arbitrary"),
        name="merge",
    )(ya, yb_slabs, ga, gb, x2, w_a, w_glu, w_o, mods3, nw_next.reshape(1, d))


def _ffn_weights(w_in, w_out):
    ff = w_out.shape[0]
    ffp = -(-ff // 512) * 512
    wa = jnp.pad(w_in[:, :ff].astype(BF16), ((0, 0), (0, ffp - ff)))
    wb = jnp.pad(w_in[:, ff:].astype(BF16), ((0, 0), (0, ffp - ff)))
    wo = jnp.pad(w_out.astype(BF16), ((0, ffp - ff), (0, 0)))
    return wa, wb, wo


def kernel(x, c, w_ada, b_ada, norm_ffn1, w_ffn1_in, w_ffn1_out, norm_mix, w_in, conv_w, conv_b, dt_bias, a_log, d_ssd, ssd_norm_w, w_a_proj, s5_lambda_re, s5_lambda_im, s5_b_re, s5_b_im, s5_c_re, s5_c_im, s5_d, s5_log_dt, w_b_glu, w_out, norm_ffn2, w_ffn2_in, w_ffn2_out, norm_final):
    bsz, seq, d = x.shape
    depth = w_ada.shape[0]
    m = bsz * seq
    d_inner = ssd_norm_w.shape[1]
    conv_dim = conv_w.shape[2]
    heads = dt_bias.shape[1]
    s5_width = w_b_glu.shape[1]
    off_xbc = d_inner
    off_dt = off_xbc + conv_dim
    off_u = off_dt + heads
    off_g = off_u + s5_width

    assert depth == 1, "the epilogue fusion below is written for a single layer"
    l = 0
    x2 = x.reshape(m, d)
    mods3 = _mods(c, w_ada[l], b_ada[l]).reshape(bsz, N_ADA, d)
    h = _prenorm(x2, norm_ffn1[l], mods3, 0, seq)

    x2, h = _ffn(h, x2, *_ffn_weights(w_ffn1_in[l], w_ffn1_out[l]), mods3, norm_mix[l], 2, 3, seq)

    wi = w_in[l].astype(BF16)
    off_bc = off_xbc + d_inner
    zs = _proj(h, wi[:, :off_xbc], 1024, act=True)
    xs = _proj_conv(h, wi[:, off_xbc:off_bc], conv_w[l][:, :d_inner], conv_b[l][:d_inner],
                    1024, seq, F32)
    bc = _proj_conv(h, wi[:, off_bc:off_dt], conv_w[l][:, d_inner:], conv_b[l][d_inner:],
                    1024, seq, BF16)
    dtp = _proj(h, jnp.pad(wi[:, off_dt:off_u], ((0, 0), (0, LANES - heads))), LANES)
    u = _proj_chunked(h, wi[:, off_u:off_g])
    ga = _proj(h, wi[:, off_g:off_g + d], 1024)
    gb = _proj(h, wi[:, off_g + d:], 1024)

    ya = _ssd(xs, bc, dtp, zs, dt_bias[l], a_log[l], d_ssd[l], ssd_norm_w[l], bsz, seq)
    prep = _s5_prep(s5_lambda_re[l], s5_lambda_im[l], s5_log_dt[l],
                    s5_b_re[l], s5_b_im[l], s5_c_re[l], s5_c_im[l])
    yb = _s5(u, prep, s5_d[l], bsz, seq)

    x2, h = _merge(ya, yb, ga, gb, x2, w_a_proj[l].astype(BF16),
                   w_b_glu[l].astype(BF16).reshape(s5_width // S5_SLAB, S5_SLAB, 2 * d),
                   w_out[l].astype(BF16), mods3, norm_ffn2[l], 5, 6, seq)

    (x2,) = _ffn(h, x2, *_ffn_weights(w_ffn2_in[l], w_ffn2_out[l]), mods3, norm_final, 8, None, seq)
    return x2.reshape(bsz, seq, d)
```

```python
import functools
import math
from typing import NamedTuple

import jax
import jax.numpy as jnp
from jax import lax
from jax.experimental import pallas as pl
from jax.experimental.pallas import tpu as pltpu

F32 = jnp.float32
BF16 = jnp.bfloat16
EPS = 1e-6

VMEM_LIMIT_BYTES = 56 * 1024 * 1024
LANES = 128
SUBLANES = 8
BF16_ROWS = 16

SSD_HEAD_DIM = 64
SSD_GROUPS = 4
SSD_STATE = 128
SSD_CONV = 4
SSD_CHUNK = 128
S5_GROUP_SIZE = 16
S5_STATE = 64
S5_SLAB = 256
S5_GROUPS_PER_SLAB = S5_SLAB // S5_GROUP_SIZE
S5_SLAB_STATE = S5_GROUPS_PER_SLAB * S5_STATE
S5_CHUNK = 8
N_ADA = 9


def _params(*sem):
    return pltpu.CompilerParams(dimension_semantics=sem, vmem_limit_bytes=VMEM_LIMIT_BYTES)


def _silu(v):
    return v * jax.nn.sigmoid(v)


def _dot(a, b):
    return jnp.dot(a, b, preferred_element_type=F32)


def _dot_nt(a, b):
    return lax.dot_general(a, b, (((1,), (1,)), ((), ())), preferred_element_type=F32)


def _mod_rms(x, nw, shift, scale):
    ms = jnp.mean(x * x, axis=-1, keepdims=True)
    return (x * lax.rsqrt(ms + EPS) * nw) * (1.0 + scale) + shift


def _mods_kernel(ct_ref, w_ref, b_ref, o_ref):
    ca = _silu(ct_ref[...])
    w = w_ref[...]
    rows = [jnp.sum(ca[:, b:b + 1] * w, axis=0, keepdims=True) for b in range(ca.shape[1])]
    o_ref[...] = jnp.concatenate(rows, axis=0) + b_ref[...]


def _mods(c, w_ada, b_ada):
    bsz, d = c.shape
    n = w_ada.shape[1]
    tn = 1024
    return pl.pallas_call(
        _mods_kernel,
        out_shape=jax.ShapeDtypeStruct((bsz, n), F32),
        grid=(n // tn,),
        in_specs=[pl.BlockSpec((d, bsz), lambda j: (0, 0)),
                  pl.BlockSpec((d, tn), lambda j: (0, j)),
                  pl.BlockSpec((1, tn), lambda j: (0, j))],
        out_specs=pl.BlockSpec((bsz, tn), lambda j: (0, j)),
        compiler_params=_params("arbitrary"),
        name="mods",
    )(c.T, w_ada, b_ada.reshape(1, n))


def _prenorm_kernel(x_ref, nw_ref, mods_ref, h_ref, *, k):
    h = _mod_rms(x_ref[...], nw_ref[...], mods_ref[k:k + 1, :], mods_ref[k + 1:k + 2, :])
    h_ref[...] = h.astype(h_ref.dtype)


def _prenorm(x2, nw, mods3, k, seq):
    m, d = x2.shape
    tm = min(512, seq)
    per_b = seq // tm
    return pl.pallas_call(
        functools.partial(_prenorm_kernel, k=k),
        out_shape=jax.ShapeDtypeStruct((m, d), BF16),
        grid=(m // tm,),
        in_specs=[pl.BlockSpec((tm, d), lambda i: (i, 0)),
                  pl.BlockSpec((1, d), lambda i: (0, 0)),
                  pl.BlockSpec((None, N_ADA, d), lambda i: (i // per_b, 0, 0))],
        out_specs=pl.BlockSpec((tm, d), lambda i: (i, 0)),
        compiler_params=_params("arbitrary"),
        name="prenorm",
    )(x2, nw.reshape(1, d), mods3)


FFN_ROWS = 1024
FFN_HIDDEN = 512
FFN_EPI_CHUNKS = 8


class _CastJob(NamedTuple):
    src: jax.Array
    cols: tuple
    out_rows: int

    def chunk_rows(self, slots):
        need = -(-self.out_rows // slots)
        rows = next(r for r in range(BF16_ROWS, self.out_rows + 1, BF16_ROWS)
                    if r >= need and self.out_rows % r == 0)
        return rows


def _cast_chunk(job, rows, chunk, src_ref, dst_refs):
    src_rows = job.src.shape[0]
    out_chunk = jnp.minimum(chunk, job.out_rows // rows - 1)
    partial = src_rows % rows != 0 or job.out_rows != src_rows
    if partial:
        r = lax.broadcasted_iota(jnp.int32, (rows, 1), 0) + out_chunk * rows
        keep = r < src_rows
    for (c0, c1, width), dst in zip(job.cols, dst_refs):
        v = src_ref[:, c0:c1]
        if partial:
            v = jnp.where(keep, v, 0.0)
        dst[:, :c1 - c0] = v.astype(BF16)
        if width > c1 - c0:
            dst[:, c1 - c0:] = jnp.zeros((rows, width - (c1 - c0)), BF16)


def _ffn_kernel(h_ref, x_ref, wa_ref, wb_ref, wo_ref, mods_ref, nw_ref, *refs, k_gate, k_next, n_tiles,
                jobs):
    n_job_out = sum(len(job.cols) for job, _ in jobs)
    job_src = refs[:len(jobs)]
    refs = refs[len(jobs):]
    acc = refs[-1]
    out_refs = refs[:len(refs) - 1 - n_job_out]
    job_dst = refs[len(out_refs):-1]
    i = pl.program_id(0)
    j = pl.program_id(1)
    slot = i % 2

    def casts():
        chunk = i * FFN_EPI_CHUNKS + jnp.minimum(j, FFN_EPI_CHUNKS - 1)
        k = 0
        for (job, rows), src_ref in zip(jobs, job_src):
            _cast_chunk(job, rows, chunk, src_ref, job_dst[k:k + len(job.cols)])
            k += len(job.cols)

    @pl.when((i == 0) & (j == 0))
    def _():
        acc[...] = jnp.zeros_like(acc)

    def epilogue():
        rows = x_ref.shape[0]
        r0 = pl.multiple_of(jnp.minimum(j, FFN_EPI_CHUNKS - 1) * rows, rows)
        done = acc[1 - slot, pl.ds(r0, rows), :]
        xn = x_ref[...] + 0.5 * mods_ref[k_gate:k_gate + 1, :] * done
        if k_next is None:
            ms = jnp.mean(xn * xn, axis=-1, keepdims=True)
            out_refs[0][...] = xn * lax.rsqrt(ms + EPS) * nw_ref[...]
        else:
            out_refs[0][...] = xn
            hn = _mod_rms(xn, nw_ref[...], mods_ref[k_next:k_next + 1, :],
                          mods_ref[k_next + 1:k_next + 2, :])
            out_refs[1][...] = hn.astype(BF16)

    def matmuls():
        h = h_ref[...]
        a = _dot(h, wa_ref[...])
        b = _dot(h, wb_ref[...])
        act = (_silu(a) * b).astype(BF16)
        acc[slot] = jnp.where(j == 0, 0.0, acc[slot]) + _dot(act, wo_ref[...])

    @pl.when(i == 0)
    def _():
        casts()
        matmuls()

    @pl.when((i > 0) & (i < n_tiles))
    def _():
        epilogue()
        casts()
        matmuls()

    @pl.when(i == n_tiles)
    def _():
        epilogue()


def _ffn(h, x2, wa, wb, wo, mods3, nw_next, k_gate, k_next, seq, cast_jobs=()):
    m, d = x2.shape
    ffp = wa.shape[1]
    tm = min(FFN_ROWS, seq)
    tf = FFN_HIDDEN
    n_tiles = m // tm
    n_hidden = ffp // tf
    ec = FFN_EPI_CHUNKS
    rows = tm // ec
    per_b = seq // tm
    assert n_hidden >= ec

    def done_chunk(i, j):
        return (jnp.where(i == 0, 0, (i - 1) * ec + jnp.minimum(j, ec - 1)), 0)

    def hidden(i, j):
        return jnp.where(i == n_tiles, n_hidden - 1, j)

    def cast_chunk(i, j):
        return jnp.where(i == n_tiles, n_tiles * ec - 1, i * ec + jnp.minimum(j, ec - 1))

    out_shape = [jax.ShapeDtypeStruct((m, d), F32)]
    out_specs = [pl.BlockSpec((rows, d), done_chunk)]
    if k_next is not None:
        out_shape.append(jax.ShapeDtypeStruct((m, d), BF16))
        out_specs.append(pl.BlockSpec((rows, d), done_chunk))
    in_specs = [pl.BlockSpec((tm, d), lambda i, j: (jnp.minimum(i, n_tiles - 1), 0)),
                pl.BlockSpec((rows, d), done_chunk),
                pl.BlockSpec((d, tf), lambda i, j: (0, hidden(i, j))),
                pl.BlockSpec((d, tf), lambda i, j: (0, hidden(i, j))),
                pl.BlockSpec((tf, d), lambda i, j: (hidden(i, j), 0)),
                pl.BlockSpec((None, N_ADA, d), lambda i, j: (jnp.maximum(i - 1, 0) // per_b, 0, 0)),
                pl.BlockSpec((1, d), lambda i, j: (0, 0))]
    sized_jobs = []
    for job in cast_jobs:
        r = job.chunk_rows(n_tiles * ec)
        sized_jobs.append((job._replace(src=jax.ShapeDtypeStruct(job.src.shape, F32)), r))
        last_out = job.out_rows // r - 1
        last_in = (job.src.shape[0] - 1) // r
        in_specs.append(pl.BlockSpec(
            (r, job.src.shape[1]),
            lambda i, j, lo=last_out, li=last_in: (jnp.minimum(jnp.minimum(cast_chunk(i, j), lo), li), 0)))
        for _, _, width in job.cols:
            out_shape.append(jax.ShapeDtypeStruct((job.out_rows, width), BF16))
            out_specs.append(pl.BlockSpec(
                (r, width), lambda i, j, lo=last_out: (jnp.minimum(cast_chunk(i, j), lo), 0)))
    return pl.pallas_call(
        functools.partial(_ffn_kernel, k_gate=k_gate, k_next=k_next, n_tiles=n_tiles,
                          jobs=tuple(sized_jobs)),
        out_shape=out_shape,
        grid=(n_tiles + 1, n_hidden),
        in_specs=in_specs,
        out_specs=out_specs,
        scratch_shapes=[pltpu.VMEM((2, tm, d), F32)],
        compiler_params=_params("arbitrary", "arbitrary"),
        name="ffn",
    )(h, x2, wa, wb, wo, mods3, nw_next.reshape(1, d), *[job.src for job in cast_jobs])


PROJ_ROWS = 1024
PROJ_SUB = 256


def _proj_kernel(h_ref, w_ref, o_ref, *, act):
    y = _dot(h_ref[...], w_ref[...])
    if act:
        y = _silu(y)
    o_ref[...] = y.astype(o_ref.dtype)


def _proj(h, w, col0, n, tn, act=False):
    m, d = h.shape
    tm = min(PROJ_ROWS, m)
    cb = col0 // tn
    return pl.pallas_call(
        functools.partial(_proj_kernel, act=act),
        out_shape=jax.ShapeDtypeStruct((m, n), F32),
        grid=(m // tm, n // tn),
        in_specs=[pl.BlockSpec((tm, d), lambda i, j: (i, 0)),
                  pl.BlockSpec((d, tn), lambda i, j: (0, cb + j))],
        out_specs=pl.BlockSpec((tm, tn), lambda i, j: (i, j)),
        compiler_params=_params("arbitrary", "arbitrary"),
        name="proj",
    )(h, w)


def _proj_conv_kernel(h_ref, w_ref, cw_ref, cb_ref, o_ref, halo, *, tiles_per_seq):
    i = pl.program_id(0)
    j = pl.program_id(1)
    tm = h_ref.shape[0]
    sub = min(PROJ_SUB, tm)
    taps = SSD_CONV - 1

    @pl.when(i % tiles_per_seq == 0)
    def _():
        halo[j] = jnp.zeros(halo.shape[1:], F32)

    prev = halo[j]
    w = w_ref[...]
    cw = cw_ref[...]
    cb = cb_ref[...]
    for r in range(tm // sub):
        raw = _dot(h_ref[r * sub:(r + 1) * sub, :], w)
        ext = jnp.concatenate([prev, raw], axis=0)
        conv = cb + cw[taps:taps + 1, :] * raw
        for k in range(taps):
            lo = SUBLANES - taps + k
            conv = conv + cw[k:k + 1, :] * ext[lo:lo + sub, :]
        o_ref[r * sub:(r + 1) * sub, :] = _silu(conv).astype(o_ref.dtype)
        prev = raw[sub - SUBLANES:, :]
    halo[j] = prev


def _proj_conv(h, w, col0, conv_w, conv_b, tn, seq, out_dtype):
    m, d = h.shape
    n = conv_w.shape[1]
    tm = min(PROJ_ROWS, seq)
    cb = col0 // tn
    return pl.pallas_call(
        functools.partial(_proj_conv_kernel, tiles_per_seq=seq // tm),
        out_shape=jax.ShapeDtypeStruct((m, n), out_dtype),
        grid=(m // tm, n // tn),
        in_specs=[pl.BlockSpec((tm, d), lambda i, j: (i, 0)),
                  pl.BlockSpec((d, tn), lambda i, j: (0, cb + j)),
                  pl.BlockSpec((SSD_CONV, tn), lambda i, j: (0, j)),
                  pl.BlockSpec((1, tn), lambda i, j: (0, j))],
        out_specs=pl.BlockSpec((tm, tn), lambda i, j: (i, j)),
        scratch_shapes=[pltpu.VMEM((n // tn, SUBLANES, tn), F32)],
        compiler_params=_params("arbitrary", "arbitrary"),
        name="projconv",
    )(h, w, conv_w, conv_b.reshape(1, n))


def _proj_chunked_kernel(h_ref, w_ref, o_ref, scr):
    n_slabs, rows, lq = o_ref.shape
    q = S5_SLAB
    L = lq // q
    y = _dot(h_ref[...], w_ref[...])
    for c in range(scr.shape[0]):
        scr[c] = y[:, c * LANES:(c + 1) * LANES]
    for c in range(scr.shape[0]):
        s, off = divmod(c * LANES, q)
        for l in range(L):
            o_ref[s, :, l * q + off:l * q + off + LANES] = scr[c, pl.ds(l, rows, stride=L), :]


def _proj_chunked(h, w, n):
    m, d = h.shape
    tm = min(PROJ_ROWS, m)
    L = S5_CHUNK
    n_slabs = n // S5_SLAB
    return pl.pallas_call(
        _proj_chunked_kernel,
        out_shape=jax.ShapeDtypeStruct((n_slabs, m // L, L * S5_SLAB), F32),
        grid=(m // tm,),
        in_specs=[pl.BlockSpec((tm, d), lambda i: (i, 0)),
                  pl.BlockSpec((d, n), lambda i: (0, 0))],
        out_specs=pl.BlockSpec((n_slabs, tm // L, L * S5_SLAB), lambda i: (0, i, 0)),
        scratch_shapes=[pltpu.VMEM((n // LANES, tm, LANES), F32)],
        compiler_params=_params("arbitrary"),
        name="projchunk",
    )(h, w)


def _split3(v):
    hi = v.astype(BF16)
    r = v - hi.astype(F32)
    mid = r.astype(BF16)
    lo = (r - mid.astype(F32)).astype(BF16)
    return hi, mid, lo


def _pair(v, j, first):
    return jnp.where(first, v[:, 2 * j:2 * j + 1], v[:, 2 * j + 1:2 * j + 2])


def _ssd_kernel(xs_ref, bc_ref, dt_ref, zs_ref, dtb_ref, alog_ref, dsk_ref, nw_ref, y_ref, state):
    L = SSD_CHUNK
    d_inner = xs_ref.shape[1]
    n_state = SSD_STATE
    gw = d_inner // SSD_GROUPS
    pairs_per_group = gw // LANES

    @pl.when(pl.program_id(1) == 0)
    def _():
        state[...] = jnp.zeros_like(state)

    dtr = dt_ref[...] + dtb_ref[...]
    dt = jnp.maximum(dtr, 0.0) + jnp.log1p(jnp.exp(-jnp.abs(dtr)))
    da = dt * (-jnp.exp(alog_ref[...]))
    row = lax.broadcasted_iota(jnp.int32, (L, L), 0)
    col = lax.broadcasted_iota(jnp.int32, (L, L), 1)
    causal = row >= col
    tril = jnp.where(causal, 1.0, 0.0).astype(BF16)
    cs = sum(_dot(tril, part) for part in _split3(da))
    cs_last = cs[L - 1:L, :]
    ecs = jnp.exp(cs)
    ecl = jnp.exp(cs_last)
    cs_t = cs.T
    dt_t = dt.T
    wst_t = (dt * jnp.exp(cs_last - cs)).T

    lane = lax.broadcasted_iota(jnp.int32, (L, LANES), 1)
    first = lane < SSD_HEAD_DIM
    first_row = first[0:1, :]

    y_parts = []
    for g in range(SSD_GROUPS):
        bm_g = bc_ref[:, g * n_state:(g + 1) * n_state]
        cm_g = bc_ref[:, (SSD_GROUPS + g) * n_state:(SSD_GROUPS + g + 1) * n_state]
        cb = _dot_nt(cm_g, bm_g)
        bm_t = bm_g.astype(F32).T
        y_off = _dot(cm_g, state[g].astype(BF16))
        for jj in range(pairs_per_group):
            j = g * pairs_per_group + jj
            h0, h1 = 2 * j, 2 * j + 1
            cols = slice(j * LANES, (j + 1) * LANES)
            gcols = slice(jj * LANES, (jj + 1) * LANES)
            xs_p = xs_ref[:, cols]
            xb = xs_p.astype(BF16)
            zero = jnp.zeros_like(xb)
            rhs = jnp.concatenate([jnp.where(first, xb, zero), jnp.where(first, zero, xb)], axis=0)
            lhs_y = []
            lhs_s = []
            for h in (h0, h1):
                dec = jnp.exp(jnp.where(causal, cs[:, h:h + 1] - cs_t[h:h + 1, :], -jnp.inf))
                lhs_y.append((cb * dec * dt_t[h:h + 1, :]).astype(BF16))
                lhs_s.append((bm_t * wst_t[h:h + 1, :]).astype(BF16))
            y_p = (_dot(jnp.concatenate(lhs_y, axis=1), rhs)
                   + _pair(ecs, j, first) * y_off[:, gcols]
                   + dsk_ref[:, cols] * xs_p)
            y_parts.append(y_p)
            st_new = _dot(jnp.concatenate(lhs_s, axis=1), rhs)
            state[g, :, gcols] = _pair(ecl, j, first_row) * state[g, :, gcols] + st_new

    yz = jnp.concatenate(y_parts, axis=1) * zs_ref[...]
    outs = []
    for g in range(SSD_GROUPS):
        seg = yz[:, g * gw:(g + 1) * gw]
        ms = jnp.mean(seg * seg, axis=-1, keepdims=True)
        outs.append(seg * lax.rsqrt(ms + EPS))
    y_ref[...] = (jnp.concatenate(outs, axis=1) * nw_ref[...]).astype(y_ref.dtype)


def _ssd(xs, bc, dtp, zs, dt_bias, a_log, d_ssd, norm_w, bsz, seq):
    m, d_inner = xs.shape
    L = SSD_CHUNK
    nc = seq // L
    heads = d_inner // SSD_HEAD_DIM
    pad = LANES - heads
    row = lambda b, c: (b * nc + c, 0)
    const = lambda b, c: (0, 0)
    return pl.pallas_call(
        _ssd_kernel,
        out_shape=jax.ShapeDtypeStruct((m, d_inner), BF16),
        grid=(bsz, nc),
        in_specs=[pl.BlockSpec((L, d_inner), row),
                  pl.BlockSpec((L, bc.shape[1]), row),
                  pl.BlockSpec((L, LANES), row),
                  pl.BlockSpec((L, d_inner), row),
                  pl.BlockSpec((1, LANES), const),
                  pl.BlockSpec((1, LANES), const),
                  pl.BlockSpec((1, d_inner), const),
                  pl.BlockSpec((1, d_inner), const)],
        out_specs=pl.BlockSpec((L, d_inner), row),
        scratch_shapes=[pltpu.VMEM((SSD_GROUPS, SSD_STATE, d_inner // SSD_GROUPS), F32)],
        compiler_params=_params("arbitrary", "arbitrary"),
        name="ssd",
    )(xs, bc, dtp, zs,
      jnp.pad(dt_bias, (0, pad)).reshape(1, LANES), jnp.pad(a_log, (0, pad)).reshape(1, LANES),
      jnp.repeat(d_ssd, SSD_HEAD_DIM).reshape(1, d_inner), norm_w.reshape(1, d_inner))


def _s5_prep_kernel(lre_ref, lim_ref, ldt_ref, btr_ref, bti_ref, cnr_ref, cni_ref,
                    wend_ref, wcorr_ref, krev_ref, al_ref):
    L = S5_CHUNK
    q = S5_SLAB
    ns = S5_SLAB_STATE
    lr = jnp.minimum(lre_ref[...], -1e-4)
    li = lim_ref[...]
    dt = jnp.exp(ldt_ref[...])

    def power(k):
        mag = jnp.exp(lr * dt * float(k))
        ang = li * dt * float(k)
        return mag * jnp.cos(ang), mag * jnp.sin(ang)

    ar, ai = power(1)
    den = lr * lr + li * li
    nr = ar - 1.0
    kr = (nr * lr + ai * li) / den
    ki = (ai * lr - nr * li) / den

    rows = lax.broadcasted_iota(jnp.int32, (q, ns), 0) // S5_GROUP_SIZE
    cols = lax.broadcasted_iota(jnp.int32, (q, ns), 1) // S5_STATE
    same = rows == cols

    def block_diag(ref):
        return jnp.where(same, jnp.concatenate([ref[...]] * (ns // LANES), axis=1), 0.0)

    btr, bti = block_diag(btr_ref), block_diag(bti_ref)
    bbr = btr * kr - bti * ki
    bbi = btr * ki + bti * kr
    cr, ci = block_diag(cnr_ref), block_diag(cni_ref)

    for k in range(L):
        pr, pi = power(k)
        blk = jnp.concatenate([bbr * pr - bbi * pi, bbr * pi + bbi * pr], axis=1)
        wend_ref[(L - 1 - k) * q:(L - k) * q, :] = blk.astype(BF16)
    for l in range(L):
        pr, pi = power(l + 1)
        blk = jnp.concatenate([cr * pr - ci * pi, -(cr * pi) - ci * pr], axis=1)
        wcorr_ref[l * q:(l + 1) * q, :] = blk.astype(BF16)
    cstack = jnp.concatenate([cr, -ci], axis=1).astype(BF16)
    krev_ref[...] = _dot_nt(wend_ref[...], cstack).astype(BF16)
    pr, pi = power(L)
    al_ref[...] = jnp.concatenate([pr, pi], axis=1)


def _s5_prep(lam_re, lam_im, log_dt, b_re, b_im, c_re, c_im):
    g, p = lam_re.shape
    width = g * S5_GROUP_SIZE
    n_slabs = width // S5_SLAB
    ns = S5_SLAB_STATE
    lq = S5_CHUNK * S5_SLAB
    row_vec = lambda v: v.reshape(1, g * p)
    twice = lambda v: jnp.concatenate([v, v], axis=1)
    bt = lambda v: twice(jnp.swapaxes(v, 1, 2).reshape(width, p))
    cn = lambda v: twice(v.reshape(width, p))
    vec = pl.BlockSpec((None, 1, ns), lambda s: (s, 0, 0))
    mat = pl.BlockSpec((S5_SLAB, LANES), lambda s: (s, 0))
    vec3 = lambda v: row_vec(v).reshape(n_slabs, 1, ns)
    return pl.pallas_call(
        _s5_prep_kernel,
        out_shape=[jax.ShapeDtypeStruct((n_slabs, lq, 2 * ns), BF16),
                   jax.ShapeDtypeStruct((n_slabs, lq, 2 * ns), BF16),
                   jax.ShapeDtypeStruct((n_slabs, lq, S5_SLAB), BF16),
                   jax.ShapeDtypeStruct((n_slabs, 1, 2 * ns), F32)],
        grid=(n_slabs,),
        in_specs=[vec, vec, vec, mat, mat, mat, mat],
        out_specs=[pl.BlockSpec((None, lq, 2 * ns), lambda s: (s, 0, 0)),
                   pl.BlockSpec((None, lq, 2 * ns), lambda s: (s, 0, 0)),
                   pl.BlockSpec((None, lq, S5_SLAB), lambda s: (s, 0, 0)),
                   pl.BlockSpec((None, 1, 2 * ns), lambda s: (s, 0, 0))],
        compiler_params=_params("arbitrary"),
        name="s5prep",
    )(vec3(lam_re), vec3(lam_im), vec3(jnp.repeat(log_dt, p)),
      bt(b_re), bt(b_im), cn(c_re), cn(c_im))


def _gelu_tanh(v):
    return 0.5 * v * (1.0 + jnp.tanh(math.sqrt(2.0 / math.pi) * (v + 0.044715 * (v * v * v))))


def _s5_kernel(u_ref, wend_ref, wcorr_ref, krev_ref, al_ref, d_ref, y_ref, e_scr, cin_scr, st_scr,
               tok_scr, *, blocks_per_seq):
    L = S5_CHUNK
    q = S5_SLAB
    ns = S5_SLAB_STATE
    rows = u_ref.shape[0]

    @pl.when(pl.program_id(1) % blocks_per_seq == 0)
    def _():
        st_scr[...] = jnp.zeros_like(st_scr)

    u = u_ref[...]
    ub = u.astype(BF16)
    e_scr[...] = _dot(ub, wend_ref[...])
    alr = al_ref[:, :ns]
    ali = al_ref[:, ns:]

    def tile_step(t, st):
        base = pl.multiple_of(t * SUBLANES, SUBLANES)
        e = e_scr[pl.ds(base, SUBLANES), :]
        sr, si = st
        carried = []
        for r in range(SUBLANES):
            carried.append(jnp.concatenate([sr, si], axis=1))
            er = e[r:r + 1, :ns]
            ei = e[r:r + 1, ns:]
            sr, si = alr * sr - ali * si + er, alr * si + ali * sr + ei
        cin_scr[pl.ds(base, SUBLANES), :] = jnp.concatenate(carried, axis=0)
        return sr, si

    sr, si = lax.fori_loop(0, rows // SUBLANES, tile_step, (st_scr[:, :ns], st_scr[:, ns:]))
    st_scr[...] = jnp.concatenate([sr, si], axis=1)

    corr = _dot_nt(cin_scr[...].astype(BF16), wcorr_ref[...])
    intra = [_dot(ub[:, :(l + 1) * q], krev_ref[(L - 1 - l) * q:, :]) for l in range(L)]
    y = _gelu_tanh(jnp.concatenate(intra, axis=1) + corr + d_ref[...] * u)
    for c in range(tok_scr.shape[0]):
        for l in range(L):
            lo = l * q + c * LANES
            tok_scr[c, pl.ds(l, rows, stride=L), :] = y[:, lo:lo + LANES]
        y_ref[:, c * LANES:(c + 1) * LANES] = tok_scr[c].astype(y_ref.dtype)


def _s5(u2, prep, d_skip, bsz, seq):
    wend, wcorr, krev, al = prep
    n_slabs, chunks, lq = u2.shape
    L = S5_CHUNK
    q = S5_SLAB
    ns2 = 2 * S5_SLAB_STATE
    chunks_per_seq = seq // L
    rb = min(256, chunks_per_seq)
    blocks_per_seq = chunks_per_seq // rb
    d_t = jnp.tile(d_skip.reshape(n_slabs, 1, q), (1, 1, L))
    slab = lambda s, r: (s, 0, 0)
    blk = lambda s, r: (s, r, 0)
    return pl.pallas_call(
        functools.partial(_s5_kernel, blocks_per_seq=blocks_per_seq),
        out_shape=jax.ShapeDtypeStruct((n_slabs, chunks * L, q), BF16),
        grid=(n_slabs, chunks // rb),
        in_specs=[pl.BlockSpec((None, rb, lq), blk),
                  pl.BlockSpec((None, lq, ns2), slab),
                  pl.BlockSpec((None, lq, ns2), slab),
                  pl.BlockSpec((None, lq, q), slab),
                  pl.BlockSpec((None, 1, ns2), slab),
                  pl.BlockSpec((None, 1, lq), slab)],
        out_specs=pl.BlockSpec((None, rb * L, q), blk),
        scratch_shapes=[pltpu.VMEM((rb, ns2), F32), pltpu.VMEM((rb, ns2), F32),
                        pltpu.VMEM((1, ns2), F32), pltpu.VMEM((q // LANES, rb * L, LANES), F32)],
        compiler_params=_params("arbitrary", "arbitrary"),
        name="s5",
    )(u2, wend, wcorr, krev, al, d_t)


def _merge_kernel(ya_ref, yb_ref, ga_ref, gb_ref, x_ref, wa_ref, wg_ref, wo_ref, mods_ref, nw_ref,
                  xo_ref, h_ref, *, k_gate, k_next):
    d = x_ref.shape[1]
    p_a = _dot(ya_ref[...], wa_ref[...])
    glu = sum(_dot(yb_ref[s], wg_ref[s]) for s in range(yb_ref.shape[0]))
    p_b = glu[:, :d] * jax.nn.sigmoid(glu[:, d:])
    merged = jax.nn.sigmoid(ga_ref[...]) * p_a + jax.nn.sigmoid(gb_ref[...]) * p_b
    xn = x_ref[...] + mods_ref[k_gate:k_gate + 1, :] * _dot(merged.astype(BF16), wo_ref[...])
    xo_ref[...] = xn
    hn = _mod_rms(xn, nw_ref[...], mods_ref[k_next:k_next + 1, :], mods_ref[k_next + 1:k_next + 2, :])
    h_ref[...] = hn.astype(BF16)


def _merge(ya, yb_slabs, ga, gb, x2, w_a, w_glu, w_o, mods3, nw_next, k_gate, k_next, seq):
    m, d = x2.shape
    n_slabs, _, q = yb_slabs.shape
    tm = min(256, seq)
    per_b = seq // tm
    row = lambda i: (i, 0)
    const = lambda i: (0, 0)
    once = pl.Buffered(1)
    return pl.pallas_call(
        functools.partial(_merge_kernel, k_gate=k_gate, k_next=k_next),
        out_shape=[jax.ShapeDtypeStruct((m, d), F32), jax.ShapeDtypeStruct((m, d), BF16)],
        grid=(m // tm,),
        in_specs=[pl.BlockSpec((tm, ya.shape[1]), row),
                  pl.BlockSpec((n_slabs, tm, q), lambda i: (0, i, 0)),
                  pl.BlockSpec((tm, d), row),
                  pl.BlockSpec((tm, d), row),
                  pl.BlockSpec((tm, d), row),
                  pl.BlockSpec(w_a.shape, const, pipeline_mode=once),
                  pl.BlockSpec(w_glu.shape, lambda i: (0, 0, 0), pipeline_mode=once),
                  pl.BlockSpec(w_o.shape, const, pipeline_mode=once),
                  pl.BlockSpec((None, N_ADA, d), lambda i: (i // per_b, 0, 0)),
                  pl.BlockSpec((1, d), const)],
        out_specs=[pl.BlockSpec((tm, d), row), pl.BlockSpec((tm, d), row)],
        compiler_params=_params("arbitrary"),
        name="merge",
    )(ya, yb_slabs, ga, gb, x2, w_a, w_glu, w_o, mods3, nw_next.reshape(1, d))


def _ffn_weights(w_in, w_out):
    ff = w_out.shape[0]
    ffp = -(-ff // 512) * 512
    wa = jnp.pad(w_in[:, :ff].astype(BF16), ((0, 0), (0, ffp - ff)))
    wb = jnp.pad(w_in[:, ff:].astype(BF16), ((0, 0), (0, ffp - ff)))
    wo = jnp.pad(w_out.astype(BF16), ((0, ffp - ff), (0, 0)))
    return wa, wb, wo


def kernel(x, c, w_ada, b_ada, norm_ffn1, w_ffn1_in, w_ffn1_out, norm_mix, w_in, conv_w, conv_b, dt_bias, a_log, d_ssd, ssd_norm_w, w_a_proj, s5_lambda_re, s5_lambda_im, s5_b_re, s5_b_im, s5_c_re, s5_c_im, s5_d, s5_log_dt, w_b_glu, w_out, norm_ffn2, w_ffn2_in, w_ffn2_out, norm_final):
    bsz, seq, d = x.shape
    depth = w_ada.shape[0]
    m = bsz * seq
    d_inner = ssd_norm_w.shape[1]
    conv_dim = conv_w.shape[2]
    heads = dt_bias.shape[1]
    s5_width = w_b_glu.shape[1]
    off_xbc = d_inner
    off_dt = off_xbc + conv_dim
    off_u = off_dt + heads
    off_g = off_u + s5_width

    assert depth == 1, "the epilogue fusion below is written for a single layer"
    l = 0
    x2 = x.reshape(m, d)
    mods3 = _mods(c, w_ada[l], b_ada[l]).reshape(bsz, N_ADA, d)
    h = _prenorm(x2, norm_ffn1[l], mods3, 0, seq)

    ff = w_ffn2_out.shape[1]
    ffp = -(-ff // FFN_HIDDEN) * FFN_HIDDEN
    whole = lambda w: ((0, w.shape[1], w.shape[1]),)
    jobs = (_CastJob(w_ffn2_in[l], ((0, ff, ffp), (ff, 2 * ff, ffp)), d),
            _CastJob(w_ffn2_out[l], whole(w_ffn2_out[l]), ffp),
            _CastJob(w_a_proj[l], whole(w_a_proj[l]), d_inner),
            _CastJob(w_out[l], whole(w_out[l]), d),
            _CastJob(w_b_glu[l], whole(w_b_glu[l]), s5_width),
            _CastJob(w_in[l], whole(w_in[l]), d))
    x2, h, wa2, wb2, wo2, w_a, w_o, w_glu, wi = _ffn(
        h, x2, *_ffn_weights(w_ffn1_in[l], w_ffn1_out[l]), mods3, norm_mix[l], 2, 3, seq, jobs)

    off_bc = off_xbc + d_inner
    zs = _proj(h, wi, 0, d_inner, 1024, act=True)
    xs = _proj_conv(h, wi, off_xbc, conv_w[l][:, :d_inner], conv_b[l][:d_inner], 1024, seq, F32)
    bc = _proj_conv(h, wi, off_bc, conv_w[l][:, d_inner:], conv_b[l][d_inner:], 1024, seq, BF16)
    dtp = _proj(h, wi, off_dt, LANES, LANES)
    w_rest = wi[:, off_u:]
    u = _proj_chunked(h, w_rest, s5_width)
    ga = _proj(h, w_rest, s5_width, d, 1024)
    gb = _proj(h, w_rest, s5_width + d, d, 1024)

    ya = _ssd(xs, bc, dtp, zs, dt_bias[l], a_log[l], d_ssd[l], ssd_norm_w[l], bsz, seq)
    prep = _s5_prep(s5_lambda_re[l], s5_lambda_im[l], s5_log_dt[l],
                    s5_b_re[l], s5_b_im[l], s5_c_re[l], s5_c_im[l])
    yb = _s5(u, prep, s5_d[l], bsz, seq)

    x2, h = _merge(ya, yb, ga, gb, x2, w_a, w_glu.reshape(s5_width // S5_SLAB, S5_SLAB, 2 * d), w_o,
                   mods3, norm_ffn2[l], 5, 6, seq)

    (x2,) = _ffn(h, x2, wa2, wb2, wo2, mods3, norm_final, 8, None, seq)
    return x2.reshape(bsz, seq, d)
```

```python
import functools
import math
from typing import NamedTuple

import jax
import jax.numpy as jnp
from jax import lax
from jax.experimental import pallas as pl
from jax.experimental.pallas import tpu as pltpu

F32 = jnp.float32
BF16 = jnp.bfloat16
EPS = 1e-6

VMEM_LIMIT_BYTES = 56 * 1024 * 1024
LANES = 128
SUBLANES = 8
BF16_ROWS = 16

SSD_HEAD_DIM = 64
SSD_GROUPS = 4
SSD_STATE = 128
SSD_CONV = 4
SSD_CHUNK = 128
S5_GROUP_SIZE = 16
S5_STATE = 64
S5_SLAB = 256
S5_GROUPS_PER_SLAB = S5_SLAB // S5_GROUP_SIZE
S5_SLAB_STATE = S5_GROUPS_PER_SLAB * S5_STATE
S5_CHUNK = 8
N_ADA = 9


def _params(*sem):
    return pltpu.CompilerParams(dimension_semantics=sem, vmem_limit_bytes=VMEM_LIMIT_BYTES)


def _silu(v):
    return v * jax.nn.sigmoid(v)


def _dot(a, b):
    return jnp.dot(a, b, preferred_element_type=F32)


def _dot_nt(a, b):
    return lax.dot_general(a, b, (((1,), (1,)), ((), ())), preferred_element_type=F32)


def _mod_rms(x, nw, shift, scale):
    ms = jnp.mean(x * x, axis=-1, keepdims=True)
    return (x * lax.rsqrt(ms + EPS) * nw) * (1.0 + scale) + shift


def _mods_kernel(ct_ref, w_ref, b_ref, o_ref):
    ca = _silu(ct_ref[...])
    w = w_ref[...]
    rows = [jnp.sum(ca[:, b:b + 1] * w, axis=0, keepdims=True) for b in range(ca.shape[1])]
    o_ref[...] = jnp.concatenate(rows, axis=0) + b_ref[...]


def _mods(c, w_ada, b_ada):
    bsz, d = c.shape
    n = w_ada.shape[1]
    tn = 1024
    return pl.pallas_call(
        _mods_kernel,
        out_shape=jax.ShapeDtypeStruct((bsz, n), F32),
        grid=(n // tn,),
        in_specs=[pl.BlockSpec((d, bsz), lambda j: (0, 0)),
                  pl.BlockSpec((d, tn), lambda j: (0, j)),
                  pl.BlockSpec((1, tn), lambda j: (0, j))],
        out_specs=pl.BlockSpec((bsz, tn), lambda j: (0, j)),
        compiler_params=_params("arbitrary"),
        name="mods",
    )(c.T, w_ada, b_ada.reshape(1, n))


def _prenorm_kernel(x_ref, nw_ref, mods_ref, h_ref, *, k):
    h = _mod_rms(x_ref[...], nw_ref[...], mods_ref[k:k + 1, :], mods_ref[k + 1:k + 2, :])
    h_ref[...] = h.astype(h_ref.dtype)


def _prenorm(x2, nw, mods3, k, seq):
    m, d = x2.shape
    tm = min(512, seq)
    per_b = seq // tm
    return pl.pallas_call(
        functools.partial(_prenorm_kernel, k=k),
        out_shape=jax.ShapeDtypeStruct((m, d), BF16),
        grid=(m // tm,),
        in_specs=[pl.BlockSpec((tm, d), lambda i: (i, 0)),
                  pl.BlockSpec((1, d), lambda i: (0, 0)),
                  pl.BlockSpec((None, N_ADA, d), lambda i: (i // per_b, 0, 0))],
        out_specs=pl.BlockSpec((tm, d), lambda i: (i, 0)),
        compiler_params=_params("arbitrary"),
        name="prenorm",
    )(x2, nw.reshape(1, d), mods3)


FFN_ROWS = 1024
FFN_HIDDEN = 512
FFN_EPI_CHUNKS = 8


class _CastJob(NamedTuple):
    src: jax.Array
    cols: tuple
    out_rows: int

    def chunk_rows(self, slots):
        need = -(-self.out_rows // slots)
        rows = next(r for r in range(BF16_ROWS, self.out_rows + 1, BF16_ROWS)
                    if r >= need and self.out_rows % r == 0)
        return rows


def _cast_chunk(job, rows, chunk, src_ref, dst_refs):
    src_rows = job.src.shape[0]
    out_chunk = jnp.minimum(chunk, job.out_rows // rows - 1)
    partial = src_rows % rows != 0 or job.out_rows != src_rows
    if partial:
        r = lax.broadcasted_iota(jnp.int32, (rows, 1), 0) + out_chunk * rows
        keep = r < src_rows
    for (c0, c1, width), dst in zip(job.cols, dst_refs):
        v = src_ref[:, c0:c1]
        if partial:
            v = jnp.where(keep, v, 0.0)
        dst[:, :c1 - c0] = v.astype(BF16)
        if width > c1 - c0:
            dst[:, c1 - c0:] = jnp.zeros((rows, width - (c1 - c0)), BF16)


def _ffn_kernel(h_ref, x_ref, wa_ref, wb_ref, wo_ref, mods_ref, nw_ref, *refs, k_gate, k_next, n_tiles,
                jobs):
    n_job_out = sum(len(job.cols) for job, _ in jobs)
    job_src = refs[:len(jobs)]
    refs = refs[len(jobs):]
    acc = refs[-1]
    out_refs = refs[:len(refs) - 1 - n_job_out]
    job_dst = refs[len(out_refs):-1]
    i = pl.program_id(0)
    j = pl.program_id(1)
    slot = i % 2

    def casts():
        chunk = i * FFN_EPI_CHUNKS + jnp.minimum(j, FFN_EPI_CHUNKS - 1)
        k = 0
        for (job, rows), src_ref in zip(jobs, job_src):
            _cast_chunk(job, rows, chunk, src_ref, job_dst[k:k + len(job.cols)])
            k += len(job.cols)

    @pl.when((i == 0) & (j == 0))
    def _():
        acc[...] = jnp.zeros_like(acc)

    def epilogue():
        rows = x_ref.shape[0]
        r0 = pl.multiple_of(jnp.minimum(j, FFN_EPI_CHUNKS - 1) * rows, rows)
        done = acc[1 - slot, pl.ds(r0, rows), :]
        xn = x_ref[...] + 0.5 * mods_ref[k_gate:k_gate + 1, :] * done
        if k_next is None:
            ms = jnp.mean(xn * xn, axis=-1, keepdims=True)
            out_refs[0][...] = xn * lax.rsqrt(ms + EPS) * nw_ref[...]
        else:
            out_refs[0][...] = xn
            hn = _mod_rms(xn, nw_ref[...], mods_ref[k_next:k_next + 1, :],
                          mods_ref[k_next + 1:k_next + 2, :])
            out_refs[1][...] = hn.astype(BF16)

    def matmuls():
        h = h_ref[...]
        a = _dot(h, wa_ref[...])
        b = _dot(h, wb_ref[...])
        act = (_silu(a) * b).astype(BF16)
        acc[slot] = jnp.where(j == 0, 0.0, acc[slot]) + _dot(act, wo_ref[...])

    @pl.when(i == 0)
    def _():
        casts()
        matmuls()

    @pl.when((i > 0) & (i < n_tiles))
    def _():
        epilogue()
        casts()
        matmuls()

    @pl.when(i == n_tiles)
    def _():
        epilogue()


def _ffn(h, x2, wa, wb, wo, mods3, nw_next, k_gate, k_next, seq, cast_jobs=()):
    m, d = x2.shape
    ffp = wa.shape[1]
    tm = min(FFN_ROWS, seq)
    tf = FFN_HIDDEN
    n_tiles = m // tm
    n_hidden = ffp // tf
    ec = FFN_EPI_CHUNKS
    rows = tm // ec
    per_b = seq // tm
    assert n_hidden >= ec

    def done_chunk(i, j):
        return (jnp.where(i == 0, 0, (i - 1) * ec + jnp.minimum(j, ec - 1)), 0)

    def hidden(i, j):
        return jnp.where(i == n_tiles, n_hidden - 1, j)

    def cast_chunk(i, j):
        return jnp.where(i == n_tiles, n_tiles * ec - 1, i * ec + jnp.minimum(j, ec - 1))

    out_shape = [jax.ShapeDtypeStruct((m, d), F32)]
    out_specs = [pl.BlockSpec((rows, d), done_chunk)]
    if k_next is not None:
        out_shape.append(jax.ShapeDtypeStruct((m, d), BF16))
        out_specs.append(pl.BlockSpec((rows, d), done_chunk))
    in_specs = [pl.BlockSpec((tm, d), lambda i, j: (jnp.minimum(i, n_tiles - 1), 0)),
                pl.BlockSpec((rows, d), done_chunk),
                pl.BlockSpec((d, tf), lambda i, j: (0, hidden(i, j))),
                pl.BlockSpec((d, tf), lambda i, j: (0, hidden(i, j))),
                pl.BlockSpec((tf, d), lambda i, j: (hidden(i, j), 0)),
                pl.BlockSpec((None, N_ADA, d), lambda i, j: (jnp.maximum(i - 1, 0) // per_b, 0, 0)),
                pl.BlockSpec((1, d), lambda i, j: (0, 0))]
    sized_jobs = []
    for job in cast_jobs:
        r = job.chunk_rows(n_tiles * ec)
        sized_jobs.append((job._replace(src=jax.ShapeDtypeStruct(job.src.shape, F32)), r))
        last_out = job.out_rows // r - 1
        last_in = (job.src.shape[0] - 1) // r
        in_specs.append(pl.BlockSpec(
            (r, job.src.shape[1]),
            lambda i, j, lo=last_out, li=last_in: (jnp.minimum(jnp.minimum(cast_chunk(i, j), lo), li), 0)))
        for _, _, width in job.cols:
            out_shape.append(jax.ShapeDtypeStruct((job.out_rows, width), BF16))
            out_specs.append(pl.BlockSpec(
                (r, width), lambda i, j, lo=last_out: (jnp.minimum(cast_chunk(i, j), lo), 0)))
    return pl.pallas_call(
        functools.partial(_ffn_kernel, k_gate=k_gate, k_next=k_next, n_tiles=n_tiles,
                          jobs=tuple(sized_jobs)),
        out_shape=out_shape,
        grid=(n_tiles + 1, n_hidden),
        in_specs=in_specs,
        out_specs=out_specs,
        scratch_shapes=[pltpu.VMEM((2, tm, d), F32)],
        compiler_params=_params("arbitrary", "arbitrary"),
        name="ffn",
    )(h, x2, wa, wb, wo, mods3, nw_next.reshape(1, d), *[job.src for job in cast_jobs])


PROJ_ROWS = 1024
PROJ_SUB = 256


def _proj_kernel(h_ref, w_ref, o_ref, *, act):
    y = _dot_nt(h_ref[...], w_ref[...])
    if act:
        y = _silu(y)
    o_ref[...] = y.astype(o_ref.dtype)


def _proj(h, wt, col0, n, tn, act=False):
    m, d = h.shape
    tm = min(PROJ_ROWS, m)
    cb = col0 // tn
    return pl.pallas_call(
        functools.partial(_proj_kernel, act=act),
        out_shape=jax.ShapeDtypeStruct((m, n), F32),
        grid=(m // tm, n // tn),
        in_specs=[pl.BlockSpec((tm, d), lambda i, j: (i, 0)),
                  pl.BlockSpec((tn, d), lambda i, j: (cb + j, 0))],
        out_specs=pl.BlockSpec((tm, tn), lambda i, j: (i, j)),
        compiler_params=_params("arbitrary", "arbitrary"),
        name="proj",
    )(h, wt)


def _proj_conv_kernel(h_ref, w_ref, cw_ref, cb_ref, o_ref, halo, *, tiles_per_seq):
    i = pl.program_id(0)
    j = pl.program_id(1)
    tm = h_ref.shape[0]
    sub = min(PROJ_SUB, tm)
    taps = SSD_CONV - 1

    @pl.when(i % tiles_per_seq == 0)
    def _():
        halo[j] = jnp.zeros(halo.shape[1:], F32)

    prev = halo[j]
    w = w_ref[...]
    cw = cw_ref[...]
    cb = cb_ref[...]
    for r in range(tm // sub):
        raw = _dot_nt(h_ref[r * sub:(r + 1) * sub, :], w)
        ext = jnp.concatenate([prev, raw], axis=0)
        conv = cb + cw[taps:taps + 1, :] * raw
        for k in range(taps):
            lo = SUBLANES - taps + k
            conv = conv + cw[k:k + 1, :] * ext[lo:lo + sub, :]
        o_ref[r * sub:(r + 1) * sub, :] = _silu(conv).astype(o_ref.dtype)
        prev = raw[sub - SUBLANES:, :]
    halo[j] = prev


def _proj_conv(h, wt, col0, conv_w, conv_b, tn, seq, out_dtype):
    m, d = h.shape
    n = conv_w.shape[1]
    tm = min(PROJ_ROWS, seq)
    cb = col0 // tn
    return pl.pallas_call(
        functools.partial(_proj_conv_kernel, tiles_per_seq=seq // tm),
        out_shape=jax.ShapeDtypeStruct((m, n), out_dtype),
        grid=(m // tm, n // tn),
        in_specs=[pl.BlockSpec((tm, d), lambda i, j: (i, 0)),
                  pl.BlockSpec((tn, d), lambda i, j: (cb + j, 0)),
                  pl.BlockSpec((SSD_CONV, tn), lambda i, j: (0, j)),
                  pl.BlockSpec((1, tn), lambda i, j: (0, j))],
        out_specs=pl.BlockSpec((tm, tn), lambda i, j: (i, j)),
        scratch_shapes=[pltpu.VMEM((n // tn, SUBLANES, tn), F32)],
        compiler_params=_params("arbitrary", "arbitrary"),
        name="projconv",
    )(h, wt, conv_w, conv_b.reshape(1, n))


def _proj_chunked_kernel(h_ref, w_ref, o_ref, scr):
    n_slabs, rows, lq = o_ref.shape
    q = S5_SLAB
    L = lq // q
    y = _dot_nt(h_ref[...], w_ref[...])
    for c in range(scr.shape[0]):
        scr[c] = y[:, c * LANES:(c + 1) * LANES]
    for c in range(scr.shape[0]):
        s, off = divmod(c * LANES, q)
        for l in range(L):
            o_ref[s, :, l * q + off:l * q + off + LANES] = scr[c, pl.ds(l, rows, stride=L), :]


def _proj_chunked(h, wt, n):
    m, d = h.shape
    tm = min(PROJ_ROWS, m)
    L = S5_CHUNK
    n_slabs = n // S5_SLAB
    return pl.pallas_call(
        _proj_chunked_kernel,
        out_shape=jax.ShapeDtypeStruct((n_slabs, m // L, L * S5_SLAB), F32),
        grid=(m // tm,),
        in_specs=[pl.BlockSpec((tm, d), lambda i: (i, 0)),
                  pl.BlockSpec((n, d), lambda i: (0, 0))],
        out_specs=pl.BlockSpec((n_slabs, tm // L, L * S5_SLAB), lambda i: (0, i, 0)),
        scratch_shapes=[pltpu.VMEM((n // LANES, tm, LANES), F32)],
        compiler_params=_params("arbitrary"),
        name="projchunk",
    )(h, wt)


def _split3(v):
    hi = v.astype(BF16)
    r = v - hi.astype(F32)
    mid = r.astype(BF16)
    lo = (r - mid.astype(F32)).astype(BF16)
    return hi, mid, lo


def _pair(v, j, first):
    return jnp.where(first, v[:, 2 * j:2 * j + 1], v[:, 2 * j + 1:2 * j + 2])


def _ssd_kernel(xs_ref, bc_ref, dt_ref, zs_ref, dtb_ref, alog_ref, dsk_ref, nw_ref, y_ref, state):
    L = SSD_CHUNK
    d_inner = xs_ref.shape[1]
    n_state = SSD_STATE
    gw = d_inner // SSD_GROUPS
    pairs_per_group = gw // LANES

    @pl.when(pl.program_id(1) == 0)
    def _():
        state[...] = jnp.zeros_like(state)

    dtr = dt_ref[...] + dtb_ref[...]
    dt = jnp.maximum(dtr, 0.0) + jnp.log1p(jnp.exp(-jnp.abs(dtr)))
    da = dt * (-jnp.exp(alog_ref[...]))
    row = lax.broadcasted_iota(jnp.int32, (L, L), 0)
    col = lax.broadcasted_iota(jnp.int32, (L, L), 1)
    causal = row >= col
    tril = jnp.where(causal, 1.0, 0.0).astype(BF16)
    cs = sum(_dot(tril, part) for part in _split3(da))
    cs_last = cs[L - 1:L, :]
    ecs = jnp.exp(cs)
    ecl = jnp.exp(cs_last)
    cs_t = cs.T
    dt_t = dt.T
    wst_t = (dt * jnp.exp(cs_last - cs)).T

    lane = lax.broadcasted_iota(jnp.int32, (L, LANES), 1)
    first = lane < SSD_HEAD_DIM
    first_row = first[0:1, :]

    y_parts = []
    for g in range(SSD_GROUPS):
        bm_g = bc_ref[:, g * n_state:(g + 1) * n_state]
        cm_g = bc_ref[:, (SSD_GROUPS + g) * n_state:(SSD_GROUPS + g + 1) * n_state]
        cb = _dot_nt(cm_g, bm_g)
        bm_t = bm_g.astype(F32).T
        y_off = _dot(cm_g, state[g].astype(BF16))
        for jj in range(pairs_per_group):
            j = g * pairs_per_group + jj
            h0, h1 = 2 * j, 2 * j + 1
            cols = slice(j * LANES, (j + 1) * LANES)
            gcols = slice(jj * LANES, (jj + 1) * LANES)
            xs_p = xs_ref[:, cols]
            xb = xs_p.astype(BF16)
            zero = jnp.zeros_like(xb)
            rhs = jnp.concatenate([jnp.where(first, xb, zero), jnp.where(first, zero, xb)], axis=0)
            lhs_y = []
            lhs_s = []
            for h in (h0, h1):
                dec = jnp.exp(jnp.where(causal, cs[:, h:h + 1] - cs_t[h:h + 1, :], -jnp.inf))
                lhs_y.append((cb * dec * dt_t[h:h + 1, :]).astype(BF16))
                lhs_s.append((bm_t * wst_t[h:h + 1, :]).astype(BF16))
            y_p = (_dot(jnp.concatenate(lhs_y, axis=1), rhs)
                   + _pair(ecs, j, first) * y_off[:, gcols]
                   + dsk_ref[:, cols] * xs_p)
            y_parts.append(y_p)
            st_new = _dot(jnp.concatenate(lhs_s, axis=1), rhs)
            state[g, :, gcols] = _pair(ecl, j, first_row) * state[g, :, gcols] + st_new

    yz = jnp.concatenate(y_parts, axis=1) * zs_ref[...]
    outs = []
    for g in range(SSD_GROUPS):
        seg = yz[:, g * gw:(g + 1) * gw]
        ms = jnp.mean(seg * seg, axis=-1, keepdims=True)
        outs.append(seg * lax.rsqrt(ms + EPS))
    y_ref[...] = (jnp.concatenate(outs, axis=1) * nw_ref[...]).astype(y_ref.dtype)


def _ssd(xs, bc, dtp, zs, dt_bias, a_log, d_ssd, norm_w, bsz, seq):
    m, d_inner = xs.shape
    L = SSD_CHUNK
    nc = seq // L
    heads = d_inner // SSD_HEAD_DIM
    pad = LANES - heads
    row = lambda b, c: (b * nc + c, 0)
    const = lambda b, c: (0, 0)
    return pl.pallas_call(
        _ssd_kernel,
        out_shape=jax.ShapeDtypeStruct((m, d_inner), BF16),
        grid=(bsz, nc),
        in_specs=[pl.BlockSpec((L, d_inner), row),
                  pl.BlockSpec((L, bc.shape[1]), row),
                  pl.BlockSpec((L, LANES), row),
                  pl.BlockSpec((L, d_inner), row),
                  pl.BlockSpec((1, LANES), const),
                  pl.BlockSpec((1, LANES), const),
                  pl.BlockSpec((1, d_inner), const),
                  pl.BlockSpec((1, d_inner), const)],
        out_specs=pl.BlockSpec((L, d_inner), row),
        scratch_shapes=[pltpu.VMEM((SSD_GROUPS, SSD_STATE, d_inner // SSD_GROUPS), F32)],
        compiler_params=_params("arbitrary", "arbitrary"),
        name="ssd",
    )(xs, bc, dtp, zs,
      jnp.pad(dt_bias, (0, pad)).reshape(1, LANES), jnp.pad(a_log, (0, pad)).reshape(1, LANES),
      jnp.repeat(d_ssd, SSD_HEAD_DIM).reshape(1, d_inner), norm_w.reshape(1, d_inner))


def _s5_prep_kernel(lre_ref, lim_ref, ldt_ref, btr_ref, bti_ref, cnr_ref, cni_ref,
                    wend_ref, wcorr_ref, krev_ref, al_ref):
    L = S5_CHUNK
    q = S5_SLAB
    ns = S5_SLAB_STATE
    lr = jnp.minimum(lre_ref[...], -1e-4)
    li = lim_ref[...]
    dt = jnp.exp(ldt_ref[...])

    def power(k):
        mag = jnp.exp(lr * dt * float(k))
        ang = li * dt * float(k)
        return mag * jnp.cos(ang), mag * jnp.sin(ang)

    ar, ai = power(1)
    den = lr * lr + li * li
    nr = ar - 1.0
    kr = (nr * lr + ai * li) / den
    ki = (ai * lr - nr * li) / den

    rows = lax.broadcasted_iota(jnp.int32, (q, ns), 0) // S5_GROUP_SIZE
    cols = lax.broadcasted_iota(jnp.int32, (q, ns), 1) // S5_STATE
    same = rows == cols

    def block_diag(ref):
        return jnp.where(same, jnp.concatenate([ref[...]] * (ns // LANES), axis=1), 0.0)

    btr, bti = block_diag(btr_ref), block_diag(bti_ref)
    bbr = btr * kr - bti * ki
    bbi = btr * ki + bti * kr
    cr, ci = block_diag(cnr_ref), block_diag(cni_ref)

    for k in range(L):
        pr, pi = power(k)
        blk = jnp.concatenate([bbr * pr - bbi * pi, bbr * pi + bbi * pr], axis=1)
        wend_ref[(L - 1 - k) * q:(L - k) * q, :] = blk.astype(BF16)
    for l in range(L):
        pr, pi = power(l + 1)
        blk = jnp.concatenate([cr * pr - ci * pi, -(cr * pi) - ci * pr], axis=1)
        wcorr_ref[l * q:(l + 1) * q, :] = blk.astype(BF16)
    cstack = jnp.concatenate([cr, -ci], axis=1).astype(BF16)
    krev_ref[...] = _dot_nt(wend_ref[...], cstack).astype(BF16)
    pr, pi = power(L)
    al_ref[...] = jnp.concatenate([pr, pi], axis=1)


def _s5_prep(lam_re, lam_im, log_dt, b_re, b_im, c_re, c_im):
    g, p = lam_re.shape
    width = g * S5_GROUP_SIZE
    n_slabs = width // S5_SLAB
    ns = S5_SLAB_STATE
    lq = S5_CHUNK * S5_SLAB
    row_vec = lambda v: v.reshape(1, g * p)
    twice = lambda v: jnp.concatenate([v, v], axis=1)
    bt = lambda v: twice(jnp.swapaxes(v, 1, 2).reshape(width, p))
    cn = lambda v: twice(v.reshape(width, p))
    vec = pl.BlockSpec((None, 1, ns), lambda s: (s, 0, 0))
    mat = pl.BlockSpec((S5_SLAB, LANES), lambda s: (s, 0))
    vec3 = lambda v: row_vec(v).reshape(n_slabs, 1, ns)
    return pl.pallas_call(
        _s5_prep_kernel,
        out_shape=[jax.ShapeDtypeStruct((n_slabs, lq, 2 * ns), BF16),
                   jax.ShapeDtypeStruct((n_slabs, lq, 2 * ns), BF16),
                   jax.ShapeDtypeStruct((n_slabs, lq, S5_SLAB), BF16),
                   jax.ShapeDtypeStruct((n_slabs, 1, 2 * ns), F32)],
        grid=(n_slabs,),
        in_specs=[vec, vec, vec, mat, mat, mat, mat],
        out_specs=[pl.BlockSpec((None, lq, 2 * ns), lambda s: (s, 0, 0)),
                   pl.BlockSpec((None, lq, 2 * ns), lambda s: (s, 0, 0)),
                   pl.BlockSpec((None, lq, S5_SLAB), lambda s: (s, 0, 0)),
                   pl.BlockSpec((None, 1, 2 * ns), lambda s: (s, 0, 0))],
        compiler_params=_params("arbitrary"),
        name="s5prep",
    )(vec3(lam_re), vec3(lam_im), vec3(jnp.repeat(log_dt, p)),
      bt(b_re), bt(b_im), cn(c_re), cn(c_im))


def _gelu_tanh(v):
    return 0.5 * v * (1.0 + jnp.tanh(math.sqrt(2.0 / math.pi) * (v + 0.044715 * (v * v * v))))


def _s5_kernel(u_ref, wend_ref, wcorr_ref, krev_ref, al_ref, d_ref, y_ref, e_scr, cin_scr, st_scr,
               tok_scr, *, blocks_per_seq):
    L = S5_CHUNK
    q = S5_SLAB
    ns = S5_SLAB_STATE
    rows = u_ref.shape[0]

    @pl.when(pl.program_id(1) % blocks_per_seq == 0)
    def _():
        st_scr[...] = jnp.zeros_like(st_scr)

    u = u_ref[...]
    ub = u.astype(BF16)
    e_scr[...] = _dot(ub, wend_ref[...])
    alr = al_ref[:, :ns]
    ali = al_ref[:, ns:]

    def tile_step(t, st):
        base = pl.multiple_of(t * SUBLANES, SUBLANES)
        e = e_scr[pl.ds(base, SUBLANES), :]
        sr, si = st
        carried = []
        for r in range(SUBLANES):
            carried.append(jnp.concatenate([sr, si], axis=1))
            er = e[r:r + 1, :ns]
            ei = e[r:r + 1, ns:]
            sr, si = alr * sr - ali * si + er, alr * si + ali * sr + ei
        cin_scr[pl.ds(base, SUBLANES), :] = jnp.concatenate(carried, axis=0)
        return sr, si

    sr, si = lax.fori_loop(0, rows // SUBLANES, tile_step, (st_scr[:, :ns], st_scr[:, ns:]))
    st_scr[...] = jnp.concatenate([sr, si], axis=1)

    corr = _dot_nt(cin_scr[...].astype(BF16), wcorr_ref[...])
    intra = [_dot(ub[:, :(l + 1) * q], krev_ref[(L - 1 - l) * q:, :]) for l in range(L)]
    y = _gelu_tanh(jnp.concatenate(intra, axis=1) + corr + d_ref[...] * u)
    for c in range(tok_scr.shape[0]):
        for l in range(L):
            lo = l * q + c * LANES
            tok_scr[c, pl.ds(l, rows, stride=L), :] = y[:, lo:lo + LANES]
        y_ref[:, c * LANES:(c + 1) * LANES] = tok_scr[c].astype(y_ref.dtype)


def _s5(u2, prep, d_skip, bsz, seq):
    wend, wcorr, krev, al = prep
    n_slabs, chunks, lq = u2.shape
    L = S5_CHUNK
    q = S5_SLAB
    ns2 = 2 * S5_SLAB_STATE
    chunks_per_seq = seq // L
    rb = min(256, chunks_per_seq)
    blocks_per_seq = chunks_per_seq // rb
    d_t = jnp.tile(d_skip.reshape(n_slabs, 1, q), (1, 1, L))
    slab = lambda s, r: (s, 0, 0)
    blk = lambda s, r: (s, r, 0)
    return pl.pallas_call(
        functools.partial(_s5_kernel, blocks_per_seq=blocks_per_seq),
        out_shape=jax.ShapeDtypeStruct((n_slabs, chunks * L, q), BF16),
        grid=(n_slabs, chunks // rb),
        in_specs=[pl.BlockSpec((None, rb, lq), blk),
                  pl.BlockSpec((None, lq, ns2), slab),
                  pl.BlockSpec((None, lq, ns2), slab),
                  pl.BlockSpec((None, lq, q), slab),
                  pl.BlockSpec((None, 1, ns2), slab),
                  pl.BlockSpec((None, 1, lq), slab)],
        out_specs=pl.BlockSpec((None, rb * L, q), blk),
        scratch_shapes=[pltpu.VMEM((rb, ns2), F32), pltpu.VMEM((rb, ns2), F32),
                        pltpu.VMEM((1, ns2), F32), pltpu.VMEM((q // LANES, rb * L, LANES), F32)],
        compiler_params=_params("arbitrary", "arbitrary"),
        name="s5",
    )(u2, wend, wcorr, krev, al, d_t)


def _merge_kernel(ya_ref, yb_ref, ga_ref, gb_ref, x_ref, wa_ref, wg_ref, wo_ref, mods_ref, nw_ref,
                  xo_ref, h_ref, *, k_gate, k_next):
    d = x_ref.shape[1]
    p_a = _dot(ya_ref[...], wa_ref[...])
    glu = sum(_dot(yb_ref[s], wg_ref[s]) for s in range(yb_ref.shape[0]))
    p_b = glu[:, :d] * jax.nn.sigmoid(glu[:, d:])
    merged = jax.nn.sigmoid(ga_ref[...]) * p_a + jax.nn.sigmoid(gb_ref[...]) * p_b
    xn = x_ref[...] + mods_ref[k_gate:k_gate + 1, :] * _dot(merged.astype(BF16), wo_ref[...])
    xo_ref[...] = xn
    hn = _mod_rms(xn, nw_ref[...], mods_ref[k_next:k_next + 1, :], mods_ref[k_next + 1:k_next + 2, :])
    h_ref[...] = hn.astype(BF16)


def _merge(ya, yb_slabs, ga, gb, x2, w_a, w_glu, w_o, mods3, nw_next, k_gate, k_next, seq):
    m, d = x2.shape
    n_slabs, _, q = yb_slabs.shape
    tm = min(256, seq)
    per_b = seq // tm
    row = lambda i: (i, 0)
    const = lambda i: (0, 0)
    once = pl.Buffered(1)
    return pl.pallas_call(
        functools.partial(_merge_kernel, k_gate=k_gate, k_next=k_next),
        out_shape=[jax.ShapeDtypeStruct((m, d), F32), jax.ShapeDtypeStruct((m, d), BF16)],
        grid=(m // tm,),
        in_specs=[pl.BlockSpec((tm, ya.shape[1]), row),
                  pl.BlockSpec((n_slabs, tm, q), lambda i: (0, i, 0)),
                  pl.BlockSpec((tm, d), row),
                  pl.BlockSpec((tm, d), row),
                  pl.BlockSpec((tm, d), row),
                  pl.BlockSpec(w_a.shape, const, pipeline_mode=once),
                  pl.BlockSpec(w_glu.shape, lambda i: (0, 0, 0), pipeline_mode=once),
                  pl.BlockSpec(w_o.shape, const, pipeline_mode=once),
                  pl.BlockSpec((None, N_ADA, d), lambda i: (i // per_b, 0, 0)),
                  pl.BlockSpec((1, d), const)],
        out_specs=[pl.BlockSpec((tm, d), row), pl.BlockSpec((tm, d), row)],
        compiler_params=_params("arbitrary"),
        name="merge",
    )(ya, yb_slabs, ga, gb, x2, w_a, w_glu, w_o, mods3, nw_next.reshape(1, d))


CAST_STEPS = 16


def _cast_kernel(*refs, jobs):
    chunk = pl.program_id(0)
    src = refs[:len(jobs)]
    dst = refs[len(jobs):]
    k = 0
    for (job, rows), src_ref in zip(jobs, src):
        _cast_chunk(job, rows, chunk, src_ref, dst[k:k + len(job.cols)])
        k += len(job.cols)


def _cast_weights(cast_jobs):
    in_specs, out_specs, out_shape, sized = [], [], [], []
    for job in cast_jobs:
        r = job.chunk_rows(CAST_STEPS)
        sized.append((job._replace(src=jax.ShapeDtypeStruct(job.src.shape, F32)), r))
        last_out = job.out_rows // r - 1
        last_in = (job.src.shape[0] - 1) // r
        in_specs.append(pl.BlockSpec(
            (r, job.src.shape[1]), lambda c, lo=last_out, li=last_in: (jnp.minimum(jnp.minimum(c, lo), li), 0)))
        for _, _, width in job.cols:
            out_shape.append(jax.ShapeDtypeStruct((job.out_rows, width), BF16))
            out_specs.append(pl.BlockSpec((r, width), lambda c, lo=last_out: (jnp.minimum(c, lo), 0)))
    return pl.pallas_call(
        functools.partial(_cast_kernel, jobs=tuple(sized)),
        out_shape=out_shape,
        grid=(CAST_STEPS,),
        in_specs=in_specs,
        out_specs=out_specs,
        compiler_params=_params("arbitrary"),
        name="castw",
    )(*[job.src for job in cast_jobs])


def _ffn_cast_jobs(w_in, w_out):
    ff = w_out.shape[0]
    ffp = -(-ff // FFN_HIDDEN) * FFN_HIDDEN
    return (_CastJob(w_in, ((0, ff, ffp), (ff, 2 * ff, ffp)), w_in.shape[0]),
            _CastJob(w_out, ((0, w_out.shape[1], w_out.shape[1]),), ffp))


def kernel(x, c, w_ada, b_ada, norm_ffn1, w_ffn1_in, w_ffn1_out, norm_mix, w_in, conv_w, conv_b, dt_bias, a_log, d_ssd, ssd_norm_w, w_a_proj, s5_lambda_re, s5_lambda_im, s5_b_re, s5_b_im, s5_c_re, s5_c_im, s5_d, s5_log_dt, w_b_glu, w_out, norm_ffn2, w_ffn2_in, w_ffn2_out, norm_final):
    bsz, seq, d = x.shape
    depth = w_ada.shape[0]
    m = bsz * seq
    d_inner = ssd_norm_w.shape[1]
    conv_dim = conv_w.shape[2]
    heads = dt_bias.shape[1]
    s5_width = w_b_glu.shape[1]
    off_xbc = d_inner
    off_dt = off_xbc + conv_dim
    off_u = off_dt + heads
    off_g = off_u + s5_width

    assert depth == 1, "the epilogue fusion below is written for a single layer"
    l = 0
    x2 = x.reshape(m, d)
    mods3 = _mods(c, w_ada[l], b_ada[l]).reshape(bsz, N_ADA, d)
    h = _prenorm(x2, norm_ffn1[l], mods3, 0, seq)

    whole = lambda w: ((0, w.shape[1], w.shape[1]),)
    jobs = (*_ffn_cast_jobs(w_ffn2_in[l], w_ffn2_out[l]),
            _CastJob(w_a_proj[l], whole(w_a_proj[l]), d_inner),
            _CastJob(w_out[l], whole(w_out[l]), d),
            _CastJob(w_b_glu[l], whole(w_b_glu[l]), s5_width),
            _CastJob(w_in[l].T, ((0, d, d),), w_in.shape[2]))
    x2, h, wa2, wb2, wo2, w_a, w_o, w_glu, wi = _ffn(
        h, x2, *_cast_weights(_ffn_cast_jobs(w_ffn1_in[l], w_ffn1_out[l])), mods3, norm_mix[l],
        2, 3, seq, jobs)

    off_bc = off_xbc + d_inner
    zs = _proj(h, wi, 0, d_inner, 1024, act=True)
    xs = _proj_conv(h, wi, off_xbc, conv_w[l][:, :d_inner], conv_b[l][:d_inner], 1024, seq, F32)
    bc = _proj_conv(h, wi, off_bc, conv_w[l][:, d_inner:], conv_b[l][d_inner:], 1024, seq, BF16)
    dtp = _proj(h, wi, off_dt, LANES, LANES)
    w_rest = wi[off_u:]
    u = _proj_chunked(h, w_rest, s5_width)
    ga = _proj(h, w_rest, s5_width, d, 1024)
    gb = _proj(h, w_rest, s5_width + d, d, 1024)

    ya = _ssd(xs, bc, dtp, zs, dt_bias[l], a_log[l], d_ssd[l], ssd_norm_w[l], bsz, seq)
    prep = _s5_prep(s5_lambda_re[l], s5_lambda_im[l], s5_log_dt[l],
                    s5_b_re[l], s5_b_im[l], s5_c_re[l], s5_c_im[l])
    yb = _s5(u, prep, s5_d[l], bsz, seq)

    x2, h = _merge(ya, yb, ga, gb, x2, w_a, w_glu.reshape(s5_width // S5_SLAB, S5_SLAB, 2 * d), w_o,
                   mods3, norm_ffn2[l], 5, 6, seq)

    (x2,) = _ffn(h, x2, wa2, wb2, wo2, mods3, norm_final, 8, None, seq)
    return x2.reshape(bsz, seq, d)
```

```python
import functools
import math
from typing import NamedTuple

import jax
import jax.numpy as jnp
from jax import lax
from jax.experimental import pallas as pl
from jax.experimental.pallas import tpu as pltpu

F32 = jnp.float32
BF16 = jnp.bfloat16
EPS = 1e-6

VMEM_LIMIT_BYTES = 56 * 1024 * 1024
LANES = 128
SUBLANES = 8
BF16_ROWS = 16

SSD_HEAD_DIM = 64
SSD_GROUPS = 4
SSD_STATE = 128
SSD_CONV = 4
SSD_CHUNK = 128
S5_GROUP_SIZE = 16
S5_STATE = 64
MXU_WIDTH = 256
S5_SLAB = 128
S5_GROUPS_PER_SLAB = S5_SLAB // S5_GROUP_SIZE
S5_SLAB_STATE = S5_GROUPS_PER_SLAB * S5_STATE
S5_CHUNK = 16
N_ADA = 9


def _params(*sem):
    return pltpu.CompilerParams(dimension_semantics=sem, vmem_limit_bytes=VMEM_LIMIT_BYTES)


def _silu(v):
    return v * jax.nn.sigmoid(v)


def _dot(a, b):
    return jnp.dot(a, b, preferred_element_type=F32)


def _dot_nt(a, b):
    return lax.dot_general(a, b, (((1,), (1,)), ((), ())), preferred_element_type=F32)


def _mod_rms(x, nw, shift, scale):
    ms = jnp.mean(x * x, axis=-1, keepdims=True)
    return (x * lax.rsqrt(ms + EPS) * nw) * (1.0 + scale) + shift


def _mods_kernel(ct_ref, w_ref, b_ref, o_ref):
    ca = _silu(ct_ref[...])
    w = w_ref[...]
    rows = [jnp.sum(ca[:, b:b + 1] * w, axis=0, keepdims=True) for b in range(ca.shape[1])]
    o_ref[...] = jnp.concatenate(rows, axis=0) + b_ref[...]


def _mods(c, w_ada, b_ada):
    bsz, d = c.shape
    n = w_ada.shape[1]
    tn = 1024
    return pl.pallas_call(
        _mods_kernel,
        out_shape=jax.ShapeDtypeStruct((bsz, n), F32),
        grid=(n // tn,),
        in_specs=[pl.BlockSpec((d, bsz), lambda j: (0, 0)),
                  pl.BlockSpec((d, tn), lambda j: (0, j)),
                  pl.BlockSpec((1, tn), lambda j: (0, j))],
        out_specs=pl.BlockSpec((bsz, tn), lambda j: (0, j)),
        compiler_params=_params("arbitrary"),
        name="mods",
    )(c.T, w_ada, b_ada.reshape(1, n))


def _prenorm_kernel(x_ref, nw_ref, mods_ref, h_ref, *, k):
    h = _mod_rms(x_ref[...], nw_ref[...], mods_ref[k:k + 1, :], mods_ref[k + 1:k + 2, :])
    h_ref[...] = h.astype(h_ref.dtype)


def _prenorm(x2, nw, mods3, k, seq):
    m, d = x2.shape
    tm = min(512, seq)
    per_b = seq // tm
    return pl.pallas_call(
        functools.partial(_prenorm_kernel, k=k),
        out_shape=jax.ShapeDtypeStruct((m, d), BF16),
        grid=(m // tm,),
        in_specs=[pl.BlockSpec((tm, d), lambda i: (i, 0)),
                  pl.BlockSpec((1, d), lambda i: (0, 0)),
                  pl.BlockSpec((None, N_ADA, d), lambda i: (i // per_b, 0, 0))],
        out_specs=pl.BlockSpec((tm, d), lambda i: (i, 0)),
        compiler_params=_params("arbitrary"),
        name="prenorm",
    )(x2, nw.reshape(1, d), mods3)


FFN_ROWS = 1024
FFN_HIDDEN = 512
FFN_EPI_CHUNKS = 8


class _CastJob(NamedTuple):
    src: jax.Array
    cols: tuple
    out_rows: int

    def chunk_rows(self, slots):
        need = -(-self.out_rows // slots)
        rows = next(r for r in range(BF16_ROWS, self.out_rows + 1, BF16_ROWS)
                    if r >= need and self.out_rows % r == 0)
        return rows


def _cast_chunk(job, rows, chunk, src_ref, dst_refs):
    src_rows = job.src.shape[0]
    out_chunk = jnp.minimum(chunk, job.out_rows // rows - 1)
    partial = src_rows % rows != 0 or job.out_rows != src_rows
    if partial:
        r = lax.broadcasted_iota(jnp.int32, (rows, 1), 0) + out_chunk * rows
        keep = r < src_rows
    for (c0, c1, width), dst in zip(job.cols, dst_refs):
        v = src_ref[:, c0:c1]
        if partial:
            v = jnp.where(keep, v, 0.0)
        dst[:, :c1 - c0] = v.astype(BF16)
        if width > c1 - c0:
            dst[:, c1 - c0:] = jnp.zeros((rows, width - (c1 - c0)), BF16)


def _ffn_kernel(h_ref, x_ref, wa_ref, wb_ref, wo_ref, mods_ref, nw_ref, *refs, k_gate, k_next, n_tiles,
                jobs):
    n_job_out = sum(len(job.cols) for job, _ in jobs)
    job_src = refs[:len(jobs)]
    refs = refs[len(jobs):]
    acc = refs[-1]
    out_refs = refs[:len(refs) - 1 - n_job_out]
    job_dst = refs[len(out_refs):-1]
    i = pl.program_id(0)
    j = pl.program_id(1)
    slot = i % 2

    def casts():
        chunk = i * FFN_EPI_CHUNKS + jnp.minimum(j, FFN_EPI_CHUNKS - 1)
        k = 0
        for (job, rows), src_ref in zip(jobs, job_src):
            _cast_chunk(job, rows, chunk, src_ref, job_dst[k:k + len(job.cols)])
            k += len(job.cols)

    @pl.when((i == 0) & (j == 0))
    def _():
        acc[...] = jnp.zeros_like(acc)

    def epilogue():
        rows = x_ref.shape[0]
        r0 = pl.multiple_of(jnp.minimum(j, FFN_EPI_CHUNKS - 1) * rows, rows)
        done = acc[1 - slot, pl.ds(r0, rows), :]
        xn = x_ref[...] + 0.5 * mods_ref[k_gate:k_gate + 1, :] * done
        if k_next is None:
            ms = jnp.mean(xn * xn, axis=-1, keepdims=True)
            out_refs[0][...] = xn * lax.rsqrt(ms + EPS) * nw_ref[...]
        else:
            out_refs[0][...] = xn
            hn = _mod_rms(xn, nw_ref[...], mods_ref[k_next:k_next + 1, :],
                          mods_ref[k_next + 1:k_next + 2, :])
            out_refs[1][...] = hn.astype(BF16)

    def matmuls():
        h = h_ref[...]
        a = _dot(h, wa_ref[...])
        b = _dot(h, wb_ref[...])
        act = (_silu(a) * b).astype(BF16)
        acc[slot] = jnp.where(j == 0, 0.0, acc[slot]) + _dot(act, wo_ref[...])

    @pl.when(i == 0)
    def _():
        casts()
        matmuls()

    @pl.when((i > 0) & (i < n_tiles))
    def _():
        epilogue()
        casts()
        matmuls()

    @pl.when(i == n_tiles)
    def _():
        epilogue()


def _ffn(h, x2, wa, wb, wo, mods3, nw_next, k_gate, k_next, seq, cast_jobs=()):
    m, d = x2.shape
    ffp = wa.shape[1]
    tm = min(FFN_ROWS, seq)
    tf = FFN_HIDDEN
    n_tiles = m // tm
    n_hidden = ffp // tf
    ec = FFN_EPI_CHUNKS
    rows = tm // ec
    per_b = seq // tm
    assert n_hidden >= ec

    def done_chunk(i, j):
        return (jnp.where(i == 0, 0, (i - 1) * ec + jnp.minimum(j, ec - 1)), 0)

    def hidden(i, j):
        return jnp.where(i == n_tiles, n_hidden - 1, j)

    def cast_chunk(i, j):
        return jnp.where(i == n_tiles, n_tiles * ec - 1, i * ec + jnp.minimum(j, ec - 1))

    out_shape = [jax.ShapeDtypeStruct((m, d), F32)]
    out_specs = [pl.BlockSpec((rows, d), done_chunk)]
    if k_next is not None:
        out_shape.append(jax.ShapeDtypeStruct((m, d), BF16))
        out_specs.append(pl.BlockSpec((rows, d), done_chunk))
    in_specs = [pl.BlockSpec((tm, d), lambda i, j: (jnp.minimum(i, n_tiles - 1), 0)),
                pl.BlockSpec((rows, d), done_chunk),
                pl.BlockSpec((d, tf), lambda i, j: (0, hidden(i, j))),
                pl.BlockSpec((d, tf), lambda i, j: (0, hidden(i, j))),
                pl.BlockSpec((tf, d), lambda i, j: (hidden(i, j), 0)),
                pl.BlockSpec((None, N_ADA, d), lambda i, j: (jnp.maximum(i - 1, 0) // per_b, 0, 0)),
                pl.BlockSpec((1, d), lambda i, j: (0, 0))]
    sized_jobs = []
    for job in cast_jobs:
        r = job.chunk_rows(n_tiles * ec)
        sized_jobs.append((job._replace(src=jax.ShapeDtypeStruct(job.src.shape, F32)), r))
        last_out = job.out_rows // r - 1
        last_in = (job.src.shape[0] - 1) // r
        in_specs.append(pl.BlockSpec(
            (r, job.src.shape[1]),
            lambda i, j, lo=last_out, li=last_in: (jnp.minimum(jnp.minimum(cast_chunk(i, j), lo), li), 0)))
        for _, _, width in job.cols:
            out_shape.append(jax.ShapeDtypeStruct((job.out_rows, width), BF16))
            out_specs.append(pl.BlockSpec(
                (r, width), lambda i, j, lo=last_out: (jnp.minimum(cast_chunk(i, j), lo), 0)))
    return pl.pallas_call(
        functools.partial(_ffn_kernel, k_gate=k_gate, k_next=k_next, n_tiles=n_tiles,
                          jobs=tuple(sized_jobs)),
        out_shape=out_shape,
        grid=(n_tiles + 1, n_hidden),
        in_specs=in_specs,
        out_specs=out_specs,
        scratch_shapes=[pltpu.VMEM((2, tm, d), F32)],
        compiler_params=_params("arbitrary", "arbitrary"),
        name="ffn",
    )(h, x2, wa, wb, wo, mods3, nw_next.reshape(1, d), *[job.src for job in cast_jobs])


PROJ_ROWS = 1024
PROJ_SUB = 256


def _proj_kernel(h_ref, w_ref, o_ref, *, act):
    y = _dot_nt(h_ref[...], w_ref[...])
    if act:
        y = _silu(y)
    o_ref[...] = y.astype(o_ref.dtype)


def _proj(h, wt, col0, n, tn, act=False):
    m, d = h.shape
    tm = min(PROJ_ROWS, m)
    cb = col0 // tn
    return pl.pallas_call(
        functools.partial(_proj_kernel, act=act),
        out_shape=jax.ShapeDtypeStruct((m, n), F32),
        grid=(m // tm, n // tn),
        in_specs=[pl.BlockSpec((tm, d), lambda i, j: (i, 0)),
                  pl.BlockSpec((tn, d), lambda i, j: (cb + j, 0))],
        out_specs=pl.BlockSpec((tm, tn), lambda i, j: (i, j)),
        compiler_params=_params("arbitrary", "arbitrary"),
        name="proj",
    )(h, wt)


def _proj_conv_kernel(h_ref, w_ref, cw_ref, cb_ref, o_ref, halo, *, tiles_per_seq):
    i = pl.program_id(0)
    j = pl.program_id(1)
    tm = h_ref.shape[0]
    sub = min(PROJ_SUB, tm)
    taps = SSD_CONV - 1

    @pl.when(i % tiles_per_seq == 0)
    def _():
        halo[j] = jnp.zeros(halo.shape[1:], F32)

    prev = halo[j]
    w = w_ref[...]
    cw = cw_ref[...]
    cb = cb_ref[...]
    for r in range(tm // sub):
        raw = _dot_nt(h_ref[r * sub:(r + 1) * sub, :], w)
        ext = jnp.concatenate([prev, raw], axis=0)
        conv = cb + cw[taps:taps + 1, :] * raw
        for k in range(taps):
            lo = SUBLANES - taps + k
            conv = conv + cw[k:k + 1, :] * ext[lo:lo + sub, :]
        o_ref[r * sub:(r + 1) * sub, :] = _silu(conv).astype(o_ref.dtype)
        prev = raw[sub - SUBLANES:, :]
    halo[j] = prev


def _proj_conv(h, wt, col0, conv_w, conv_b, tn, seq, out_dtype):
    m, d = h.shape
    n = conv_w.shape[1]
    tm = min(PROJ_ROWS, seq)
    cb = col0 // tn
    return pl.pallas_call(
        functools.partial(_proj_conv_kernel, tiles_per_seq=seq // tm),
        out_shape=jax.ShapeDtypeStruct((m, n), out_dtype),
        grid=(m // tm, n // tn),
        in_specs=[pl.BlockSpec((tm, d), lambda i, j: (i, 0)),
                  pl.BlockSpec((tn, d), lambda i, j: (cb + j, 0)),
                  pl.BlockSpec((SSD_CONV, tn), lambda i, j: (0, j)),
                  pl.BlockSpec((1, tn), lambda i, j: (0, j))],
        out_specs=pl.BlockSpec((tm, tn), lambda i, j: (i, j)),
        scratch_shapes=[pltpu.VMEM((n // tn, SUBLANES, tn), F32)],
        compiler_params=_params("arbitrary", "arbitrary"),
        name="projconv",
    )(h, wt, conv_w, conv_b.reshape(1, n))


def _proj_chunked_kernel(h_ref, w_ref, o_ref, scr):
    n_slabs, rows, lq = o_ref.shape
    q = S5_SLAB
    L = lq // q
    y = _dot_nt(h_ref[...], w_ref[...])
    for c in range(scr.shape[0]):
        scr[c] = y[:, c * LANES:(c + 1) * LANES]
    for c in range(scr.shape[0]):
        s, off = divmod(c * LANES, q)
        for l in range(L):
            o_ref[s, :, l * q + off:l * q + off + LANES] = scr[c, pl.ds(l, rows, stride=L), :]


def _proj_chunked(h, wt, n):
    m, d = h.shape
    tm = min(PROJ_ROWS, m)
    L = S5_CHUNK
    n_slabs = n // S5_SLAB
    return pl.pallas_call(
        _proj_chunked_kernel,
        out_shape=jax.ShapeDtypeStruct((n_slabs, m // L, L * S5_SLAB), F32),
        grid=(m // tm,),
        in_specs=[pl.BlockSpec((tm, d), lambda i: (i, 0)),
                  pl.BlockSpec((n, d), lambda i: (0, 0))],
        out_specs=pl.BlockSpec((n_slabs, tm // L, L * S5_SLAB), lambda i: (0, i, 0)),
        scratch_shapes=[pltpu.VMEM((n // LANES, tm, LANES), F32)],
        compiler_params=_params("arbitrary"),
        name="projchunk",
    )(h, wt)


def _split3(v):
    hi = v.astype(BF16)
    r = v - hi.astype(F32)
    mid = r.astype(BF16)
    lo = (r - mid.astype(F32)).astype(BF16)
    return hi, mid, lo


def _pair(v, j, first):
    return jnp.where(first, v[:, 2 * j:2 * j + 1], v[:, 2 * j + 1:2 * j + 2])


def _ssd_kernel(xs_ref, bc_ref, dt_ref, zs_ref, dtb_ref, alog_ref, dsk_ref, nw_ref, y_ref, state):
    L = SSD_CHUNK
    d_inner = xs_ref.shape[1]
    n_state = SSD_STATE
    gw = d_inner // SSD_GROUPS
    pairs_per_group = gw // LANES

    @pl.when(pl.program_id(1) == 0)
    def _():
        state[...] = jnp.zeros_like(state)

    dtr = dt_ref[...] + dtb_ref[...]
    dt = jnp.maximum(dtr, 0.0) + jnp.log1p(jnp.exp(-jnp.abs(dtr)))
    da = dt * (-jnp.exp(alog_ref[...]))
    row = lax.broadcasted_iota(jnp.int32, (L, L), 0)
    col = lax.broadcasted_iota(jnp.int32, (L, L), 1)
    causal = row >= col
    tril = jnp.where(causal, 1.0, 0.0).astype(BF16)
    cs = sum(_dot(tril, part) for part in _split3(da))
    cs_last = cs[L - 1:L, :]
    ecs = jnp.exp(cs)
    ecl = jnp.exp(cs_last)
    cs_t = cs.T
    dt_t = dt.T
    wst_t = (dt * jnp.exp(cs_last - cs)).T

    lane = lax.broadcasted_iota(jnp.int32, (L, LANES), 1)
    first = lane < SSD_HEAD_DIM
    first_row = first[0:1, :]

    y_parts = []
    for g in range(SSD_GROUPS):
        bm_g = bc_ref[:, g * n_state:(g + 1) * n_state]
        cm_g = bc_ref[:, (SSD_GROUPS + g) * n_state:(SSD_GROUPS + g + 1) * n_state]
        cb = _dot_nt(cm_g, bm_g)
        bm_t = bm_g.astype(F32).T
        y_off = _dot(cm_g, state[g].astype(BF16))
        for jj in range(pairs_per_group):
            j = g * pairs_per_group + jj
            h0, h1 = 2 * j, 2 * j + 1
            cols = slice(j * LANES, (j + 1) * LANES)
            gcols = slice(jj * LANES, (jj + 1) * LANES)
            xs_p = xs_ref[:, cols]
            xb = xs_p.astype(BF16)
            zero = jnp.zeros_like(xb)
            rhs = jnp.concatenate([jnp.where(first, xb, zero), jnp.where(first, zero, xb)], axis=0)
            lhs_y = []
            lhs_s = []
            for h in (h0, h1):
                dec = jnp.exp(jnp.where(causal, cs[:, h:h + 1] - cs_t[h:h + 1, :], -jnp.inf))
                lhs_y.append((cb * dec * dt_t[h:h + 1, :]).astype(BF16))
                lhs_s.append((bm_t * wst_t[h:h + 1, :]).astype(BF16))
            y_p = (_dot(jnp.concatenate(lhs_y, axis=1), rhs)
                   + _pair(ecs, j, first) * y_off[:, gcols]
                   + dsk_ref[:, cols] * xs_p)
            y_parts.append(y_p)
            st_new = _dot(jnp.concatenate(lhs_s, axis=1), rhs)
            state[g, :, gcols] = _pair(ecl, j, first_row) * state[g, :, gcols] + st_new

    yz = jnp.concatenate(y_parts, axis=1) * zs_ref[...]
    outs = []
    for g in range(SSD_GROUPS):
        seg = yz[:, g * gw:(g + 1) * gw]
        ms = jnp.mean(seg * seg, axis=-1, keepdims=True)
        outs.append(seg * lax.rsqrt(ms + EPS))
    y_ref[...] = (jnp.concatenate(outs, axis=1) * nw_ref[...]).astype(y_ref.dtype)


def _ssd(xs, bc, dtp, zs, dt_bias, a_log, d_ssd, norm_w, bsz, seq):
    m, d_inner = xs.shape
    L = SSD_CHUNK
    nc = seq // L
    heads = d_inner // SSD_HEAD_DIM
    pad = LANES - heads
    row = lambda b, c: (b * nc + c, 0)
    const = lambda b, c: (0, 0)
    return pl.pallas_call(
        _ssd_kernel,
        out_shape=jax.ShapeDtypeStruct((m, d_inner), BF16),
        grid=(bsz, nc),
        in_specs=[pl.BlockSpec((L, d_inner), row),
                  pl.BlockSpec((L, bc.shape[1]), row),
                  pl.BlockSpec((L, LANES), row),
                  pl.BlockSpec((L, d_inner), row),
                  pl.BlockSpec((1, LANES), const),
                  pl.BlockSpec((1, LANES), const),
                  pl.BlockSpec((1, d_inner), const),
                  pl.BlockSpec((1, d_inner), const)],
        out_specs=pl.BlockSpec((L, d_inner), row),
        scratch_shapes=[pltpu.VMEM((SSD_GROUPS, SSD_STATE, d_inner // SSD_GROUPS), F32)],
        compiler_params=_params("arbitrary", "arbitrary"),
        name="ssd",
    )(xs, bc, dtp, zs,
      jnp.pad(dt_bias, (0, pad)).reshape(1, LANES), jnp.pad(a_log, (0, pad)).reshape(1, LANES),
      jnp.repeat(d_ssd, SSD_HEAD_DIM).reshape(1, d_inner), norm_w.reshape(1, d_inner))


def _s5_prep_kernel(lre_ref, lim_ref, ldt_ref, btr_ref, bti_ref, cnr_ref, cni_ref,
                    wend_ref, wcorr_ref, krev_ref, al_ref):
    L = S5_CHUNK
    q = S5_SLAB
    ns = S5_SLAB_STATE
    lr = jnp.minimum(lre_ref[...], -1e-4)
    li = lim_ref[...]
    dt = jnp.exp(ldt_ref[...])

    def power(k):
        mag = jnp.exp(lr * dt * float(k))
        ang = li * dt * float(k)
        return mag * jnp.cos(ang), mag * jnp.sin(ang)

    ar, ai = power(1)
    den = lr * lr + li * li
    nr = ar - 1.0
    kr = (nr * lr + ai * li) / den
    ki = (ai * lr - nr * li) / den

    rows = lax.broadcasted_iota(jnp.int32, (q, ns), 0) // S5_GROUP_SIZE
    cols = lax.broadcasted_iota(jnp.int32, (q, ns), 1) // S5_STATE
    same = rows == cols

    def block_diag(ref):
        return jnp.where(same, jnp.concatenate([ref[...]] * (ns // LANES), axis=1), 0.0)

    btr, bti = block_diag(btr_ref), block_diag(bti_ref)
    bbr = btr * kr - bti * ki
    bbi = btr * ki + bti * kr
    cr, ci = block_diag(cnr_ref), block_diag(cni_ref)

    for k in range(L):
        pr, pi = power(k)
        blk = jnp.concatenate([bbr * pr - bbi * pi, bbr * pi + bbi * pr], axis=1)
        wend_ref[(L - 1 - k) * q:(L - k) * q, :] = blk.astype(BF16)
    for l in range(L):
        pr, pi = power(l + 1)
        blk = jnp.concatenate([cr * pr - ci * pi, -(cr * pi) - ci * pr], axis=1)
        wcorr_ref[l * q:(l + 1) * q, :] = blk.astype(BF16)
    cstack = jnp.concatenate([cr, -ci], axis=1).astype(BF16)
    kw = _dot_nt(wend_ref[...], cstack)
    tb = MXU_WIDTH // q
    for b in range(tb):
        up = (tb - 1 - b) * q
        shifted = kw if up == 0 else jnp.concatenate([kw[up:], jnp.zeros((up, q), F32)], axis=0)
        krev_ref[:, b * q:(b + 1) * q] = shifted.astype(BF16)
    pr, pi = power(L)
    al_ref[...] = jnp.concatenate([pr, pi], axis=1)


def _s5_prep(lam_re, lam_im, log_dt, b_re, b_im, c_re, c_im):
    g, p = lam_re.shape
    width = g * S5_GROUP_SIZE
    n_slabs = width // S5_SLAB
    ns = S5_SLAB_STATE
    lq = S5_CHUNK * S5_SLAB
    row_vec = lambda v: v.reshape(1, g * p)
    twice = lambda v: jnp.concatenate([v, v], axis=1)
    bt = lambda v: twice(jnp.swapaxes(v, 1, 2).reshape(width, p))
    cn = lambda v: twice(v.reshape(width, p))
    vec = pl.BlockSpec((None, 1, ns), lambda s: (s, 0, 0))
    mat = pl.BlockSpec((S5_SLAB, LANES), lambda s: (s, 0))
    vec3 = lambda v: row_vec(v).reshape(n_slabs, 1, ns)
    return pl.pallas_call(
        _s5_prep_kernel,
        out_shape=[jax.ShapeDtypeStruct((n_slabs, lq, 2 * ns), BF16),
                   jax.ShapeDtypeStruct((n_slabs, lq, 2 * ns), BF16),
                   jax.ShapeDtypeStruct((n_slabs, lq, MXU_WIDTH), BF16),
                   jax.ShapeDtypeStruct((n_slabs, 1, 2 * ns), F32)],
        grid=(n_slabs,),
        in_specs=[vec, vec, vec, mat, mat, mat, mat],
        out_specs=[pl.BlockSpec((None, lq, 2 * ns), lambda s: (s, 0, 0)),
                   pl.BlockSpec((None, lq, 2 * ns), lambda s: (s, 0, 0)),
                   pl.BlockSpec((None, lq, MXU_WIDTH), lambda s: (s, 0, 0)),
                   pl.BlockSpec((None, 1, 2 * ns), lambda s: (s, 0, 0))],
        compiler_params=_params("arbitrary"),
        name="s5prep",
    )(vec3(lam_re), vec3(lam_im), vec3(jnp.repeat(log_dt, p)),
      bt(b_re), bt(b_im), cn(c_re), cn(c_im))


def _gelu_tanh(v):
    return 0.5 * v * (1.0 + jnp.tanh(math.sqrt(2.0 / math.pi) * (v + 0.044715 * (v * v * v))))


def _s5_kernel(u_ref, wend_ref, wcorr_ref, krev_ref, al_ref, d_ref, y_ref, e_scr, cin_scr, st_scr,
               tok_scr, *, blocks_per_seq):
    L = S5_CHUNK
    q = S5_SLAB
    ns = S5_SLAB_STATE
    rows = u_ref.shape[0]

    @pl.when(pl.program_id(1) % blocks_per_seq == 0)
    def _():
        st_scr[...] = jnp.zeros_like(st_scr)

    u = u_ref[...]
    ub = u.astype(BF16)
    e_scr[...] = _dot(ub, wend_ref[...])
    alr = al_ref[:, :ns]
    ali = al_ref[:, ns:]

    def tile_step(t, st):
        base = pl.multiple_of(t * SUBLANES, SUBLANES)
        e = e_scr[pl.ds(base, SUBLANES), :]
        sr, si = st
        carried = []
        for r in range(SUBLANES):
            carried.append(jnp.concatenate([sr, si], axis=1))
            er = e[r:r + 1, :ns]
            ei = e[r:r + 1, ns:]
            sr, si = alr * sr - ali * si + er, alr * si + ali * sr + ei
        cin_scr[pl.ds(base, SUBLANES), :] = jnp.concatenate(carried, axis=0)
        return sr, si

    sr, si = lax.fori_loop(0, rows // SUBLANES, tile_step, (st_scr[:, :ns], st_scr[:, ns:]))
    st_scr[...] = jnp.concatenate([sr, si], axis=1)

    corr = _dot_nt(cin_scr[...].astype(BF16), wcorr_ref[...])
    w = MXU_WIDTH
    nb = L * q // w
    intra = [_dot(ub[:, :(b + 1) * w], krev_ref[(nb - 1 - b) * w:, :]) for b in range(nb)]
    y = _gelu_tanh(jnp.concatenate(intra, axis=1) + corr + d_ref[...] * u)
    for c in range(tok_scr.shape[0]):
        for l in range(L):
            lo = l * q + c * LANES
            tok_scr[c, pl.ds(l, rows, stride=L), :] = y[:, lo:lo + LANES]
        y_ref[:, c * LANES:(c + 1) * LANES] = tok_scr[c].astype(y_ref.dtype)


def _s5(u2, prep, d_skip, bsz, seq):
    wend, wcorr, krev, al = prep
    n_slabs, chunks, lq = u2.shape
    L = S5_CHUNK
    q = S5_SLAB
    ns2 = 2 * S5_SLAB_STATE
    chunks_per_seq = seq // L
    rb = min(256, chunks_per_seq)
    blocks_per_seq = chunks_per_seq // rb
    per_w = MXU_WIDTH // q
    d_t = jnp.tile(d_skip.reshape(n_slabs, 1, q), (1, 1, L))
    slab = lambda s, r: (s, 0, 0)
    blk = lambda s, r: (s, r, 0)
    return pl.pallas_call(
        functools.partial(_s5_kernel, blocks_per_seq=blocks_per_seq),
        out_shape=jax.ShapeDtypeStruct((n_slabs // per_w, chunks * L, MXU_WIDTH), BF16),
        grid=(n_slabs, chunks // rb),
        in_specs=[pl.BlockSpec((None, rb, lq), blk),
                  pl.BlockSpec((None, lq, ns2), slab),
                  pl.BlockSpec((None, lq, ns2), slab),
                  pl.BlockSpec((None, lq, MXU_WIDTH), slab),
                  pl.BlockSpec((None, 1, ns2), slab),
                  pl.BlockSpec((None, 1, lq), slab)],
        out_specs=pl.BlockSpec((None, rb * L, q), lambda s, r: (s // per_w, r, s % per_w)),
        scratch_shapes=[pltpu.VMEM((rb, ns2), F32), pltpu.VMEM((rb, ns2), F32),
                        pltpu.VMEM((1, ns2), F32), pltpu.VMEM((q // LANES, rb * L, LANES), F32)],
        compiler_params=_params("arbitrary", "arbitrary"),
        name="s5",
    )(u2, wend, wcorr, krev, al, d_t)


def _merge_kernel(ya_ref, yb_ref, ga_ref, gb_ref, x_ref, wa_ref, wg_ref, wo_ref, mods_ref, nw_ref,
                  xo_ref, h_ref, *, k_gate, k_next):
    d = x_ref.shape[1]
    p_a = _dot(ya_ref[...], wa_ref[...])
    glu = sum(_dot(yb_ref[s], wg_ref[s]) for s in range(yb_ref.shape[0]))
    p_b = glu[:, :d] * jax.nn.sigmoid(glu[:, d:])
    merged = jax.nn.sigmoid(ga_ref[...]) * p_a + jax.nn.sigmoid(gb_ref[...]) * p_b
    xn = x_ref[...] + mods_ref[k_gate:k_gate + 1, :] * _dot(merged.astype(BF16), wo_ref[...])
    xo_ref[...] = xn
    hn = _mod_rms(xn, nw_ref[...], mods_ref[k_next:k_next + 1, :], mods_ref[k_next + 1:k_next + 2, :])
    h_ref[...] = hn.astype(BF16)


def _merge(ya, yb_slabs, ga, gb, x2, w_a, w_glu, w_o, mods3, nw_next, k_gate, k_next, seq):
    m, d = x2.shape
    n_slabs, _, q = yb_slabs.shape
    tm = min(256, seq)
    per_b = seq // tm
    row = lambda i: (i, 0)
    const = lambda i: (0, 0)
    once = pl.Buffered(1)
    return pl.pallas_call(
        functools.partial(_merge_kernel, k_gate=k_gate, k_next=k_next),
        out_shape=[jax.ShapeDtypeStruct((m, d), F32), jax.ShapeDtypeStruct((m, d), BF16)],
        grid=(m // tm,),
        in_specs=[pl.BlockSpec((tm, ya.shape[1]), row),
                  pl.BlockSpec((n_slabs, tm, q), lambda i: (0, i, 0)),
                  pl.BlockSpec((tm, d), row),
                  pl.BlockSpec((tm, d), row),
                  pl.BlockSpec((tm, d), row),
                  pl.BlockSpec(w_a.shape, const, pipeline_mode=once),
                  pl.BlockSpec(w_glu.shape, lambda i: (0, 0, 0), pipeline_mode=once),
                  pl.BlockSpec(w_o.shape, const, pipeline_mode=once),
                  pl.BlockSpec((None, N_ADA, d), lambda i: (i // per_b, 0, 0)),
                  pl.BlockSpec((1, d), const)],
        out_specs=[pl.BlockSpec((tm, d), row), pl.BlockSpec((tm, d), row)],
        compiler_params=_params("arbitrary"),
        name="merge",
    )(ya, yb_slabs, ga, gb, x2, w_a, w_glu, w_o, mods3, nw_next.reshape(1, d))


CAST_STEPS = 16


def _cast_kernel(*refs, jobs):
    chunk = pl.program_id(0)
    src = refs[:len(jobs)]
    dst = refs[len(jobs):]
    k = 0
    for (job, rows), src_ref in zip(jobs, src):
        _cast_chunk(job, rows, chunk, src_ref, dst[k:k + len(job.cols)])
        k += len(job.cols)


def _cast_weights(cast_jobs):
    in_specs, out_specs, out_shape, sized = [], [], [], []
    for job in cast_jobs:
        r = job.chunk_rows(CAST_STEPS)
        sized.append((job._replace(src=jax.ShapeDtypeStruct(job.src.shape, F32)), r))
        last_out = job.out_rows // r - 1
        last_in = (job.src.shape[0] - 1) // r
        in_specs.append(pl.BlockSpec(
            (r, job.src.shape[1]), lambda c, lo=last_out, li=last_in: (jnp.minimum(jnp.minimum(c, lo), li), 0)))
        for _, _, width in job.cols:
            out_shape.append(jax.ShapeDtypeStruct((job.out_rows, width), BF16))
            out_specs.append(pl.BlockSpec((r, width), lambda c, lo=last_out: (jnp.minimum(c, lo), 0)))
    return pl.pallas_call(
        functools.partial(_cast_kernel, jobs=tuple(sized)),
        out_shape=out_shape,
        grid=(CAST_STEPS,),
        in_specs=in_specs,
        out_specs=out_specs,
        compiler_params=_params("arbitrary"),
        name="castw",
    )(*[job.src for job in cast_jobs])


def _ffn_cast_jobs(w_in, w_out):
    ff = w_out.shape[0]
    ffp = -(-ff // FFN_HIDDEN) * FFN_HIDDEN
    return (_CastJob(w_in, ((0, ff, ffp), (ff, 2 * ff, ffp)), w_in.shape[0]),
            _CastJob(w_out, ((0, w_out.shape[1], w_out.shape[1]),), ffp))


def kernel(x, c, w_ada, b_ada, norm_ffn1, w_ffn1_in, w_ffn1_out, norm_mix, w_in, conv_w, conv_b, dt_bias, a_log, d_ssd, ssd_norm_w, w_a_proj, s5_lambda_re, s5_lambda_im, s5_b_re, s5_b_im, s5_c_re, s5_c_im, s5_d, s5_log_dt, w_b_glu, w_out, norm_ffn2, w_ffn2_in, w_ffn2_out, norm_final):
    bsz, seq, d = x.shape
    depth = w_ada.shape[0]
    m = bsz * seq
    d_inner = ssd_norm_w.shape[1]
    conv_dim = conv_w.shape[2]
    heads = dt_bias.shape[1]
    s5_width = w_b_glu.shape[1]
    off_xbc = d_inner
    off_dt = off_xbc + conv_dim
    off_u = off_dt + heads
    off_g = off_u + s5_width

    assert depth == 1, "the epilogue fusion below is written for a single layer"
    l = 0
    x2 = x.reshape(m, d)
    mods3 = _mods(c, w_ada[l], b_ada[l]).reshape(bsz, N_ADA, d)
    h = _prenorm(x2, norm_ffn1[l], mods3, 0, seq)

    whole = lambda w: ((0, w.shape[1], w.shape[1]),)
    jobs = (*_ffn_cast_jobs(w_ffn2_in[l], w_ffn2_out[l]),
            _CastJob(w_a_proj[l], whole(w_a_proj[l]), d_inner),
            _CastJob(w_out[l], whole(w_out[l]), d),
            _CastJob(w_b_glu[l], whole(w_b_glu[l]), s5_width),
            _CastJob(w_in[l].T, ((0, d, d),), w_in.shape[2]))
    x2, h, wa2, wb2, wo2, w_a, w_o, w_glu, wi = _ffn(
        h, x2, *_cast_weights(_ffn_cast_jobs(w_ffn1_in[l], w_ffn1_out[l])), mods3, norm_mix[l],
        2, 3, seq, jobs)

    off_bc = off_xbc + d_inner
    zs = _proj(h, wi, 0, d_inner, 1024, act=True)
    xs = _proj_conv(h, wi, off_xbc, conv_w[l][:, :d_inner], conv_b[l][:d_inner], 1024, seq, F32)
    bc = _proj_conv(h, wi, off_bc, conv_w[l][:, d_inner:], conv_b[l][d_inner:], 1024, seq, BF16)
    dtp = _proj(h, wi, off_dt, LANES, LANES)
    w_rest = wi[off_u:]
    u = _proj_chunked(h, w_rest, s5_width)
    ga = _proj(h, w_rest, s5_width, d, 1024)
    gb = _proj(h, w_rest, s5_width + d, d, 1024)

    ya = _ssd(xs, bc, dtp, zs, dt_bias[l], a_log[l], d_ssd[l], ssd_norm_w[l], bsz, seq)
    prep = _s5_prep(s5_lambda_re[l], s5_lambda_im[l], s5_log_dt[l],
                    s5_b_re[l], s5_b_im[l], s5_c_re[l], s5_c_im[l])
    yb = _s5(u, prep, s5_d[l], bsz, seq)

    x2, h = _merge(ya, yb, ga, gb, x2, w_a, w_glu.reshape(yb.shape[0], yb.shape[2], 2 * d), w_o,
                   mods3, norm_ffn2[l], 5, 6, seq)

    (x2,) = _ffn(h, x2, wa2, wb2, wo2, mods3, norm_final, 8, None, seq)
    return x2.reshape(bsz, seq, d)
```

```python
import functools
import math
from typing import NamedTuple

import jax
import jax.numpy as jnp
from jax import lax
from jax.experimental import pallas as pl
from jax.experimental.pallas import tpu as pltpu

F32 = jnp.float32
BF16 = jnp.bfloat16
EPS = 1e-6

VMEM_LIMIT_BYTES = 56 * 1024 * 1024
LANES = 128
SUBLANES = 8
BF16_ROWS = 16

SSD_HEAD_DIM = 64
SSD_GROUPS = 4
SSD_STATE = 128
SSD_CONV = 4
SSD_CHUNK = 128
S5_GROUP_SIZE = 16
S5_STATE = 64
MXU_WIDTH = 256
S5_SLAB = 128
S5_GROUPS_PER_SLAB = S5_SLAB // S5_GROUP_SIZE
S5_SLAB_STATE = S5_GROUPS_PER_SLAB * S5_STATE
S5_CHUNK = 16
N_ADA = 9


def _params(*sem):
    return pltpu.CompilerParams(dimension_semantics=sem, vmem_limit_bytes=VMEM_LIMIT_BYTES)


def _sigmoid(v):
    return 0.5 * jnp.tanh(0.5 * v) + 0.5


def _silu(v):
    return v * _sigmoid(v)


def _dot(a, b):
    return jnp.dot(a, b, preferred_element_type=F32)


def _dot_nt(a, b):
    return lax.dot_general(a, b, (((1,), (1,)), ((), ())), preferred_element_type=F32)


def _mod_rms(x, nw, shift, scale):
    ms = jnp.mean(x * x, axis=-1, keepdims=True)
    return (x * lax.rsqrt(ms + EPS) * nw) * (1.0 + scale) + shift


def _mods_kernel(ct_ref, w_ref, b_ref, o_ref):
    ca = _silu(ct_ref[...])
    w = w_ref[...]
    rows = [jnp.sum(ca[:, b:b + 1] * w, axis=0, keepdims=True) for b in range(ca.shape[1])]
    o_ref[...] = jnp.concatenate(rows, axis=0) + b_ref[...]


def _mods(c, w_ada, b_ada):
    bsz, d = c.shape
    n = w_ada.shape[1]
    tn = 1024
    return pl.pallas_call(
        _mods_kernel,
        out_shape=jax.ShapeDtypeStruct((bsz, n), F32),
        grid=(n // tn,),
        in_specs=[pl.BlockSpec((d, bsz), lambda j: (0, 0)),
                  pl.BlockSpec((d, tn), lambda j: (0, j)),
                  pl.BlockSpec((1, tn), lambda j: (0, j))],
        out_specs=pl.BlockSpec((bsz, tn), lambda j: (0, j)),
        compiler_params=_params("arbitrary"),
        name="mods",
    )(c.T, w_ada, b_ada.reshape(1, n))


def _prenorm_kernel(x_ref, nw_ref, mods_ref, h_ref, *, k):
    h = _mod_rms(x_ref[...], nw_ref[...], mods_ref[k:k + 1, :], mods_ref[k + 1:k + 2, :])
    h_ref[...] = h.astype(h_ref.dtype)


def _prenorm(x2, nw, mods3, k, seq):
    m, d = x2.shape
    tm = min(512, seq)
    per_b = seq // tm
    return pl.pallas_call(
        functools.partial(_prenorm_kernel, k=k),
        out_shape=jax.ShapeDtypeStruct((m, d), BF16),
        grid=(m // tm,),
        in_specs=[pl.BlockSpec((tm, d), lambda i: (i, 0)),
                  pl.BlockSpec((1, d), lambda i: (0, 0)),
                  pl.BlockSpec((None, N_ADA, d), lambda i: (i // per_b, 0, 0))],
        out_specs=pl.BlockSpec((tm, d), lambda i: (i, 0)),
        compiler_params=_params("arbitrary"),
        name="prenorm",
    )(x2, nw.reshape(1, d), mods3)


FFN_ROWS = 1024
FFN_HIDDEN = 512
FFN_EPI_CHUNKS = 8
FFN_SUB_BLOCKS = 2


class _CastJob(NamedTuple):
    src: jax.Array
    cols: tuple
    out_rows: int

    def chunk_rows(self, slots):
        need = -(-self.out_rows // slots)
        rows = next(r for r in range(BF16_ROWS, self.out_rows + 1, BF16_ROWS)
                    if r >= need and self.out_rows % r == 0)
        return rows


def _cast_chunk(job, rows, chunk, src_ref, dst_refs):
    src_rows = job.src.shape[0]
    out_chunk = jnp.minimum(chunk, job.out_rows // rows - 1)
    partial = src_rows % rows != 0 or job.out_rows != src_rows
    if partial:
        r = lax.broadcasted_iota(jnp.int32, (rows, 1), 0) + out_chunk * rows
        keep = r < src_rows
    for (c0, c1, width), dst in zip(job.cols, dst_refs):
        v = src_ref[:, c0:c1]
        if partial:
            v = jnp.where(keep, v, 0.0)
        dst[:, :c1 - c0] = v.astype(BF16)
        if width > c1 - c0:
            dst[:, c1 - c0:] = jnp.zeros((rows, width - (c1 - c0)), BF16)


def _ffn_kernel(h_ref, x_ref, wa_ref, wb_ref, wo_ref, mods_ref, nw_ref, *refs, k_gate, k_next, n_tiles,
                jobs):
    n_job_out = sum(len(job.cols) for job, _ in jobs)
    job_src = refs[:len(jobs)]
    refs = refs[len(jobs):]
    acc = refs[-1]
    out_refs = refs[:len(refs) - 1 - n_job_out]
    job_dst = refs[len(out_refs):-1]
    i = pl.program_id(0)
    j = pl.program_id(1)
    slot = i % 2

    def casts():
        chunk = i * FFN_EPI_CHUNKS + jnp.minimum(j, FFN_EPI_CHUNKS - 1)
        k = 0
        for (job, rows), src_ref in zip(jobs, job_src):
            _cast_chunk(job, rows, chunk, src_ref, job_dst[k:k + len(job.cols)])
            k += len(job.cols)

    @pl.when((i == 0) & (j == 0))
    def _():
        acc[...] = jnp.zeros_like(acc)

    def epilogue():
        rows = x_ref.shape[0]
        r0 = pl.multiple_of(jnp.minimum(j, FFN_EPI_CHUNKS - 1) * rows, rows)
        done = acc[1 - slot, pl.ds(r0, rows), :]
        xn = x_ref[...] + 0.5 * mods_ref[k_gate:k_gate + 1, :] * done
        if k_next is None:
            ms = jnp.mean(xn * xn, axis=-1, keepdims=True)
            out_refs[0][...] = xn * lax.rsqrt(ms + EPS) * nw_ref[...]
        else:
            out_refs[0][...] = xn
            hn = _mod_rms(xn, nw_ref[...], mods_ref[k_next:k_next + 1, :],
                          mods_ref[k_next + 1:k_next + 2, :])
            out_refs[1][...] = hn.astype(BF16)

    def matmuls():
        sub = h_ref.shape[0] // FFN_SUB_BLOCKS
        for r in range(FFN_SUB_BLOCKS):
            rows = pl.ds(r * sub, sub)
            h = h_ref[rows, :]
            a = _dot(h, wa_ref[...])
            b = _dot(h, wb_ref[...])
            act = (_silu(a) * b).astype(BF16)
            acc[slot, rows, :] = jnp.where(j == 0, 0.0, acc[slot, rows, :]) + _dot(act, wo_ref[...])

    @pl.when(i == 0)
    def _():
        casts()
        matmuls()

    @pl.when((i > 0) & (i < n_tiles))
    def _():
        epilogue()
        casts()
        matmuls()

    @pl.when(i == n_tiles)
    def _():
        epilogue()


def _ffn(h, x2, wa, wb, wo, mods3, nw_next, k_gate, k_next, seq, cast_jobs=()):
    m, d = x2.shape
    ffp = wa.shape[1]
    tm = min(FFN_ROWS, seq)
    tf = FFN_HIDDEN
    n_tiles = m // tm
    n_hidden = ffp // tf
    ec = FFN_EPI_CHUNKS
    rows = tm // ec
    per_b = seq // tm
    assert n_hidden >= ec

    def done_chunk(i, j):
        return (jnp.where(i == 0, 0, (i - 1) * ec + jnp.minimum(j, ec - 1)), 0)

    def hidden(i, j):
        return jnp.where(i == n_tiles, n_hidden - 1, j)

    def cast_chunk(i, j):
        return jnp.where(i == n_tiles, n_tiles * ec - 1, i * ec + jnp.minimum(j, ec - 1))

    out_shape = [jax.ShapeDtypeStruct((m, d), F32)]
    out_specs = [pl.BlockSpec((rows, d), done_chunk)]
    if k_next is not None:
        out_shape.append(jax.ShapeDtypeStruct((m, d), BF16))
        out_specs.append(pl.BlockSpec((rows, d), done_chunk))
    in_specs = [pl.BlockSpec((tm, d), lambda i, j: (jnp.minimum(i, n_tiles - 1), 0)),
                pl.BlockSpec((rows, d), done_chunk),
                pl.BlockSpec((d, tf), lambda i, j: (0, hidden(i, j))),
                pl.BlockSpec((d, tf), lambda i, j: (0, hidden(i, j))),
                pl.BlockSpec((tf, d), lambda i, j: (hidden(i, j), 0)),
                pl.BlockSpec((None, N_ADA, d), lambda i, j: (jnp.maximum(i - 1, 0) // per_b, 0, 0)),
                pl.BlockSpec((1, d), lambda i, j: (0, 0))]
    sized_jobs = []
    for job in cast_jobs:
        r = job.chunk_rows(n_tiles * ec)
        sized_jobs.append((job._replace(src=jax.ShapeDtypeStruct(job.src.shape, F32)), r))
        last_out = job.out_rows // r - 1
        last_in = (job.src.shape[0] - 1) // r
        in_specs.append(pl.BlockSpec(
            (r, job.src.shape[1]),
            lambda i, j, lo=last_out, li=last_in: (jnp.minimum(jnp.minimum(cast_chunk(i, j), lo), li), 0)))
        for _, _, width in job.cols:
            out_shape.append(jax.ShapeDtypeStruct((job.out_rows, width), BF16))
            out_specs.append(pl.BlockSpec(
                (r, width), lambda i, j, lo=last_out: (jnp.minimum(cast_chunk(i, j), lo), 0)))
    return pl.pallas_call(
        functools.partial(_ffn_kernel, k_gate=k_gate, k_next=k_next, n_tiles=n_tiles,
                          jobs=tuple(sized_jobs)),
        out_shape=out_shape,
        grid=(n_tiles + 1, n_hidden),
        in_specs=in_specs,
        out_specs=out_specs,
        scratch_shapes=[pltpu.VMEM((2, tm, d), F32)],
        compiler_params=_params("arbitrary", "arbitrary"),
        name="ffn",
    )(h, x2, wa, wb, wo, mods3, nw_next.reshape(1, d), *[job.src for job in cast_jobs])


PROJ_ROWS = 1024
PROJ_SUB = 256


def _proj_kernel(h_ref, w_ref, o_ref, *, act):
    y = _dot_nt(h_ref[...], w_ref[...])
    if act is not None:
        y = act(y)
    o_ref[...] = y.astype(o_ref.dtype)


def _proj(h, wt, col0, n, tn, act=None, out_dtype=F32):
    m, d = h.shape
    tm = min(PROJ_ROWS, m)
    cb = col0 // tn
    return pl.pallas_call(
        functools.partial(_proj_kernel, act=act),
        out_shape=jax.ShapeDtypeStruct((m, n), out_dtype),
        grid=(m // tm, n // tn),
        in_specs=[pl.BlockSpec((tm, d), lambda i, j: (i, 0)),
                  pl.BlockSpec((tn, d), lambda i, j: (cb + j, 0))],
        out_specs=pl.BlockSpec((tm, tn), lambda i, j: (i, j)),
        compiler_params=_params("arbitrary", "arbitrary"),
        name="proj",
    )(h, wt)


def _proj_conv_kernel(h_ref, w_ref, cw_ref, cb_ref, o_ref, halo, *, tiles_per_seq):
    i = pl.program_id(0)
    j = pl.program_id(1)
    tm = h_ref.shape[0]
    sub = min(PROJ_SUB, tm)
    taps = SSD_CONV - 1

    @pl.when(i % tiles_per_seq == 0)
    def _():
        halo[j] = jnp.zeros(halo.shape[1:], F32)

    prev = halo[j]
    w = w_ref[...]
    cw = cw_ref[...]
    cb = cb_ref[...]
    for r in range(tm // sub):
        raw = _dot_nt(h_ref[r * sub:(r + 1) * sub, :], w)
        ext = jnp.concatenate([prev, raw], axis=0)
        conv = cb + cw[taps:taps + 1, :] * raw
        for k in range(taps):
            lo = SUBLANES - taps + k
            conv = conv + cw[k:k + 1, :] * ext[lo:lo + sub, :]
        o_ref[r * sub:(r + 1) * sub, :] = _silu(conv).astype(o_ref.dtype)
        prev = raw[sub - SUBLANES:, :]
    halo[j] = prev


def _proj_conv(h, wt, col0, conv_w, conv_b, tn, seq, out_dtype):
    m, d = h.shape
    n = conv_w.shape[1]
    tm = min(PROJ_ROWS, seq)
    cb = col0 // tn
    return pl.pallas_call(
        functools.partial(_proj_conv_kernel, tiles_per_seq=seq // tm),
        out_shape=jax.ShapeDtypeStruct((m, n), out_dtype),
        grid=(m // tm, n // tn),
        in_specs=[pl.BlockSpec((tm, d), lambda i, j: (i, 0)),
                  pl.BlockSpec((tn, d), lambda i, j: (cb + j, 0)),
                  pl.BlockSpec((SSD_CONV, tn), lambda i, j: (0, j)),
                  pl.BlockSpec((1, tn), lambda i, j: (0, j))],
        out_specs=pl.BlockSpec((tm, tn), lambda i, j: (i, j)),
        scratch_shapes=[pltpu.VMEM((n // tn, SUBLANES, tn), F32)],
        compiler_params=_params("arbitrary", "arbitrary"),
        name="projconv",
    )(h, wt, conv_w, conv_b.reshape(1, n))


def _proj_chunked_kernel(h_ref, w_ref, o_ref, scr):
    n_slabs, rows, lq = o_ref.shape
    q = S5_SLAB
    L = lq // q
    y = _dot_nt(h_ref[...], w_ref[...])
    for c in range(scr.shape[0]):
        scr[c] = y[:, c * LANES:(c + 1) * LANES]
    for c in range(scr.shape[0]):
        s, off = divmod(c * LANES, q)
        for l in range(L):
            o_ref[s, :, l * q + off:l * q + off + LANES] = scr[c, pl.ds(l, rows, stride=L), :]


def _proj_chunked(h, wt, n):
    m, d = h.shape
    tm = min(PROJ_ROWS, m)
    L = S5_CHUNK
    n_slabs = n // S5_SLAB
    return pl.pallas_call(
        _proj_chunked_kernel,
        out_shape=jax.ShapeDtypeStruct((n_slabs, m // L, L * S5_SLAB), F32),
        grid=(m // tm,),
        in_specs=[pl.BlockSpec((tm, d), lambda i: (i, 0)),
                  pl.BlockSpec((n, d), lambda i: (0, 0))],
        out_specs=pl.BlockSpec((n_slabs, tm // L, L * S5_SLAB), lambda i: (0, i, 0)),
        scratch_shapes=[pltpu.VMEM((n // LANES, tm, LANES), F32)],
        compiler_params=_params("arbitrary"),
        name="projchunk",
    )(h, wt)


def _split3(v):
    hi = v.astype(BF16)
    r = v - hi.astype(F32)
    mid = r.astype(BF16)
    lo = (r - mid.astype(F32)).astype(BF16)
    return hi, mid, lo


def _pair(v, j, first):
    return jnp.where(first, v[:, 2 * j:2 * j + 1], v[:, 2 * j + 1:2 * j + 2])


def _ssd_kernel(xs_ref, bc_ref, dt_ref, zs_ref, dtb_ref, alog_ref, dsk_ref, nw_ref, y_ref, state):
    L = SSD_CHUNK
    d_inner = xs_ref.shape[1]
    n_state = SSD_STATE
    gw = d_inner // SSD_GROUPS
    pairs_per_group = gw // LANES

    @pl.when(pl.program_id(1) == 0)
    def _():
        state[...] = jnp.zeros_like(state)

    dtr = dt_ref[...] + dtb_ref[...]
    dt = jnp.maximum(dtr, 0.0) + jnp.log1p(jnp.exp(-jnp.abs(dtr)))
    da = dt * (-jnp.exp(alog_ref[...]))
    row = lax.broadcasted_iota(jnp.int32, (L, L), 0)
    col = lax.broadcasted_iota(jnp.int32, (L, L), 1)
    causal = row >= col
    tril = jnp.where(causal, 1.0, 0.0).astype(BF16)
    cs = sum(_dot(tril, part) for part in _split3(da))
    cs_last = cs[L - 1:L, :]
    ecs = jnp.exp(cs)
    ecl = jnp.exp(cs_last)
    cs_t = cs.T
    dt_t = dt.T
    wst_t = (dt * jnp.exp(cs_last - cs)).T

    lane = lax.broadcasted_iota(jnp.int32, (L, LANES), 1)
    first = lane < SSD_HEAD_DIM
    first_row = first[0:1, :]

    y_parts = []
    for g in range(SSD_GROUPS):
        bm_g = bc_ref[:, g * n_state:(g + 1) * n_state]
        cm_g = bc_ref[:, (SSD_GROUPS + g) * n_state:(SSD_GROUPS + g + 1) * n_state]
        cb = _dot_nt(cm_g, bm_g)
        bm_t = bm_g.astype(F32).T
        y_off = _dot(cm_g, state[g].astype(BF16))
        for jj in range(pairs_per_group):
            j = g * pairs_per_group + jj
            h0, h1 = 2 * j, 2 * j + 1
            cols = slice(j * LANES, (j + 1) * LANES)
            gcols = slice(jj * LANES, (jj + 1) * LANES)
            xs_p = xs_ref[:, cols]
            xb = xs_p.astype(BF16)
            zero = jnp.zeros_like(xb)
            rhs = jnp.concatenate([jnp.where(first, xb, zero), jnp.where(first, zero, xb)], axis=0)
            lhs_y = []
            lhs_s = []
            for h in (h0, h1):
                dec = jnp.exp(jnp.where(causal, cs[:, h:h + 1] - cs_t[h:h + 1, :], -jnp.inf))
                lhs_y.append((cb * dec * dt_t[h:h + 1, :]).astype(BF16))
                lhs_s.append((bm_t * wst_t[h:h + 1, :]).astype(BF16))
            y_p = (_dot(jnp.concatenate(lhs_y, axis=1), rhs)
                   + _pair(ecs, j, first) * y_off[:, gcols]
                   + dsk_ref[:, cols] * xs_p)
            y_parts.append(y_p)
            st_new = _dot(jnp.concatenate(lhs_s, axis=1), rhs)
            state[g, :, gcols] = _pair(ecl, j, first_row) * state[g, :, gcols] + st_new

    yz = jnp.concatenate(y_parts, axis=1) * zs_ref[...]
    outs = []
    for g in range(SSD_GROUPS):
        seg = yz[:, g * gw:(g + 1) * gw]
        ms = jnp.mean(seg * seg, axis=-1, keepdims=True)
        outs.append(seg * lax.rsqrt(ms + EPS))
    y_ref[...] = (jnp.concatenate(outs, axis=1) * nw_ref[...]).astype(y_ref.dtype)


def _ssd(xs, bc, dtp, zs, dt_bias, a_log, d_ssd, norm_w, bsz, seq):
    m, d_inner = xs.shape
    L = SSD_CHUNK
    nc = seq // L
    heads = d_inner // SSD_HEAD_DIM
    pad = LANES - heads
    row = lambda b, c: (b * nc + c, 0)
    const = lambda b, c: (0, 0)
    return pl.pallas_call(
        _ssd_kernel,
        out_shape=jax.ShapeDtypeStruct((m, d_inner), BF16),
        grid=(bsz, nc),
        in_specs=[pl.BlockSpec((L, d_inner), row),
                  pl.BlockSpec((L, bc.shape[1]), row),
                  pl.BlockSpec((L, LANES), row),
                  pl.BlockSpec((L, d_inner), row),
                  pl.BlockSpec((1, LANES), const),
                  pl.BlockSpec((1, LANES), const),
                  pl.BlockSpec((1, d_inner), const),
                  pl.BlockSpec((1, d_inner), const)],
        out_specs=pl.BlockSpec((L, d_inner), row),
        scratch_shapes=[pltpu.VMEM((SSD_GROUPS, SSD_STATE, d_inner // SSD_GROUPS), F32)],
        compiler_params=_params("arbitrary", "arbitrary"),
        name="ssd",
    )(xs, bc, dtp, zs,
      jnp.pad(dt_bias, (0, pad)).reshape(1, LANES), jnp.pad(a_log, (0, pad)).reshape(1, LANES),
      jnp.repeat(d_ssd, SSD_HEAD_DIM).reshape(1, d_inner), norm_w.reshape(1, d_inner))


def _s5_prep_kernel(lre_ref, lim_ref, ldt_ref, btr_ref, bti_ref, cnr_ref, cni_ref,
                    wend_ref, wcorr_ref, krev_ref, al_ref):
    L = S5_CHUNK
    q = S5_SLAB
    ns = S5_SLAB_STATE
    lr = jnp.minimum(lre_ref[...], -1e-4)
    li = lim_ref[...]
    dt = jnp.exp(ldt_ref[...])

    def power(k):
        mag = jnp.exp(lr * dt * float(k))
        ang = li * dt * float(k)
        return mag * jnp.cos(ang), mag * jnp.sin(ang)

    ar, ai = power(1)
    den = lr * lr + li * li
    nr = ar - 1.0
    kr = (nr * lr + ai * li) / den
    ki = (ai * lr - nr * li) / den

    rows = lax.broadcasted_iota(jnp.int32, (q, ns), 0) // S5_GROUP_SIZE
    cols = lax.broadcasted_iota(jnp.int32, (q, ns), 1) // S5_STATE
    same = rows == cols

    def block_diag(ref):
        return jnp.where(same, jnp.concatenate([ref[...]] * (ns // LANES), axis=1), 0.0)

    btr, bti = block_diag(btr_ref), block_diag(bti_ref)
    bbr = btr * kr - bti * ki
    bbi = btr * ki + bti * kr
    cr, ci = block_diag(cnr_ref), block_diag(cni_ref)

    for k in range(L):
        pr, pi = power(k)
        blk = jnp.concatenate([bbr * pr - bbi * pi, bbr * pi + bbi * pr], axis=1)
        wend_ref[(L - 1 - k) * q:(L - k) * q, :] = blk.astype(BF16)
    for l in range(L):
        pr, pi = power(l + 1)
        blk = jnp.concatenate([cr * pr - ci * pi, -(cr * pi) - ci * pr], axis=1)
        wcorr_ref[l * q:(l + 1) * q, :] = blk.astype(BF16)
    cstack = jnp.concatenate([cr, -ci], axis=1).astype(BF16)
    kw = _dot_nt(wend_ref[...], cstack)
    tb = MXU_WIDTH // q
    for b in range(tb):
        up = (tb - 1 - b) * q
        shifted = kw if up == 0 else jnp.concatenate([kw[up:], jnp.zeros((up, q), F32)], axis=0)
        krev_ref[:, b * q:(b + 1) * q] = shifted.astype(BF16)
    pr, pi = power(L)
    al_ref[...] = jnp.concatenate([pr, pi], axis=1)


def _s5_prep(lam_re, lam_im, log_dt, b_re, b_im, c_re, c_im):
    g, p = lam_re.shape
    width = g * S5_GROUP_SIZE
    n_slabs = width // S5_SLAB
    ns = S5_SLAB_STATE
    lq = S5_CHUNK * S5_SLAB
    row_vec = lambda v: v.reshape(1, g * p)
    twice = lambda v: jnp.concatenate([v, v], axis=1)
    bt = lambda v: twice(jnp.swapaxes(v, 1, 2).reshape(width, p))
    cn = lambda v: twice(v.reshape(width, p))
    vec = pl.BlockSpec((None, 1, ns), lambda s: (s, 0, 0))
    mat = pl.BlockSpec((S5_SLAB, LANES), lambda s: (s, 0))
    vec3 = lambda v: row_vec(v).reshape(n_slabs, 1, ns)
    return pl.pallas_call(
        _s5_prep_kernel,
        out_shape=[jax.ShapeDtypeStruct((n_slabs, lq, 2 * ns), BF16),
                   jax.ShapeDtypeStruct((n_slabs, lq, 2 * ns), BF16),
                   jax.ShapeDtypeStruct((n_slabs, lq, MXU_WIDTH), BF16),
                   jax.ShapeDtypeStruct((n_slabs, 1, 2 * ns), F32)],
        grid=(n_slabs,),
        in_specs=[vec, vec, vec, mat, mat, mat, mat],
        out_specs=[pl.BlockSpec((None, lq, 2 * ns), lambda s: (s, 0, 0)),
                   pl.BlockSpec((None, lq, 2 * ns), lambda s: (s, 0, 0)),
                   pl.BlockSpec((None, lq, MXU_WIDTH), lambda s: (s, 0, 0)),
                   pl.BlockSpec((None, 1, 2 * ns), lambda s: (s, 0, 0))],
        compiler_params=_params("arbitrary"),
        name="s5prep",
    )(vec3(lam_re), vec3(lam_im), vec3(jnp.repeat(log_dt, p)),
      bt(b_re), bt(b_im), cn(c_re), cn(c_im))


def _gelu_tanh(v):
    return 0.5 * v * (1.0 + jnp.tanh(math.sqrt(2.0 / math.pi) * (v + 0.044715 * (v * v * v))))


def _s5_kernel(u_ref, wend_ref, wcorr_ref, krev_ref, al_ref, d_ref, y_ref, e_scr, cin_scr, st_scr,
               tok_scr, *, blocks_per_seq):
    L = S5_CHUNK
    q = S5_SLAB
    ns = S5_SLAB_STATE
    rows = u_ref.shape[0]

    @pl.when(pl.program_id(1) % blocks_per_seq == 0)
    def _():
        st_scr[...] = jnp.zeros_like(st_scr)

    u = u_ref[...]
    ub = u.astype(BF16)
    e_scr[...] = _dot(ub, wend_ref[...])
    alr = al_ref[:, :ns]
    ali = al_ref[:, ns:]

    def tile_step(t, st):
        base = pl.multiple_of(t * SUBLANES, SUBLANES)
        e = e_scr[pl.ds(base, SUBLANES), :]
        sr, si = st
        carried = []
        for r in range(SUBLANES):
            carried.append(jnp.concatenate([sr, si], axis=1))
            er = e[r:r + 1, :ns]
            ei = e[r:r + 1, ns:]
            sr, si = alr * sr - ali * si + er, alr * si + ali * sr + ei
        cin_scr[pl.ds(base, SUBLANES), :] = jnp.concatenate(carried, axis=0)
        return sr, si

    sr, si = lax.fori_loop(0, rows // SUBLANES, tile_step, (st_scr[:, :ns], st_scr[:, ns:]))
    st_scr[...] = jnp.concatenate([sr, si], axis=1)

    corr = _dot_nt(cin_scr[...].astype(BF16), wcorr_ref[...])
    w = MXU_WIDTH
    nb = L * q // w
    intra = [_dot(ub[:, :(b + 1) * w], krev_ref[(nb - 1 - b) * w:, :]) for b in range(nb)]
    y = _gelu_tanh(jnp.concatenate(intra, axis=1) + corr + d_ref[...] * u)
    for c in range(tok_scr.shape[0]):
        for l in range(L):
            lo = l * q + c * LANES
            tok_scr[c, pl.ds(l, rows, stride=L), :] = y[:, lo:lo + LANES]
        y_ref[:, c * LANES:(c + 1) * LANES] = tok_scr[c].astype(y_ref.dtype)


def _s5(u2, prep, d_skip, bsz, seq):
    wend, wcorr, krev, al = prep
    n_slabs, chunks, lq = u2.shape
    L = S5_CHUNK
    q = S5_SLAB
    ns2 = 2 * S5_SLAB_STATE
    chunks_per_seq = seq // L
    rb = min(256, chunks_per_seq)
    blocks_per_seq = chunks_per_seq // rb
    per_w = MXU_WIDTH // q
    d_t = jnp.tile(d_skip.reshape(n_slabs, 1, q), (1, 1, L))
    slab = lambda s, r: (s, 0, 0)
    blk = lambda s, r: (s, r, 0)
    return pl.pallas_call(
        functools.partial(_s5_kernel, blocks_per_seq=blocks_per_seq),
        out_shape=jax.ShapeDtypeStruct((n_slabs // per_w, chunks * L, MXU_WIDTH), BF16),
        grid=(n_slabs, chunks // rb),
        in_specs=[pl.BlockSpec((None, rb, lq), blk),
                  pl.BlockSpec((None, lq, ns2), slab),
                  pl.BlockSpec((None, lq, ns2), slab),
                  pl.BlockSpec((None, lq, MXU_WIDTH), slab),
                  pl.BlockSpec((None, 1, ns2), slab),
                  pl.BlockSpec((None, 1, lq), slab)],
        out_specs=pl.BlockSpec((None, rb * L, q), lambda s, r: (s // per_w, r, s % per_w)),
        scratch_shapes=[pltpu.VMEM((rb, ns2), F32), pltpu.VMEM((rb, ns2), F32),
                        pltpu.VMEM((1, ns2), F32), pltpu.VMEM((q // LANES, rb * L, LANES), F32)],
        compiler_params=_params("arbitrary", "arbitrary"),
        name="s5",
    )(u2, wend, wcorr, krev, al, d_t)


MERGE_COLS = 512


def _merge_kernel(ya_ref, yb_ref, ga_ref, gb_ref, x_ref, wa_ref, wg_ref, wo_ref, mods_ref, nw_ref,
                  xo_ref, h_ref, *, k_gate, k_next):
    d = x_ref.shape[1]
    ya = ya_ref[...]
    yb = [yb_ref[s] for s in range(yb_ref.shape[0])]
    proj = None
    for c0 in range(0, d, MERGE_COLS):
        c1 = c0 + MERGE_COLS
        p_a = _dot(ya, wa_ref[:, c0:c1])
        glu_a = sum(_dot(v, wg_ref[s, :, c0:c1]) for s, v in enumerate(yb))
        glu_g = sum(_dot(v, wg_ref[s, :, d + c0:d + c1]) for s, v in enumerate(yb))
        merged = ga_ref[:, c0:c1] * p_a + gb_ref[:, c0:c1] * (glu_a * _sigmoid(glu_g))
        part = _dot(merged.astype(BF16), wo_ref[c0:c1, :])
        proj = part if proj is None else proj + part
    xn = x_ref[...] + mods_ref[k_gate:k_gate + 1, :] * proj
    xo_ref[...] = xn
    hn = _mod_rms(xn, nw_ref[...], mods_ref[k_next:k_next + 1, :], mods_ref[k_next + 1:k_next + 2, :])
    h_ref[...] = hn.astype(BF16)


def _merge(ya, yb_slabs, gates, x2, w_a, w_glu, w_o, mods3, nw_next, k_gate, k_next, seq):
    m, d = x2.shape
    n_slabs, _, q = yb_slabs.shape
    tm = min(256, seq)
    per_b = seq // tm
    row = lambda i: (i, 0)
    const = lambda i: (0, 0)
    once = pl.Buffered(1)
    return pl.pallas_call(
        functools.partial(_merge_kernel, k_gate=k_gate, k_next=k_next),
        out_shape=[jax.ShapeDtypeStruct((m, d), F32), jax.ShapeDtypeStruct((m, d), BF16)],
        grid=(m // tm,),
        in_specs=[pl.BlockSpec((tm, ya.shape[1]), row),
                  pl.BlockSpec((n_slabs, tm, q), lambda i: (0, i, 0)),
                  pl.BlockSpec((tm, d), row),
                  pl.BlockSpec((tm, d), lambda i: (i, 1)),
                  pl.BlockSpec((tm, d), row),
                  pl.BlockSpec(w_a.shape, const, pipeline_mode=once),
                  pl.BlockSpec(w_glu.shape, lambda i: (0, 0, 0), pipeline_mode=once),
                  pl.BlockSpec(w_o.shape, const, pipeline_mode=once),
                  pl.BlockSpec((None, N_ADA, d), lambda i: (i // per_b, 0, 0)),
                  pl.BlockSpec((1, d), const)],
        out_specs=[pl.BlockSpec((tm, d), row), pl.BlockSpec((tm, d), row)],
        compiler_params=_params("arbitrary"),
        name="merge",
    )(ya, yb_slabs, gates, gates, x2, w_a, w_glu, w_o, mods3, nw_next.reshape(1, d))


CAST_STEPS = 16


def _cast_kernel(*refs, jobs):
    chunk = pl.program_id(0)
    src = refs[:len(jobs)]
    dst = refs[len(jobs):]
    k = 0
    for (job, rows), src_ref in zip(jobs, src):
        _cast_chunk(job, rows, chunk, src_ref, dst[k:k + len(job.cols)])
        k += len(job.cols)


def _cast_weights(cast_jobs):
    in_specs, out_specs, out_shape, sized = [], [], [], []
    for job in cast_jobs:
        r = job.chunk_rows(CAST_STEPS)
        sized.append((job._replace(src=jax.ShapeDtypeStruct(job.src.shape, F32)), r))
        last_out = job.out_rows // r - 1
        last_in = (job.src.shape[0] - 1) // r
        in_specs.append(pl.BlockSpec(
            (r, job.src.shape[1]), lambda c, lo=last_out, li=last_in: (jnp.minimum(jnp.minimum(c, lo), li), 0)))
        for _, _, width in job.cols:
            out_shape.append(jax.ShapeDtypeStruct((job.out_rows, width), BF16))
            out_specs.append(pl.BlockSpec((r, width), lambda c, lo=last_out: (jnp.minimum(c, lo), 0)))
    return pl.pallas_call(
        functools.partial(_cast_kernel, jobs=tuple(sized)),
        out_shape=out_shape,
        grid=(CAST_STEPS,),
        in_specs=in_specs,
        out_specs=out_specs,
        compiler_params=_params("arbitrary"),
        name="castw",
    )(*[job.src for job in cast_jobs])


def _ffn_cast_jobs(w_in, w_out):
    ff = w_out.shape[0]
    ffp = -(-ff // FFN_HIDDEN) * FFN_HIDDEN
    return (_CastJob(w_in, ((0, ff, ffp), (ff, 2 * ff, ffp)), w_in.shape[0]),
            _CastJob(w_out, ((0, w_out.shape[1], w_out.shape[1]),), ffp))


def kernel(x, c, w_ada, b_ada, norm_ffn1, w_ffn1_in, w_ffn1_out, norm_mix, w_in, conv_w, conv_b, dt_bias, a_log, d_ssd, ssd_norm_w, w_a_proj, s5_lambda_re, s5_lambda_im, s5_b_re, s5_b_im, s5_c_re, s5_c_im, s5_d, s5_log_dt, w_b_glu, w_out, norm_ffn2, w_ffn2_in, w_ffn2_out, norm_final):
    bsz, seq, d = x.shape
    depth = w_ada.shape[0]
    m = bsz * seq
    d_inner = ssd_norm_w.shape[1]
    conv_dim = conv_w.shape[2]
    heads = dt_bias.shape[1]
    s5_width = w_b_glu.shape[1]
    off_xbc = d_inner
    off_dt = off_xbc + conv_dim
    off_u = off_dt + heads
    off_g = off_u + s5_width

    assert depth == 1, "the epilogue fusion below is written for a single layer"
    l = 0
    x2 = x.reshape(m, d)
    mods3 = _mods(c, w_ada[l], b_ada[l]).reshape(bsz, N_ADA, d)
    h = _prenorm(x2, norm_ffn1[l], mods3, 0, seq)

    whole = lambda w: ((0, w.shape[1], w.shape[1]),)
    jobs = (*_ffn_cast_jobs(w_ffn2_in[l], w_ffn2_out[l]),
            _CastJob(w_a_proj[l], whole(w_a_proj[l]), d_inner),
            _CastJob(w_out[l], whole(w_out[l]), d),
            _CastJob(w_b_glu[l], whole(w_b_glu[l]), s5_width),
            _CastJob(w_in[l].T, ((0, d, d),), w_in.shape[2]))
    x2, h, wa2, wb2, wo2, w_a, w_o, w_glu, wi = _ffn(
        h, x2, *_cast_weights(_ffn_cast_jobs(w_ffn1_in[l], w_ffn1_out[l])), mods3, norm_mix[l],
        2, 3, seq, jobs)

    off_bc = off_xbc + d_inner
    zs = _proj(h, wi, 0, d_inner, 1024, act=_silu)
    xs = _proj_conv(h, wi, off_xbc, conv_w[l][:, :d_inner], conv_b[l][:d_inner], 1024, seq, F32)
    bc = _proj_conv(h, wi, off_bc, conv_w[l][:, d_inner:], conv_b[l][d_inner:], 1024, seq, BF16)
    dtp = _proj(h, wi, off_dt, LANES, LANES)
    w_rest = wi[off_u:]
    u = _proj_chunked(h, w_rest, s5_width)
    gates = _proj(h, w_rest, s5_width, 2 * d, 1024, act=_sigmoid, out_dtype=BF16)

    ya = _ssd(xs, bc, dtp, zs, dt_bias[l], a_log[l], d_ssd[l], ssd_norm_w[l], bsz, seq)
    prep = _s5_prep(s5_lambda_re[l], s5_lambda_im[l], s5_log_dt[l],
                    s5_b_re[l], s5_b_im[l], s5_c_re[l], s5_c_im[l])
    yb = _s5(u, prep, s5_d[l], bsz, seq)

    x2, h = _merge(ya, yb, gates, x2, w_a, w_glu.reshape(yb.shape[0], yb.shape[2], 2 * d), w_o,
                   mods3, norm_ffn2[l], 5, 6, seq)

    (x2,) = _ffn(h, x2, wa2, wb2, wo2, mods3, norm_final, 8, None, seq)
    return x2.reshape(bsz, seq, d)
```

```python
import functools
import math
from typing import NamedTuple

import jax
import jax.numpy as jnp
from jax import lax
from jax.experimental import pallas as pl
from jax.experimental.pallas import tpu as pltpu

F32 = jnp.float32
BF16 = jnp.bfloat16
EPS = 1e-6

VMEM_LIMIT_BYTES = 56 * 1024 * 1024
LANES = 128
SUBLANES = 8
BF16_ROWS = 16

SSD_HEAD_DIM = 64
SSD_GROUPS = 4
SSD_STATE = 128
SSD_CONV = 4
SSD_CHUNK = 128
S5_GROUP_SIZE = 16
S5_STATE = 64
MXU_WIDTH = 256
S5_SLAB = 128
S5_GROUPS_PER_SLAB = S5_SLAB // S5_GROUP_SIZE
S5_SLAB_STATE = S5_GROUPS_PER_SLAB * S5_STATE
S5_CHUNK = 16
N_ADA = 9


def _params(*sem):
    return pltpu.CompilerParams(dimension_semantics=sem, vmem_limit_bytes=VMEM_LIMIT_BYTES)


def _sigmoid(v):
    return 0.5 * jnp.tanh(0.5 * v) + 0.5


def _silu(v):
    return v * _sigmoid(v)


def _dot(a, b):
    return jnp.dot(a, b, preferred_element_type=F32)


def _dot_nt(a, b):
    return lax.dot_general(a, b, (((1,), (1,)), ((), ())), preferred_element_type=F32)


def _mod_rms(x, nw, shift, scale):
    ms = jnp.mean(x * x, axis=-1, keepdims=True)
    return (x * lax.rsqrt(ms + EPS) * nw) * (1.0 + scale) + shift


def _mods_kernel(ct_ref, w_ref, b_ref, o_ref):
    ca = _silu(ct_ref[...])
    w = w_ref[...]
    rows = [jnp.sum(ca[:, b:b + 1] * w, axis=0, keepdims=True) for b in range(ca.shape[1])]
    o_ref[...] = jnp.concatenate(rows, axis=0) + b_ref[...]


def _mods(c, w_ada, b_ada):
    bsz, d = c.shape
    n = w_ada.shape[1]
    tn = 2048
    return pl.pallas_call(
        _mods_kernel,
        out_shape=jax.ShapeDtypeStruct((bsz, n), F32),
        grid=(n // tn,),
        in_specs=[pl.BlockSpec((d, bsz), lambda j: (0, 0)),
                  pl.BlockSpec((d, tn), lambda j: (0, j)),
                  pl.BlockSpec((1, tn), lambda j: (0, j))],
        out_specs=pl.BlockSpec((bsz, tn), lambda j: (0, j)),
        compiler_params=_params("arbitrary"),
        name="mods",
    )(c.T, w_ada, b_ada.reshape(1, n))


def _prenorm_kernel(x_ref, nw_ref, mods_ref, h_ref, *, k):
    h = _mod_rms(x_ref[...], nw_ref[...], mods_ref[k:k + 1, :], mods_ref[k + 1:k + 2, :])
    h_ref[...] = h.astype(h_ref.dtype)


def _prenorm(x2, nw, mods3, k, seq):
    m, d = x2.shape
    tm = min(1024, seq)
    per_b = seq // tm
    return pl.pallas_call(
        functools.partial(_prenorm_kernel, k=k),
        out_shape=jax.ShapeDtypeStruct((m, d), BF16),
        grid=(m // tm,),
        in_specs=[pl.BlockSpec((tm, d), lambda i: (i, 0)),
                  pl.BlockSpec((1, d), lambda i: (0, 0)),
                  pl.BlockSpec((None, N_ADA, d), lambda i: (i // per_b, 0, 0))],
        out_specs=pl.BlockSpec((tm, d), lambda i: (i, 0)),
        compiler_params=_params("arbitrary"),
        name="prenorm",
    )(x2, nw.reshape(1, d), mods3)


FFN_ROWS = 1024
FFN_HIDDEN = 512
FFN_EPI_CHUNKS = 8
FFN_SUB_BLOCKS = 2


class _CastJob(NamedTuple):
    src: jax.Array
    cols: tuple
    out_rows: int

    def chunk_rows(self, slots):
        need = -(-self.out_rows // slots)
        rows = next(r for r in range(BF16_ROWS, self.out_rows + 1, BF16_ROWS)
                    if r >= need and self.out_rows % r == 0)
        return rows


def _cast_chunk(job, rows, chunk, src_ref, dst_refs):
    src_rows = job.src.shape[0]
    out_chunk = jnp.minimum(chunk, job.out_rows // rows - 1)
    partial = src_rows % rows != 0 or job.out_rows != src_rows
    if partial:
        r = lax.broadcasted_iota(jnp.int32, (rows, 1), 0) + out_chunk * rows
        keep = r < src_rows
    for (c0, c1, width), dst in zip(job.cols, dst_refs):
        v = src_ref[:, c0:c1]
        if partial:
            v = jnp.where(keep, v, 0.0)
        dst[:, :c1 - c0] = v.astype(BF16)
        if width > c1 - c0:
            dst[:, c1 - c0:] = jnp.zeros((rows, width - (c1 - c0)), BF16)


def _ffn_kernel(h_ref, x_ref, wa_ref, wb_ref, wo_ref, mods_ref, nw_ref, *refs, k_gate, k_next, n_tiles,
                jobs):
    n_job_out = sum(len(job.cols) for job, _ in jobs)
    job_src = refs[:len(jobs)]
    refs = refs[len(jobs):]
    acc = refs[-1]
    out_refs = refs[:len(refs) - 1 - n_job_out]
    job_dst = refs[len(out_refs):-1]
    i = pl.program_id(0)
    j = pl.program_id(1)
    slot = i % 2

    def casts():
        chunk = i * FFN_EPI_CHUNKS + jnp.minimum(j, FFN_EPI_CHUNKS - 1)
        k = 0
        for (job, rows), src_ref in zip(jobs, job_src):
            _cast_chunk(job, rows, chunk, src_ref, job_dst[k:k + len(job.cols)])
            k += len(job.cols)

    @pl.when((i == 0) & (j == 0))
    def _():
        acc[...] = jnp.zeros_like(acc)

    def epilogue():
        rows = x_ref.shape[0]
        r0 = pl.multiple_of(jnp.minimum(j, FFN_EPI_CHUNKS - 1) * rows, rows)
        done = acc[1 - slot, pl.ds(r0, rows), :]
        xn = x_ref[...] + 0.5 * mods_ref[k_gate:k_gate + 1, :] * done
        if k_next is None:
            ms = jnp.mean(xn * xn, axis=-1, keepdims=True)
            out_refs[0][...] = xn * lax.rsqrt(ms + EPS) * nw_ref[...]
        else:
            out_refs[0][...] = xn
            hn = _mod_rms(xn, nw_ref[...], mods_ref[k_next:k_next + 1, :],
                          mods_ref[k_next + 1:k_next + 2, :])
            out_refs[1][...] = hn.astype(BF16)

    def matmuls():
        sub = h_ref.shape[0] // FFN_SUB_BLOCKS
        for r in range(FFN_SUB_BLOCKS):
            rows = pl.ds(r * sub, sub)
            h = h_ref[rows, :]
            a = _dot(h, wa_ref[...])
            b = _dot(h, wb_ref[...])
            act = (_silu(a) * b).astype(BF16)
            acc[slot, rows, :] = jnp.where(j == 0, 0.0, acc[slot, rows, :]) + _dot(act, wo_ref[...])

    @pl.when(i == 0)
    def _():
        casts()
        matmuls()

    @pl.when((i > 0) & (i < n_tiles))
    def _():
        epilogue()
        casts()
        matmuls()

    @pl.when(i == n_tiles)
    def _():
        epilogue()


def _ffn(h, x2, wa, wb, wo, mods3, nw_next, k_gate, k_next, seq, cast_jobs=()):
    m, d = x2.shape
    ffp = wa.shape[1]
    tm = min(FFN_ROWS, seq)
    tf = FFN_HIDDEN
    n_tiles = m // tm
    n_hidden = ffp // tf
    ec = FFN_EPI_CHUNKS
    rows = tm // ec
    per_b = seq // tm
    assert n_hidden >= ec

    def done_chunk(i, j):
        return (jnp.where(i == 0, 0, (i - 1) * ec + jnp.minimum(j, ec - 1)), 0)

    def hidden(i, j):
        return jnp.where(i == n_tiles, n_hidden - 1, j)

    def cast_chunk(i, j):
        return jnp.where(i == n_tiles, n_tiles * ec - 1, i * ec + jnp.minimum(j, ec - 1))

    out_shape = [jax.ShapeDtypeStruct((m, d), F32)]
    out_specs = [pl.BlockSpec((rows, d), done_chunk)]
    if k_next is not None:
        out_shape.append(jax.ShapeDtypeStruct((m, d), BF16))
        out_specs.append(pl.BlockSpec((rows, d), done_chunk))
    in_specs = [pl.BlockSpec((tm, d), lambda i, j: (jnp.minimum(i, n_tiles - 1), 0)),
                pl.BlockSpec((rows, d), done_chunk),
                pl.BlockSpec((d, tf), lambda i, j: (0, hidden(i, j))),
                pl.BlockSpec((d, tf), lambda i, j: (0, hidden(i, j))),
                pl.BlockSpec((tf, d), lambda i, j: (hidden(i, j), 0)),
                pl.BlockSpec((None, N_ADA, d), lambda i, j: (jnp.maximum(i - 1, 0) // per_b, 0, 0)),
                pl.BlockSpec((1, d), lambda i, j: (0, 0))]
    sized_jobs = []
    for job in cast_jobs:
        r = job.chunk_rows(n_tiles * ec)
        sized_jobs.append((job._replace(src=jax.ShapeDtypeStruct(job.src.shape, F32)), r))
        last_out = job.out_rows // r - 1
        last_in = (job.src.shape[0] - 1) // r
        in_specs.append(pl.BlockSpec(
            (r, job.src.shape[1]),
            lambda i, j, lo=last_out, li=last_in: (jnp.minimum(jnp.minimum(cast_chunk(i, j), lo), li), 0)))
        for _, _, width in job.cols:
            out_shape.append(jax.ShapeDtypeStruct((job.out_rows, width), BF16))
            out_specs.append(pl.BlockSpec(
                (r, width), lambda i, j, lo=last_out: (jnp.minimum(cast_chunk(i, j), lo), 0)))
    return pl.pallas_call(
        functools.partial(_ffn_kernel, k_gate=k_gate, k_next=k_next, n_tiles=n_tiles,
                          jobs=tuple(sized_jobs)),
        out_shape=out_shape,
        grid=(n_tiles + 1, n_hidden),
        in_specs=in_specs,
        out_specs=out_specs,
        scratch_shapes=[pltpu.VMEM((2, tm, d), F32)],
        compiler_params=_params("arbitrary", "arbitrary"),
        name="ffn",
    )(h, x2, wa, wb, wo, mods3, nw_next.reshape(1, d), *[job.src for job in cast_jobs])


PROJ_ROWS = 1024
PROJ_SUB = 256


def _proj_kernel(h_ref, w_ref, o_ref, *, act):
    tm = h_ref.shape[0]
    sub = tm if act is None else min(2 * PROJ_SUB, tm)
    w = w_ref[...]
    for r in range(tm // sub):
        y = _dot_nt(h_ref[r * sub:(r + 1) * sub, :], w)
        if act is not None:
            y = act(y)
        o_ref[r * sub:(r + 1) * sub, :] = y.astype(o_ref.dtype)


def _proj(h, wt, col0, n, tn, act=None, out_dtype=F32):
    m, d = h.shape
    tm = min(PROJ_ROWS, m)
    cb = col0 // tn
    return pl.pallas_call(
        functools.partial(_proj_kernel, act=act),
        out_shape=jax.ShapeDtypeStruct((m, n), out_dtype),
        grid=(m // tm, n // tn),
        in_specs=[pl.BlockSpec((tm, d), lambda i, j: (i, 0)),
                  pl.BlockSpec((tn, d), lambda i, j: (cb + j, 0))],
        out_specs=pl.BlockSpec((tm, tn), lambda i, j: (i, j)),
        compiler_params=_params("arbitrary", "arbitrary"),
        name="proj",
    )(h, wt)


def _proj_conv_kernel(h_ref, w_ref, cw_ref, cb_ref, o_ref, *refs, tiles_per_seq):
    plain_ref = refs[0] if len(refs) == 2 else None
    halo = refs[-1]
    tn = o_ref.shape[1]
    i = pl.program_id(0)
    j = pl.program_id(1)
    tm = h_ref.shape[0]
    sub = min(PROJ_SUB, tm)
    taps = SSD_CONV - 1

    @pl.when(i % tiles_per_seq == 0)
    def _():
        halo[j] = jnp.zeros(halo.shape[1:], F32)

    prev = halo[j]
    w = w_ref[...]
    cw = cw_ref[...]
    cb = cb_ref[...]
    for r in range(tm // sub):
        raw = _dot_nt(h_ref[r * sub:(r + 1) * sub, :], w)
        if plain_ref is not None:
            plain_ref[r * sub:(r + 1) * sub, :] = raw[:, tn:]
            raw = raw[:, :tn]
        ext = jnp.concatenate([prev, raw], axis=0)
        part = cw[0:1, :] * ext
        for k in range(1, taps):
            part = pltpu.roll(part, 1, axis=0) + cw[k:k + 1, :] * ext
        conv = pltpu.roll(part, 1, axis=0)[SUBLANES:, :] + (cb + cw[taps:taps + 1, :] * raw)
        o_ref[r * sub:(r + 1) * sub, :] = _silu(conv).astype(o_ref.dtype)
        prev = raw[sub - SUBLANES:, :]
    halo[j] = prev


def _proj_conv(h, wt, col0, conv_w, conv_b, tn, seq, out_dtype, plain=0):
    m, d = h.shape
    n = conv_w.shape[1]
    tm = min(PROJ_ROWS, seq)
    out_shape = [jax.ShapeDtypeStruct((m, n), out_dtype)]
    out_specs = [pl.BlockSpec((tm, tn), lambda i, j: (i, j))]
    if plain:
        assert n == tn
        w_spec = pl.BlockSpec((pl.Element(tn + plain), pl.Element(d)), lambda i, j: (col0, 0))
        out_shape.append(jax.ShapeDtypeStruct((m, plain), F32))
        out_specs.append(pl.BlockSpec((tm, plain), lambda i, j: (i, 0)))
    else:
        cb = col0 // tn
        w_spec = pl.BlockSpec((tn, d), lambda i, j: (cb + j, 0))
    out = pl.pallas_call(
        functools.partial(_proj_conv_kernel, tiles_per_seq=seq // tm),
        out_shape=out_shape,
        grid=(m // tm, n // tn),
        in_specs=[pl.BlockSpec((tm, d), lambda i, j: (i, 0)),
                  w_spec,
                  pl.BlockSpec((SSD_CONV, tn), lambda i, j: (0, j)),
                  pl.BlockSpec((1, tn), lambda i, j: (0, j))],
        out_specs=out_specs,
        scratch_shapes=[pltpu.VMEM((n // tn, SUBLANES, tn), F32)],
        compiler_params=_params("arbitrary", "arbitrary"),
        name="projconv",
    )(h, wt, conv_w, conv_b.reshape(1, n))
    return out if plain else out[0]


def _proj_chunked_kernel(h_ref, w_ref, o_ref, scr):
    n_slabs, rows, lq = o_ref.shape
    q = S5_SLAB
    L = lq // q
    tm = h_ref.shape[0]
    sub = min(PROJ_SUB, tm)
    w = w_ref[...]
    for r in range(tm // sub):
        tok = slice(r * sub, (r + 1) * sub)
        chunks = slice(r * sub // L, (r + 1) * sub // L)
        y = _dot_nt(h_ref[tok, :], w)
        for c in range(scr.shape[0]):
            scr[c, tok, :] = y[:, c * LANES:(c + 1) * LANES]
        for c in range(scr.shape[0]):
            s, off = divmod(c * LANES, q)
            for l in range(L):
                o_ref[s, chunks, l * q + off:l * q + off + LANES] = \
                    scr[c, pl.ds(r * sub + l, sub // L, stride=L), :]


def _proj_chunked(h, wt, n):
    m, d = h.shape
    tm = min(PROJ_ROWS, m)
    L = S5_CHUNK
    n_slabs = n // S5_SLAB
    return pl.pallas_call(
        _proj_chunked_kernel,
        out_shape=jax.ShapeDtypeStruct((n_slabs, m // L, L * S5_SLAB), F32),
        grid=(m // tm,),
        in_specs=[pl.BlockSpec((tm, d), lambda i: (i, 0)),
                  pl.BlockSpec((n, d), lambda i: (0, 0))],
        out_specs=pl.BlockSpec((n_slabs, tm // L, L * S5_SLAB), lambda i: (0, i, 0)),
        scratch_shapes=[pltpu.VMEM((n // LANES, tm, LANES), F32)],
        compiler_params=_params("arbitrary"),
        name="projchunk",
    )(h, wt)


def _split3(v):
    hi = v.astype(BF16)
    r = v - hi.astype(F32)
    mid = r.astype(BF16)
    lo = (r - mid.astype(F32)).astype(BF16)
    return hi, mid, lo


def _pair(v, j, first):
    return jnp.where(first, v[:, 2 * j:2 * j + 1], v[:, 2 * j + 1:2 * j + 2])


def _ssd_kernel(xs_ref, bc_ref, dt_ref, zs_ref, dtb_ref, alog_ref, dsk_ref, nw_ref, y_ref, state):
    L = SSD_CHUNK
    d_inner = xs_ref.shape[1]
    n_state = SSD_STATE
    gw = d_inner // SSD_GROUPS
    pairs_per_group = gw // LANES

    @pl.when(pl.program_id(1) == 0)
    def _():
        state[...] = jnp.zeros_like(state)

    dtr = dt_ref[...] + dtb_ref[...]
    dt = jnp.maximum(dtr, 0.0) + jnp.log1p(jnp.exp(-jnp.abs(dtr)))
    da = dt * (-jnp.exp(alog_ref[...]))
    row = lax.broadcasted_iota(jnp.int32, (L, L), 0)
    col = lax.broadcasted_iota(jnp.int32, (L, L), 1)
    causal = row >= col
    tril = jnp.where(causal, 1.0, 0.0).astype(BF16)
    cs = sum(_dot(tril, part) for part in _split3(da))
    cs_last = cs[L - 1:L, :]
    ecl = jnp.exp(cs_last)
    cs_t = cs.T
    dt_t = dt.T
    wst_t = (dt * jnp.exp(cs_last - cs)).T

    lane = lax.broadcasted_iota(jnp.int32, (L, LANES), 1)
    first = lane < SSD_HEAD_DIM
    first_row = first[0:1, :]

    for g in range(SSD_GROUPS):
        y_parts = []
        bm_g = bc_ref[:, g * n_state:(g + 1) * n_state]
        cm_g = bc_ref[:, (SSD_GROUPS + g) * n_state:(SSD_GROUPS + g + 1) * n_state]
        cb = _dot_nt(cm_g, bm_g)
        bm_t = bm_g.astype(F32).T
        for jj in range(pairs_per_group):
            j = g * pairs_per_group + jj
            h0, h1 = 2 * j, 2 * j + 1
            cols = slice(j * LANES, (j + 1) * LANES)
            gcols = slice(jj * LANES, (jj + 1) * LANES)
            xs_p = xs_ref[:, cols]
            xb = xs_p.astype(BF16)
            zero = jnp.zeros_like(xb)
            rhs = jnp.concatenate([jnp.where(first, xb, zero), jnp.where(first, zero, xb)], axis=0)
            lhs_y = []
            lhs_s = []
            cs_cols = []
            for h in (h0, h1):
                cs_col = jnp.broadcast_to(cs[:, h:h + 1], (L, L))
                cs_cols.append(cs_col)
                dec = jnp.exp(jnp.where(causal, cs_col - cs_t[h:h + 1, :], -jnp.inf))
                lhs_y.append((cb * dec * dt_t[h:h + 1, :]).astype(BF16))
                lhs_s.append((bm_t * wst_t[h:h + 1, :]).astype(BF16))
            ecs_p = jnp.exp(jnp.where(first, cs_cols[0], cs_cols[1]))
            y_off = _dot(cm_g, state[g, :, gcols].astype(BF16))
            y_p = (_dot(jnp.concatenate(lhs_y, axis=1), rhs)
                   + ecs_p * y_off
                   + dsk_ref[:, cols] * xs_p)
            y_parts.append(y_p * zs_ref[:, cols])
            st_new = _dot(jnp.concatenate(lhs_s, axis=1), rhs)
            state[g, :, gcols] = _pair(ecl, j, first_row) * state[g, :, gcols] + st_new

        seg = jnp.concatenate(y_parts, axis=1)
        ms = jnp.mean(seg * seg, axis=-1, keepdims=True)
        gsl = slice(g * gw, (g + 1) * gw)
        y_ref[:, gsl] = (seg * lax.rsqrt(ms + EPS) * nw_ref[:, gsl]).astype(y_ref.dtype)


def _ssd(xs, bc, dtp, zs, dt_bias, a_log, d_ssd, norm_w, bsz, seq):
    m, d_inner = xs.shape
    L = SSD_CHUNK
    nc = seq // L
    heads = d_inner // SSD_HEAD_DIM
    pad = LANES - heads
    row = lambda b, c: (b * nc + c, 0)
    const = lambda b, c: (0, 0)
    return pl.pallas_call(
        _ssd_kernel,
        out_shape=jax.ShapeDtypeStruct((m, d_inner), BF16),
        grid=(bsz, nc),
        in_specs=[pl.BlockSpec((L, d_inner), row),
                  pl.BlockSpec((L, bc.shape[1]), row),
                  pl.BlockSpec((L, LANES), row),
                  pl.BlockSpec((L, d_inner), row),
                  pl.BlockSpec((1, LANES), const),
                  pl.BlockSpec((1, LANES), const),
                  pl.BlockSpec((1, d_inner), const),
                  pl.BlockSpec((1, d_inner), const)],
        out_specs=pl.BlockSpec((L, d_inner), row),
        scratch_shapes=[pltpu.VMEM((SSD_GROUPS, SSD_STATE, d_inner // SSD_GROUPS), F32)],
        compiler_params=_params("arbitrary", "arbitrary"),
        name="ssd",
    )(xs, bc, dtp, zs,
      jnp.pad(dt_bias, (0, pad)).reshape(1, LANES), jnp.pad(a_log, (0, pad)).reshape(1, LANES),
      jnp.repeat(d_ssd, SSD_HEAD_DIM).reshape(1, d_inner), norm_w.reshape(1, d_inner))


def _s5_prep_kernel(lre_ref, lim_ref, ldt_ref, btr_ref, bti_ref, cnr_ref, cni_ref,
                    wend_ref, wcorr_ref, krev_ref, al_ref):
    L = S5_CHUNK
    q = S5_SLAB
    ns = S5_SLAB_STATE
    lr = jnp.minimum(lre_ref[...], -1e-4)
    li = lim_ref[...]
    dt = jnp.exp(ldt_ref[...])

    def power(k):
        mag = jnp.exp(lr * dt * float(k))
        ang = li * dt * float(k)
        return mag * jnp.cos(ang), mag * jnp.sin(ang)

    ar, ai = power(1)
    den = lr * lr + li * li
    nr = ar - 1.0
    kr = (nr * lr + ai * li) / den
    ki = (ai * lr - nr * li) / den

    rows = lax.broadcasted_iota(jnp.int32, (q, ns), 0) // S5_GROUP_SIZE
    cols = lax.broadcasted_iota(jnp.int32, (q, ns), 1) // S5_STATE
    same = rows == cols

    def block_diag(ref):
        return jnp.where(same, jnp.concatenate([ref[...]] * (ns // LANES), axis=1), 0.0)

    btr, bti = block_diag(btr_ref), block_diag(bti_ref)
    bbr = btr * kr - bti * ki
    bbi = btr * ki + bti * kr
    cr, ci = block_diag(cnr_ref), block_diag(cni_ref)

    for k in range(L):
        pr, pi = power(k)
        blk = jnp.concatenate([bbr * pr - bbi * pi, bbr * pi + bbi * pr], axis=1)
        wend_ref[(L - 1 - k) * q:(L - k) * q, :] = blk.astype(BF16)
    for l in range(L):
        pr, pi = power(l + 1)
        blk = jnp.concatenate([cr * pr - ci * pi, -(cr * pi) - ci * pr], axis=1)
        wcorr_ref[l * q:(l + 1) * q, :] = blk.astype(BF16)
    cstack = jnp.concatenate([cr, -ci], axis=1).astype(BF16)
    kw = _dot_nt(wend_ref[...], cstack)
    tb = MXU_WIDTH // q
    for b in range(tb):
        up = (tb - 1 - b) * q
        shifted = kw if up == 0 else jnp.concatenate([kw[up:], jnp.zeros((up, q), F32)], axis=0)
        krev_ref[:, b * q:(b + 1) * q] = shifted.astype(BF16)
    pr, pi = power(L)
    al_ref[...] = jnp.concatenate([pr, pi], axis=1)


def _s5_prep(lam_re, lam_im, log_dt, b_re, b_im, c_re, c_im):
    g, p = lam_re.shape
    width = g * S5_GROUP_SIZE
    n_slabs = width // S5_SLAB
    ns = S5_SLAB_STATE
    lq = S5_CHUNK * S5_SLAB
    row_vec = lambda v: v.reshape(1, g * p)
    twice = lambda v: jnp.concatenate([v, v], axis=1)
    bt = lambda v: twice(jnp.swapaxes(v, 1, 2).reshape(width, p))
    cn = lambda v: twice(v.reshape(width, p))
    vec = pl.BlockSpec((None, 1, ns), lambda s: (s, 0, 0))
    mat = pl.BlockSpec((S5_SLAB, LANES), lambda s: (s, 0))
    vec3 = lambda v: row_vec(v).reshape(n_slabs, 1, ns)
    return pl.pallas_call(
        _s5_prep_kernel,
        out_shape=[jax.ShapeDtypeStruct((n_slabs, lq, 2 * ns), BF16),
                   jax.ShapeDtypeStruct((n_slabs, lq, 2 * ns), BF16),
                   jax.ShapeDtypeStruct((n_slabs, lq, MXU_WIDTH), BF16),
                   jax.ShapeDtypeStruct((n_slabs, 1, 2 * ns), F32)],
        grid=(n_slabs,),
        in_specs=[vec, vec, vec, mat, mat, mat, mat],
        out_specs=[pl.BlockSpec((None, lq, 2 * ns), lambda s: (s, 0, 0)),
                   pl.BlockSpec((None, lq, 2 * ns), lambda s: (s, 0, 0)),
                   pl.BlockSpec((None, lq, MXU_WIDTH), lambda s: (s, 0, 0)),
                   pl.BlockSpec((None, 1, 2 * ns), lambda s: (s, 0, 0))],
        compiler_params=_params("arbitrary"),
        name="s5prep",
    )(vec3(lam_re), vec3(lam_im), vec3(jnp.repeat(log_dt, p)),
      bt(b_re), bt(b_im), cn(c_re), cn(c_im))


def _gelu_tanh(v):
    return 0.5 * v * (1.0 + jnp.tanh(math.sqrt(2.0 / math.pi) * (v + 0.044715 * (v * v * v))))


def _s5_kernel(u_ref, wend_ref, wcorr_ref, krev_ref, al_ref, d_ref, y_ref, e_scr, cin_scr, st_scr,
               tok_scr, *, blocks_per_seq):
    L = S5_CHUNK
    q = S5_SLAB
    ns = S5_SLAB_STATE
    rows = u_ref.shape[0]

    @pl.when(pl.program_id(1) % blocks_per_seq == 0)
    def _():
        st_scr[...] = jnp.zeros_like(st_scr)

    u = u_ref[...]
    ub = u.astype(BF16)
    e_scr[...] = _dot(ub, wend_ref[...])
    alr = al_ref[:, :ns]
    ali = al_ref[:, ns:]

    def tile_step(t, st):
        base = pl.multiple_of(t * SUBLANES, SUBLANES)
        e = e_scr[pl.ds(base, SUBLANES), :]
        sr, si = st
        carried = []
        for r in range(SUBLANES):
            carried.append(jnp.concatenate([sr, si], axis=1))
            er = e[r:r + 1, :ns]
            ei = e[r:r + 1, ns:]
            sr, si = alr * sr - ali * si + er, alr * si + ali * sr + ei
        cin_scr[pl.ds(base, SUBLANES), :] = jnp.concatenate(carried, axis=0)
        return sr, si

    sr, si = lax.fori_loop(0, rows // SUBLANES, tile_step, (st_scr[:, :ns], st_scr[:, ns:]))
    st_scr[...] = jnp.concatenate([sr, si], axis=1)

    corr = _dot_nt(cin_scr[...].astype(BF16), wcorr_ref[...])
    w = MXU_WIDTH
    nb = L * q // w
    intra = [_dot(ub[:, :(b + 1) * w], krev_ref[(nb - 1 - b) * w:, :]) for b in range(nb)]
    y = _gelu_tanh(jnp.concatenate(intra, axis=1) + corr + d_ref[...] * u)
    for c in range(tok_scr.shape[0]):
        for l in range(L):
            lo = l * q + c * LANES
            tok_scr[c, pl.ds(l, rows, stride=L), :] = y[:, lo:lo + LANES]
        y_ref[:, c * LANES:(c + 1) * LANES] = tok_scr[c].astype(y_ref.dtype)


def _s5(u2, prep, d_skip, bsz, seq):
    wend, wcorr, krev, al = prep
    n_slabs, chunks, lq = u2.shape
    L = S5_CHUNK
    q = S5_SLAB
    ns2 = 2 * S5_SLAB_STATE
    chunks_per_seq = seq // L
    rb = min(256, chunks_per_seq)
    blocks_per_seq = chunks_per_seq // rb
    per_w = MXU_WIDTH // q
    d_t = jnp.tile(d_skip.reshape(n_slabs, 1, q), (1, 1, L))
    slab = lambda s, r: (s, 0, 0)
    blk = lambda s, r: (s, r, 0)
    return pl.pallas_call(
        functools.partial(_s5_kernel, blocks_per_seq=blocks_per_seq),
        out_shape=jax.ShapeDtypeStruct((n_slabs // per_w, chunks * L, MXU_WIDTH), BF16),
        grid=(n_slabs, chunks // rb),
        in_specs=[pl.BlockSpec((None, rb, lq), blk),
                  pl.BlockSpec((None, lq, ns2), slab),
                  pl.BlockSpec((None, lq, ns2), slab),
                  pl.BlockSpec((None, lq, MXU_WIDTH), slab),
                  pl.BlockSpec((None, 1, ns2), slab),
                  pl.BlockSpec((None, 1, lq), slab)],
        out_specs=pl.BlockSpec((None, rb * L, q), lambda s, r: (s // per_w, r, s % per_w)),
        scratch_shapes=[pltpu.VMEM((rb, ns2), F32), pltpu.VMEM((rb, ns2), F32),
                        pltpu.VMEM((1, ns2), F32), pltpu.VMEM((q // LANES, rb * L, LANES), F32)],
        compiler_params=_params("arbitrary", "arbitrary"),
        name="s5",
    )(u2, wend, wcorr, krev, al, d_t)


MERGE_COLS = 512


def _merge_kernel(ya_ref, yb_ref, ga_ref, gb_ref, x_ref, wa_ref, wg_ref, wo_ref, mods_ref, nw_ref,
                  xo_ref, h_ref, *, k_gate, k_next):
    d = x_ref.shape[1]
    ya = ya_ref[...]
    yb = [yb_ref[s] for s in range(yb_ref.shape[0])]
    proj = None
    for c0 in range(0, d, MERGE_COLS):
        c1 = c0 + MERGE_COLS
        p_a = _dot(ya, wa_ref[:, c0:c1])
        glu_a = sum(_dot(v, wg_ref[s, :, c0:c1]) for s, v in enumerate(yb))
        glu_g = sum(_dot(v, wg_ref[s, :, d + c0:d + c1]) for s, v in enumerate(yb))
        merged = ga_ref[:, c0:c1] * p_a + gb_ref[:, c0:c1] * (glu_a * _sigmoid(glu_g))
        part = _dot(merged.astype(BF16), wo_ref[c0:c1, :])
        proj = part if proj is None else proj + part
    xn = x_ref[...] + mods_ref[k_gate:k_gate + 1, :] * proj
    xo_ref[...] = xn
    hn = _mod_rms(xn, nw_ref[...], mods_ref[k_next:k_next + 1, :], mods_ref[k_next + 1:k_next + 2, :])
    h_ref[...] = hn.astype(BF16)


def _merge(ya, yb_slabs, gates, x2, w_a, w_glu, w_o, mods3, nw_next, k_gate, k_next, seq):
    m, d = x2.shape
    n_slabs, _, q = yb_slabs.shape
    tm = min(256, seq)
    per_b = seq // tm
    row = lambda i: (i, 0)
    const = lambda i: (0, 0)
    once = pl.Buffered(1)
    return pl.pallas_call(
        functools.partial(_merge_kernel, k_gate=k_gate, k_next=k_next),
        out_shape=[jax.ShapeDtypeStruct((m, d), F32), jax.ShapeDtypeStruct((m, d), BF16)],
        grid=(m // tm,),
        in_specs=[pl.BlockSpec((tm, ya.shape[1]), row),
                  pl.BlockSpec((n_slabs, tm, q), lambda i: (0, i, 0)),
                  pl.BlockSpec((tm, d), row),
                  pl.BlockSpec((tm, d), lambda i: (i, 1)),
                  pl.BlockSpec((tm, d), row),
                  pl.BlockSpec(w_a.shape, const, pipeline_mode=once),
                  pl.BlockSpec(w_glu.shape, lambda i: (0, 0, 0), pipeline_mode=once),
                  pl.BlockSpec(w_o.shape, const, pipeline_mode=once),
                  pl.BlockSpec((None, N_ADA, d), lambda i: (i // per_b, 0, 0)),
                  pl.BlockSpec((1, d), const)],
        out_specs=[pl.BlockSpec((tm, d), row), pl.BlockSpec((tm, d), row)],
        compiler_params=_params("arbitrary"),
        name="merge",
    )(ya, yb_slabs, gates, gates, x2, w_a, w_glu, w_o, mods3, nw_next.reshape(1, d))


CAST_STEPS = 16


def _cast_kernel(*refs, jobs):
    chunk = pl.program_id(0)
    src = refs[:len(jobs)]
    dst = refs[len(jobs):]
    k = 0
    for (job, rows), src_ref in zip(jobs, src):
        _cast_chunk(job, rows, chunk, src_ref, dst[k:k + len(job.cols)])
        k += len(job.cols)


def _cast_weights(cast_jobs):
    in_specs, out_specs, out_shape, sized = [], [], [], []
    for job in cast_jobs:
        r = job.chunk_rows(CAST_STEPS)
        sized.append((job._replace(src=jax.ShapeDtypeStruct(job.src.shape, F32)), r))
        last_out = job.out_rows // r - 1
        last_in = (job.src.shape[0] - 1) // r
        in_specs.append(pl.BlockSpec(
            (r, job.src.shape[1]), lambda c, lo=last_out, li=last_in: (jnp.minimum(jnp.minimum(c, lo), li), 0)))
        for _, _, width in job.cols:
            out_shape.append(jax.ShapeDtypeStruct((job.out_rows, width), BF16))
            out_specs.append(pl.BlockSpec((r, width), lambda c, lo=last_out: (jnp.minimum(c, lo), 0)))
    return pl.pallas_call(
        functools.partial(_cast_kernel, jobs=tuple(sized)),
        out_shape=out_shape,
        grid=(CAST_STEPS,),
        in_specs=in_specs,
        out_specs=out_specs,
        compiler_params=_params("arbitrary"),
        name="castw",
    )(*[job.src for job in cast_jobs])


def _ffn_cast_jobs(w_in, w_out):
    ff = w_out.shape[0]
    ffp = -(-ff // FFN_HIDDEN) * FFN_HIDDEN
    return (_CastJob(w_in, ((0, ff, ffp), (ff, 2 * ff, ffp)), w_in.shape[0]),
            _CastJob(w_out, ((0, w_out.shape[1], w_out.shape[1]),), ffp))


def kernel(x, c, w_ada, b_ada, norm_ffn1, w_ffn1_in, w_ffn1_out, norm_mix, w_in, conv_w, conv_b, dt_bias, a_log, d_ssd, ssd_norm_w, w_a_proj, s5_lambda_re, s5_lambda_im, s5_b_re, s5_b_im, s5_c_re, s5_c_im, s5_d, s5_log_dt, w_b_glu, w_out, norm_ffn2, w_ffn2_in, w_ffn2_out, norm_final):
    bsz, seq, d = x.shape
    depth = w_ada.shape[0]
    m = bsz * seq
    d_inner = ssd_norm_w.shape[1]
    conv_dim = conv_w.shape[2]
    heads = dt_bias.shape[1]
    s5_width = w_b_glu.shape[1]
    off_xbc = d_inner
    off_dt = off_xbc + conv_dim
    off_u = off_dt + heads
    off_g = off_u + s5_width

    assert depth == 1, "the epilogue fusion below is written for a single layer"
    l = 0
    x2 = x.reshape(m, d)
    mods3 = _mods(c, w_ada[l], b_ada[l]).reshape(bsz, N_ADA, d)
    h = _prenorm(x2, norm_ffn1[l], mods3, 0, seq)

    whole = lambda w: ((0, w.shape[1], w.shape[1]),)
    jobs = (*_ffn_cast_jobs(w_ffn2_in[l], w_ffn2_out[l]),
            _CastJob(w_a_proj[l], whole(w_a_proj[l]), d_inner),
            _CastJob(w_out[l], whole(w_out[l]), d),
            _CastJob(w_b_glu[l], whole(w_b_glu[l]), s5_width),
            _CastJob(w_in[l].T, ((0, d, d),), w_in.shape[2]))
    x2, h, wa2, wb2, wo2, w_a, w_o, w_glu, wi = _ffn(
        h, x2, *_cast_weights(_ffn_cast_jobs(w_ffn1_in[l], w_ffn1_out[l])), mods3, norm_mix[l],
        2, 3, seq, jobs)

    off_bc = off_xbc + d_inner
    zs = _proj(h, wi, 0, d_inner, 1024, act=_silu)
    xs = _proj_conv(h, wi, off_xbc, conv_w[l][:, :d_inner], conv_b[l][:d_inner], 1024, seq, F32)
    bc, dtp = _proj_conv(h, wi, off_bc, conv_w[l][:, d_inner:], conv_b[l][d_inner:], 1024, seq, BF16,
                         plain=LANES)
    w_rest = wi[off_u:]
    u = _proj_chunked(h, w_rest, s5_width)
    gates = _proj(h, w_rest, s5_width, 2 * d, 1024, act=_sigmoid, out_dtype=BF16)

    ya = _ssd(xs, bc, dtp, zs, dt_bias[l], a_log[l], d_ssd[l], ssd_norm_w[l], bsz, seq)
    prep = _s5_prep(s5_lambda_re[l], s5_lambda_im[l], s5_log_dt[l],
                    s5_b_re[l], s5_b_im[l], s5_c_re[l], s5_c_im[l])
    yb = _s5(u, prep, s5_d[l], bsz, seq)

    x2, h = _merge(ya, yb, gates, x2, w_a, w_glu.reshape(yb.shape[0], yb.shape[2], 2 * d), w_o,
                   mods3, norm_ffn2[l], 5, 6, seq)

    (x2,) = _ffn(h, x2, wa2, wb2, wo2, mods3, norm_final, 8, None, seq)
    return x2.reshape(bsz, seq, d)
```

```python
import functools
import math
from typing import NamedTuple

import jax
import jax.numpy as jnp
from jax import lax
from jax.experimental import pallas as pl
from jax.experimental.pallas import tpu as pltpu

F32 = jnp.float32
BF16 = jnp.bfloat16
EPS = 1e-6

VMEM_LIMIT_BYTES = 56 * 1024 * 1024
LANES = 128
SUBLANES = 8
BF16_ROWS = 16

SSD_HEAD_DIM = 64
SSD_GROUPS = 4
SSD_STATE = 128
SSD_CONV = 4
SSD_CHUNK = 128
S5_GROUP_SIZE = 16
S5_STATE = 64
MXU_WIDTH = 256
S5_SLAB = 128
S5_GROUPS_PER_SLAB = S5_SLAB // S5_GROUP_SIZE
S5_SLAB_STATE = S5_GROUPS_PER_SLAB * S5_STATE
S5_CHUNK = 16
N_ADA = 9


def _params(*sem):
    return pltpu.CompilerParams(dimension_semantics=sem, vmem_limit_bytes=VMEM_LIMIT_BYTES)


def _sigmoid(v):
    return 0.5 * jnp.tanh(0.5 * v) + 0.5


def _silu(v):
    return v * _sigmoid(v)


def _dot(a, b):
    return jnp.dot(a, b, preferred_element_type=F32)


def _dot_nt(a, b):
    return lax.dot_general(a, b, (((1,), (1,)), ((), ())), preferred_element_type=F32)


def _mod_rms(x, nw, shift, scale):
    ms = jnp.mean(x * x, axis=-1, keepdims=True)
    return (x * lax.rsqrt(ms + EPS) * nw) * (1.0 + scale) + shift


def _mods_kernel(ct_ref, w_ref, b_ref, o_ref):
    ca = _silu(ct_ref[...])
    w = w_ref[...]
    rows = [jnp.sum(ca[:, b:b + 1] * w, axis=0, keepdims=True) for b in range(ca.shape[1])]
    o_ref[...] = jnp.concatenate(rows, axis=0) + b_ref[...]


def _mods(c, w_ada, b_ada):
    bsz, d = c.shape
    n = w_ada.shape[1]
    tn = 1024
    return pl.pallas_call(
        _mods_kernel,
        out_shape=jax.ShapeDtypeStruct((bsz, n), F32),
        grid=(n // tn,),
        in_specs=[pl.BlockSpec((d, bsz), lambda j: (0, 0)),
                  pl.BlockSpec((d, tn), lambda j: (0, j)),
                  pl.BlockSpec((1, tn), lambda j: (0, j))],
        out_specs=pl.BlockSpec((bsz, tn), lambda j: (0, j)),
        compiler_params=_params("arbitrary"),
        name="mods",
    )(c.T, w_ada, b_ada.reshape(1, n))


def _prenorm_kernel(x_ref, nw_ref, mods_ref, h_ref, *, k):
    h = _mod_rms(x_ref[...], nw_ref[...], mods_ref[k:k + 1, :], mods_ref[k + 1:k + 2, :])
    h_ref[...] = h.astype(h_ref.dtype)


def _prenorm(x2, nw, mods3, k, seq):
    m, d = x2.shape
    tm = min(1024, seq)
    per_b = seq // tm
    return pl.pallas_call(
        functools.partial(_prenorm_kernel, k=k),
        out_shape=jax.ShapeDtypeStruct((m, d), BF16),
        grid=(m // tm,),
        in_specs=[pl.BlockSpec((tm, d), lambda i: (i, 0)),
                  pl.BlockSpec((1, d), lambda i: (0, 0)),
                  pl.BlockSpec((None, N_ADA, d), lambda i: (i // per_b, 0, 0))],
        out_specs=pl.BlockSpec((tm, d), lambda i: (i, 0)),
        compiler_params=_params("arbitrary"),
        name="prenorm",
    )(x2, nw.reshape(1, d), mods3)


FFN_ROWS = 1024
FFN_HIDDEN = 512
FFN_EPI_CHUNKS = 8
FFN_SUB_BLOCKS = 2


class _CastJob(NamedTuple):
    src: jax.Array
    cols: tuple
    out_rows: int

    def chunk_rows(self, slots):
        need = -(-self.out_rows // slots)
        rows = next(r for r in range(BF16_ROWS, self.out_rows + 1, BF16_ROWS)
                    if r >= need and self.out_rows % r == 0)
        return rows


def _cast_chunk(job, rows, chunk, src_ref, dst_refs):
    src_rows = job.src.shape[0]
    out_chunk = jnp.minimum(chunk, job.out_rows // rows - 1)
    partial = src_rows % rows != 0 or job.out_rows != src_rows
    if partial:
        r = lax.broadcasted_iota(jnp.int32, (rows, 1), 0) + out_chunk * rows
        keep = r < src_rows
    for (c0, c1, width), dst in zip(job.cols, dst_refs):
        v = src_ref[:, c0:c1]
        if partial:
            v = jnp.where(keep, v, 0.0)
        dst[:, :c1 - c0] = v.astype(BF16)
        if width > c1 - c0:
            dst[:, c1 - c0:] = jnp.zeros((rows, width - (c1 - c0)), BF16)


def _ffn_kernel(h_ref, x_ref, wa_ref, wb_ref, wo_ref, mods_ref, nw_ref, *refs, k_gate, k_next, n_tiles,
                jobs):
    n_job_out = sum(len(job.cols) for job, _ in jobs)
    job_src = refs[:len(jobs)]
    refs = refs[len(jobs):]
    acc = refs[-1]
    out_refs = refs[:len(refs) - 1 - n_job_out]
    job_dst = refs[len(out_refs):-1]
    i = pl.program_id(0)
    j = pl.program_id(1)
    slot = i % 2

    def casts():
        chunk = i * FFN_EPI_CHUNKS + jnp.minimum(j, FFN_EPI_CHUNKS - 1)
        k = 0
        for (job, rows), src_ref in zip(jobs, job_src):
            _cast_chunk(job, rows, chunk, src_ref, job_dst[k:k + len(job.cols)])
            k += len(job.cols)

    @pl.when((i == 0) & (j == 0))
    def _():
        acc[...] = jnp.zeros_like(acc)

    def epilogue():
        rows = x_ref.shape[0]
        r0 = pl.multiple_of(jnp.minimum(j, FFN_EPI_CHUNKS - 1) * rows, rows)
        done = acc[1 - slot, pl.ds(r0, rows), :]
        xn = x_ref[...] + 0.5 * mods_ref[k_gate:k_gate + 1, :] * done
        if k_next is None:
            ms = jnp.mean(xn * xn, axis=-1, keepdims=True)
            out_refs[0][...] = xn * lax.rsqrt(ms + EPS) * nw_ref[...]
        else:
            out_refs[0][...] = xn
            hn = _mod_rms(xn, nw_ref[...], mods_ref[k_next:k_next + 1, :],
                          mods_ref[k_next + 1:k_next + 2, :])
            out_refs[1][...] = hn.astype(BF16)

    def matmuls():
        sub = h_ref.shape[0] // FFN_SUB_BLOCKS
        for r in range(FFN_SUB_BLOCKS):
            rows = pl.ds(r * sub, sub)
            h = h_ref[rows, :]
            a = _dot(h, wa_ref[...])
            b = _dot(h, wb_ref[...])
            act = (_silu(a) * b).astype(BF16)
            acc[slot, rows, :] = jnp.where(j == 0, 0.0, acc[slot, rows, :]) + _dot(act, wo_ref[...])

    @pl.when(i == 0)
    def _():
        casts()
        matmuls()

    @pl.when((i > 0) & (i < n_tiles))
    def _():
        epilogue()
        casts()
        matmuls()

    @pl.when(i == n_tiles)
    def _():
        epilogue()


def _ffn(h, x2, wa, wb, wo, mods3, nw_next, k_gate, k_next, seq, cast_jobs=()):
    m, d = x2.shape
    ffp = wa.shape[1]
    tm = min(FFN_ROWS, seq)
    tf = FFN_HIDDEN
    n_tiles = m // tm
    n_hidden = ffp // tf
    ec = FFN_EPI_CHUNKS
    rows = tm // ec
    per_b = seq // tm
    assert n_hidden >= ec

    def done_chunk(i, j):
        return (jnp.where(i == 0, 0, (i - 1) * ec + jnp.minimum(j, ec - 1)), 0)

    def hidden(i, j):
        return jnp.where(i == n_tiles, n_hidden - 1, j)

    def cast_chunk(i, j):
        return jnp.where(i == n_tiles, n_tiles * ec - 1, i * ec + jnp.minimum(j, ec - 1))

    out_shape = [jax.ShapeDtypeStruct((m, d), F32)]
    out_specs = [pl.BlockSpec((rows, d), done_chunk)]
    if k_next is not None:
        out_shape.append(jax.ShapeDtypeStruct((m, d), BF16))
        out_specs.append(pl.BlockSpec((rows, d), done_chunk))
    in_specs = [pl.BlockSpec((tm, d), lambda i, j: (jnp.minimum(i, n_tiles - 1), 0)),
                pl.BlockSpec((rows, d), done_chunk),
                pl.BlockSpec((d, tf), lambda i, j: (0, hidden(i, j))),
                pl.BlockSpec((d, tf), lambda i, j: (0, hidden(i, j))),
                pl.BlockSpec((tf, d), lambda i, j: (hidden(i, j), 0)),
                pl.BlockSpec((None, N_ADA, d), lambda i, j: (jnp.maximum(i - 1, 0) // per_b, 0, 0)),
                pl.BlockSpec((1, d), lambda i, j: (0, 0))]
    sized_jobs = []
    for job in cast_jobs:
        r = job.chunk_rows(n_tiles * ec)
        sized_jobs.append((job._replace(src=jax.ShapeDtypeStruct(job.src.shape, F32)), r))
        last_out = job.out_rows // r - 1
        last_in = (job.src.shape[0] - 1) // r
        in_specs.append(pl.BlockSpec(
            (r, job.src.shape[1]),
            lambda i, j, lo=last_out, li=last_in: (jnp.minimum(jnp.minimum(cast_chunk(i, j), lo), li), 0)))
        for _, _, width in job.cols:
            out_shape.append(jax.ShapeDtypeStruct((job.out_rows, width), BF16))
            out_specs.append(pl.BlockSpec(
                (r, width), lambda i, j, lo=last_out: (jnp.minimum(cast_chunk(i, j), lo), 0)))
    return pl.pallas_call(
        functools.partial(_ffn_kernel, k_gate=k_gate, k_next=k_next, n_tiles=n_tiles,
                          jobs=tuple(sized_jobs)),
        out_shape=out_shape,
        grid=(n_tiles + 1, n_hidden),
        in_specs=in_specs,
        out_specs=out_specs,
        scratch_shapes=[pltpu.VMEM((2, tm, d), F32)],
        compiler_params=_params("arbitrary", "arbitrary"),
        name="ffn",
    )(h, x2, wa, wb, wo, mods3, nw_next.reshape(1, d), *[job.src for job in cast_jobs])


PROJ_ROWS = 1024
PROJ_SUB = 256


def _weight_rows(rows, d, row0):
    return pl.BlockSpec((pl.Element(rows), pl.Element(d)),
                        lambda *idx: (pl.multiple_of(row0(*idx), BF16_ROWS), 0))


def _proj_kernel(h_ref, w_ref, o_ref, *, act):
    y = _dot_nt(h_ref[...], w_ref[...])
    if act is not None:
        y = act(y)
    o_ref[...] = y.astype(o_ref.dtype)


def _proj(h, wt, col0, n, tn, act=None, out_dtype=F32):
    m, d = h.shape
    tm = min(PROJ_ROWS, m)
    return pl.pallas_call(
        functools.partial(_proj_kernel, act=act),
        out_shape=jax.ShapeDtypeStruct((m, n), out_dtype),
        grid=(m // tm, n // tn),
        in_specs=[pl.BlockSpec((tm, d), lambda i, j: (i, 0)),
                  _weight_rows(tn, d, lambda i, j: col0 + j * tn)],
        out_specs=pl.BlockSpec((tm, tn), lambda i, j: (i, j)),
        compiler_params=_params("arbitrary", "arbitrary"),
        name="proj",
    )(h, wt)


def _proj_conv_kernel(h_ref, w_ref, cw_ref, cb_ref, o_ref, *refs, tiles_per_seq):
    plain_ref = refs[0] if len(refs) == 2 else None
    halo = refs[-1]
    tn = o_ref.shape[1]
    i = pl.program_id(0)
    j = pl.program_id(1)
    tm = h_ref.shape[0]
    sub = min(PROJ_SUB, tm)
    taps = SSD_CONV - 1

    @pl.when(i % tiles_per_seq == 0)
    def _():
        halo[j] = jnp.zeros(halo.shape[1:], F32)

    prev = halo[j]
    w = w_ref[...]
    cw = cw_ref[...]
    cb = cb_ref[...]
    for r in range(tm // sub):
        raw = _dot_nt(h_ref[r * sub:(r + 1) * sub, :], w)
        if plain_ref is not None:
            plain_ref[r * sub:(r + 1) * sub, :] = raw[:, tn:]
            raw = raw[:, :tn]
        ext = jnp.concatenate([prev, raw], axis=0)
        part = cw[0:1, :] * ext
        for k in range(1, taps):
            part = pltpu.roll(part, 1, axis=0) + cw[k:k + 1, :] * ext
        conv = pltpu.roll(part, 1, axis=0)[SUBLANES:, :] + (cb + cw[taps:taps + 1, :] * raw)
        o_ref[r * sub:(r + 1) * sub, :] = _silu(conv).astype(o_ref.dtype)
        prev = raw[sub - SUBLANES:, :]
    halo[j] = prev


def _proj_conv(h, wt, col0, conv_w, conv_b, tn, seq, out_dtype, plain=0):
    m, d = h.shape
    n = conv_w.shape[1]
    tm = min(PROJ_ROWS, seq)
    out_shape = [jax.ShapeDtypeStruct((m, n), out_dtype)]
    out_specs = [pl.BlockSpec((tm, tn), lambda i, j: (i, j))]
    if plain:
        assert n == tn
        out_shape.append(jax.ShapeDtypeStruct((m, plain), F32))
        out_specs.append(pl.BlockSpec((tm, plain), lambda i, j: (i, 0)))
    w_spec = _weight_rows(tn + plain, d, lambda i, j: col0 + j * tn)
    out = pl.pallas_call(
        functools.partial(_proj_conv_kernel, tiles_per_seq=seq // tm),
        out_shape=out_shape,
        grid=(m // tm, n // tn),
        in_specs=[pl.BlockSpec((tm, d), lambda i, j: (i, 0)),
                  w_spec,
                  pl.BlockSpec((SSD_CONV, tn), lambda i, j: (0, j)),
                  pl.BlockSpec((1, tn), lambda i, j: (0, j))],
        out_specs=out_specs,
        scratch_shapes=[pltpu.VMEM((n // tn, SUBLANES, tn), F32)],
        compiler_params=_params("arbitrary", "arbitrary"),
        name="projconv",
    )(h, wt, conv_w, conv_b.reshape(1, n))
    return out if plain else out[0]


def _proj_chunked_kernel(h_ref, w_ref, o_ref, scr):
    n_slabs, rows, lq = o_ref.shape
    q = S5_SLAB
    L = lq // q
    y = _dot_nt(h_ref[...], w_ref[...])
    for c in range(scr.shape[0]):
        scr[c] = y[:, c * LANES:(c + 1) * LANES]
    for c in range(scr.shape[0]):
        s, off = divmod(c * LANES, q)
        for l in range(L):
            o_ref[s, :, l * q + off:l * q + off + LANES] = scr[c, pl.ds(l, rows, stride=L), :]


def _proj_chunked(h, wt, row0, n):
    m, d = h.shape
    tm = min(PROJ_ROWS, m)
    L = S5_CHUNK
    n_slabs = n // S5_SLAB
    return pl.pallas_call(
        _proj_chunked_kernel,
        out_shape=jax.ShapeDtypeStruct((n_slabs, m // L, L * S5_SLAB), F32),
        grid=(m // tm,),
        in_specs=[pl.BlockSpec((tm, d), lambda i: (i, 0)),
                  _weight_rows(n, d, lambda i: row0)],
        out_specs=pl.BlockSpec((n_slabs, tm // L, L * S5_SLAB), lambda i: (0, i, 0)),
        scratch_shapes=[pltpu.VMEM((n // LANES, tm, LANES), F32)],
        compiler_params=_params("arbitrary"),
        name="projchunk",
    )(h, wt)


def _split3(v):
    hi = v.astype(BF16)
    r = v - hi.astype(F32)
    mid = r.astype(BF16)
    lo = (r - mid.astype(F32)).astype(BF16)
    return hi, mid, lo


def _pair(v, j, first):
    return jnp.where(first, v[:, 2 * j:2 * j + 1], v[:, 2 * j + 1:2 * j + 2])


def _ssd_kernel(xs_ref, bc_ref, dt_ref, zs_ref, dtb_ref, alog_ref, dsk_ref, nw_ref, y_ref, state):
    L = SSD_CHUNK
    d_inner = xs_ref.shape[1]
    n_state = SSD_STATE
    gw = d_inner // SSD_GROUPS
    pairs_per_group = gw // LANES

    @pl.when(pl.program_id(1) == 0)
    def _():
        state[...] = jnp.zeros_like(state)

    dtr = dt_ref[...] + dtb_ref[...]
    dt = jnp.maximum(dtr, 0.0) + jnp.log1p(jnp.exp(-jnp.abs(dtr)))
    da = dt * (-jnp.exp(alog_ref[...]))
    row = lax.broadcasted_iota(jnp.int32, (L, L), 0)
    col = lax.broadcasted_iota(jnp.int32, (L, L), 1)
    causal = row >= col
    tril = jnp.where(causal, 1.0, 0.0).astype(BF16)
    cs = sum(_dot(tril, part) for part in _split3(da))
    cs_last = cs[L - 1:L, :]
    ecs = jnp.exp(cs)
    ecl = jnp.exp(cs_last)
    cs_t = cs.T
    dt_t = dt.T
    wst_t = (dt * jnp.exp(cs_last - cs)).T

    lane = lax.broadcasted_iota(jnp.int32, (L, LANES), 1)
    first = lane < SSD_HEAD_DIM
    first_row = first[0:1, :]

    y_parts = []
    for g in range(SSD_GROUPS):
        bm_g = bc_ref[:, g * n_state:(g + 1) * n_state]
        cm_g = bc_ref[:, (SSD_GROUPS + g) * n_state:(SSD_GROUPS + g + 1) * n_state]
        cb = _dot_nt(cm_g, bm_g)
        bm_t = bm_g.astype(F32).T
        y_off = _dot(cm_g, state[g].astype(BF16))
        for jj in range(pairs_per_group):
            j = g * pairs_per_group + jj
            h0, h1 = 2 * j, 2 * j + 1
            cols = slice(j * LANES, (j + 1) * LANES)
            gcols = slice(jj * LANES, (jj + 1) * LANES)
            xs_p = xs_ref[:, cols]
            xb = xs_p.astype(BF16)
            zero = jnp.zeros_like(xb)
            rhs = jnp.concatenate([jnp.where(first, xb, zero), jnp.where(first, zero, xb)], axis=0)
            lhs_y = []
            lhs_s = []
            for h in (h0, h1):
                dec = jnp.exp(jnp.where(causal, cs[:, h:h + 1] - cs_t[h:h + 1, :], -jnp.inf))
                lhs_y.append((cb * dec * dt_t[h:h + 1, :]).astype(BF16))
                lhs_s.append((bm_t * wst_t[h:h + 1, :]).astype(BF16))
            y_p = (_dot(jnp.concatenate(lhs_y, axis=1), rhs)
                   + _pair(ecs, j, first) * y_off[:, gcols]
                   + dsk_ref[:, cols] * xs_p)
            y_parts.append(y_p)
            st_new = _dot(jnp.concatenate(lhs_s, axis=1), rhs)
            state[g, :, gcols] = _pair(ecl, j, first_row) * state[g, :, gcols] + st_new

    yz = jnp.concatenate(y_parts, axis=1) * zs_ref[...]
    outs = []
    for g in range(SSD_GROUPS):
        seg = yz[:, g * gw:(g + 1) * gw]
        ms = jnp.mean(seg * seg, axis=-1, keepdims=True)
        outs.append(seg * lax.rsqrt(ms + EPS))
    y_ref[...] = (jnp.concatenate(outs, axis=1) * nw_ref[...]).astype(y_ref.dtype)


def _ssd(xs, bc, dtp, zs, dt_bias, a_log, d_ssd, norm_w, bsz, seq):
    m, d_inner = xs.shape
    L = SSD_CHUNK
    nc = seq // L
    heads = d_inner // SSD_HEAD_DIM
    pad = LANES - heads
    row = lambda b, c: (b * nc + c, 0)
    const = lambda b, c: (0, 0)
    return pl.pallas_call(
        _ssd_kernel,
        out_shape=jax.ShapeDtypeStruct((m, d_inner), BF16),
        grid=(bsz, nc),
        in_specs=[pl.BlockSpec((L, d_inner), row),
                  pl.BlockSpec((L, bc.shape[1]), row),
                  pl.BlockSpec((L, LANES), row),
                  pl.BlockSpec((L, d_inner), row),
                  pl.BlockSpec((1, LANES), const),
                  pl.BlockSpec((1, LANES), const),
                  pl.BlockSpec((1, d_inner), const),
                  pl.BlockSpec((1, d_inner), const)],
        out_specs=pl.BlockSpec((L, d_inner), row),
        scratch_shapes=[pltpu.VMEM((SSD_GROUPS, SSD_STATE, d_inner // SSD_GROUPS), F32)],
        compiler_params=_params("arbitrary", "arbitrary"),
        name="ssd",
    )(xs, bc, dtp, zs,
      jnp.pad(dt_bias, (0, pad)).reshape(1, LANES), jnp.pad(a_log, (0, pad)).reshape(1, LANES),
      jnp.repeat(d_ssd, SSD_HEAD_DIM).reshape(1, d_inner), norm_w.reshape(1, d_inner))


def _s5_prep_kernel(lre_ref, lim_ref, ldt_ref, btr_ref, bti_ref, cnr_ref, cni_ref,
                    wend_ref, wcorr_ref, krev_ref, al_ref):
    L = S5_CHUNK
    q = S5_SLAB
    ns = S5_SLAB_STATE
    lr = jnp.minimum(lre_ref[...], -1e-4)
    li = lim_ref[...]
    dt = jnp.exp(ldt_ref[...])

    def power(k):
        mag = jnp.exp(lr * dt * float(k))
        ang = li * dt * float(k)
        return mag * jnp.cos(ang), mag * jnp.sin(ang)

    ar, ai = power(1)
    den = lr * lr + li * li
    nr = ar - 1.0
    kr = (nr * lr + ai * li) / den
    ki = (ai * lr - nr * li) / den

    rows = lax.broadcasted_iota(jnp.int32, (q, ns), 0) // S5_GROUP_SIZE
    cols = lax.broadcasted_iota(jnp.int32, (q, ns), 1) // S5_STATE
    same = rows == cols

    def block_diag(ref):
        return jnp.where(same, jnp.concatenate([ref[...]] * (ns // LANES), axis=1), 0.0)

    btr, bti = block_diag(btr_ref), block_diag(bti_ref)
    bbr = btr * kr - bti * ki
    bbi = btr * ki + bti * kr
    cr, ci = block_diag(cnr_ref), block_diag(cni_ref)

    for k in range(L):
        pr, pi = power(k)
        blk = jnp.concatenate([bbr * pr - bbi * pi, bbr * pi + bbi * pr], axis=1)
        wend_ref[(L - 1 - k) * q:(L - k) * q, :] = blk.astype(BF16)
    for l in range(L):
        pr, pi = power(l + 1)
        blk = jnp.concatenate([cr * pr - ci * pi, -(cr * pi) - ci * pr], axis=1)
        wcorr_ref[l * q:(l + 1) * q, :] = blk.astype(BF16)
    cstack = jnp.concatenate([cr, -ci], axis=1).astype(BF16)
    kw = _dot_nt(wend_ref[...], cstack)
    tb = MXU_WIDTH // q
    for b in range(tb):
        up = (tb - 1 - b) * q
        shifted = kw if up == 0 else jnp.concatenate([kw[up:], jnp.zeros((up, q), F32)], axis=0)
        krev_ref[:, b * q:(b + 1) * q] = shifted.astype(BF16)
    pr, pi = power(L)
    al_ref[...] = jnp.concatenate([pr, pi], axis=1)


def _s5_prep(lam_re, lam_im, log_dt, b_re, b_im, c_re, c_im):
    g, p = lam_re.shape
    width = g * S5_GROUP_SIZE
    n_slabs = width // S5_SLAB
    ns = S5_SLAB_STATE
    lq = S5_CHUNK * S5_SLAB
    row_vec = lambda v: v.reshape(1, g * p)
    twice = lambda v: jnp.concatenate([v, v], axis=1)
    bt = lambda v: twice(jnp.swapaxes(v, 1, 2).reshape(width, p))
    cn = lambda v: twice(v.reshape(width, p))
    vec = pl.BlockSpec((None, 1, ns), lambda s: (s, 0, 0))
    mat = pl.BlockSpec((S5_SLAB, LANES), lambda s: (s, 0))
    vec3 = lambda v: row_vec(v).reshape(n_slabs, 1, ns)
    return pl.pallas_call(
        _s5_prep_kernel,
        out_shape=[jax.ShapeDtypeStruct((n_slabs, lq, 2 * ns), BF16),
                   jax.ShapeDtypeStruct((n_slabs, lq, 2 * ns), BF16),
                   jax.ShapeDtypeStruct((n_slabs, lq, MXU_WIDTH), BF16),
                   jax.ShapeDtypeStruct((n_slabs, 1, 2 * ns), F32)],
        grid=(n_slabs,),
        in_specs=[vec, vec, vec, mat, mat, mat, mat],
        out_specs=[pl.BlockSpec((None, lq, 2 * ns), lambda s: (s, 0, 0)),
                   pl.BlockSpec((None, lq, 2 * ns), lambda s: (s, 0, 0)),
                   pl.BlockSpec((None, lq, MXU_WIDTH), lambda s: (s, 0, 0)),
                   pl.BlockSpec((None, 1, 2 * ns), lambda s: (s, 0, 0))],
        compiler_params=_params("arbitrary"),
        name="s5prep",
    )(vec3(lam_re), vec3(lam_im), vec3(jnp.repeat(log_dt, p)),
      bt(b_re), bt(b_im), cn(c_re), cn(c_im))


def _gelu_tanh(v):
    return 0.5 * v * (1.0 + jnp.tanh(math.sqrt(2.0 / math.pi) * (v + 0.044715 * (v * v * v))))


def _s5_kernel(u_ref, wend_ref, wcorr_ref, krev_ref, al_ref, d_ref, y_ref, e_scr, cin_scr, st_scr,
               tok_scr, *, blocks_per_seq):
    L = S5_CHUNK
    q = S5_SLAB
    ns = S5_SLAB_STATE
    rows = u_ref.shape[0]

    @pl.when(pl.program_id(1) % blocks_per_seq == 0)
    def _():
        st_scr[...] = jnp.zeros_like(st_scr)

    u = u_ref[...]
    ub = u.astype(BF16)
    e_scr[...] = _dot(ub, wend_ref[...])
    alr = al_ref[:, :ns]
    ali = al_ref[:, ns:]

    def tile_step(t, st):
        base = pl.multiple_of(t * SUBLANES, SUBLANES)
        e = e_scr[pl.ds(base, SUBLANES), :]
        sr, si = st
        carried = []
        for r in range(SUBLANES):
            carried.append(jnp.concatenate([sr, si], axis=1))
            er = e[r:r + 1, :ns]
            ei = e[r:r + 1, ns:]
            sr, si = alr * sr - ali * si + er, alr * si + ali * sr + ei
        cin_scr[pl.ds(base, SUBLANES), :] = jnp.concatenate(carried, axis=0)
        return sr, si

    sr, si = lax.fori_loop(0, rows // SUBLANES, tile_step, (st_scr[:, :ns], st_scr[:, ns:]))
    st_scr[...] = jnp.concatenate([sr, si], axis=1)

    corr = _dot_nt(cin_scr[...].astype(BF16), wcorr_ref[...])
    w = MXU_WIDTH
    nb = L * q // w
    intra = [_dot(ub[:, :(b + 1) * w], krev_ref[(nb - 1 - b) * w:, :]) for b in range(nb)]
    y = _gelu_tanh(jnp.concatenate(intra, axis=1) + corr + d_ref[...] * u)
    for c in range(tok_scr.shape[0]):
        for l in range(L):
            lo = l * q + c * LANES
            tok_scr[c, pl.ds(l, rows, stride=L), :] = y[:, lo:lo + LANES]
        y_ref[:, c * LANES:(c + 1) * LANES] = tok_scr[c].astype(y_ref.dtype)


def _s5(u2, prep, d_skip, bsz, seq):
    wend, wcorr, krev, al = prep
    n_slabs, chunks, lq = u2.shape
    L = S5_CHUNK
    q = S5_SLAB
    ns2 = 2 * S5_SLAB_STATE
    chunks_per_seq = seq // L
    rb = min(256, chunks_per_seq)
    blocks_per_seq = chunks_per_seq // rb
    per_w = MXU_WIDTH // q
    d_t = jnp.tile(d_skip.reshape(n_slabs, 1, q), (1, 1, L))
    slab = lambda s, r: (s, 0, 0)
    blk = lambda s, r: (s, r, 0)
    return pl.pallas_call(
        functools.partial(_s5_kernel, blocks_per_seq=blocks_per_seq),
        out_shape=jax.ShapeDtypeStruct((n_slabs // per_w, chunks * L, MXU_WIDTH), BF16),
        grid=(n_slabs, chunks // rb),
        in_specs=[pl.BlockSpec((None, rb, lq), blk),
                  pl.BlockSpec((None, lq, ns2), slab),
                  pl.BlockSpec((None, lq, ns2), slab),
                  pl.BlockSpec((None, lq, MXU_WIDTH), slab),
                  pl.BlockSpec((None, 1, ns2), slab),
                  pl.BlockSpec((None, 1, lq), slab)],
        out_specs=pl.BlockSpec((None, rb * L, q), lambda s, r: (s // per_w, r, s % per_w)),
        scratch_shapes=[pltpu.VMEM((rb, ns2), F32), pltpu.VMEM((rb, ns2), F32),
                        pltpu.VMEM((1, ns2), F32), pltpu.VMEM((q // LANES, rb * L, LANES), F32)],
        compiler_params=_params("arbitrary", "arbitrary"),
        name="s5",
    )(u2, wend, wcorr, krev, al, d_t)


MERGE_COLS = 512


def _merge_kernel(ya_ref, yb_ref, ga_ref, gb_ref, x_ref, wa_ref, wg_ref, wo_ref, mods_ref, nw_ref,
                  xo_ref, h_ref, *, k_gate, k_next):
    d = x_ref.shape[1]
    ya = ya_ref[...]
    yb = [yb_ref[s] for s in range(yb_ref.shape[0])]
    proj = None
    for c0 in range(0, d, MERGE_COLS):
        c1 = c0 + MERGE_COLS
        p_a = _dot(ya, wa_ref[:, c0:c1])
        glu_a = sum(_dot(v, wg_ref[s, :, c0:c1]) for s, v in enumerate(yb))
        glu_g = sum(_dot(v, wg_ref[s, :, d + c0:d + c1]) for s, v in enumerate(yb))
        merged = ga_ref[:, c0:c1] * p_a + gb_ref[:, c0:c1] * (glu_a * _sigmoid(glu_g))
        part = _dot(merged.astype(BF16), wo_ref[c0:c1, :])
        proj = part if proj is None else proj + part
    xn = x_ref[...] + mods_ref[k_gate:k_gate + 1, :] * proj
    xo_ref[...] = xn
    hn = _mod_rms(xn, nw_ref[...], mods_ref[k_next:k_next + 1, :], mods_ref[k_next + 1:k_next + 2, :])
    h_ref[...] = hn.astype(BF16)


def _merge(ya, yb_slabs, gates, x2, w_a, w_glu, w_o, mods3, nw_next, k_gate, k_next, seq):
    m, d = x2.shape
    n_slabs, _, q = yb_slabs.shape
    tm = min(256, seq)
    per_b = seq // tm
    row = lambda i: (i, 0)
    const = lambda i: (0, 0)
    once = pl.Buffered(1)
    return pl.pallas_call(
        functools.partial(_merge_kernel, k_gate=k_gate, k_next=k_next),
        out_shape=[jax.ShapeDtypeStruct((m, d), F32), jax.ShapeDtypeStruct((m, d), BF16)],
        grid=(m // tm,),
        in_specs=[pl.BlockSpec((tm, ya.shape[1]), row),
                  pl.BlockSpec((n_slabs, tm, q), lambda i: (0, i, 0)),
                  pl.BlockSpec((tm, d), row),
                  pl.BlockSpec((tm, d), lambda i: (i, 1)),
                  pl.BlockSpec((tm, d), row),
                  pl.BlockSpec(w_a.shape, const, pipeline_mode=once),
                  pl.BlockSpec(w_glu.shape, lambda i: (0, 0, 0), pipeline_mode=once),
                  pl.BlockSpec(w_o.shape, const, pipeline_mode=once),
                  pl.BlockSpec((None, N_ADA, d), lambda i: (i // per_b, 0, 0)),
                  pl.BlockSpec((1, d), const)],
        out_specs=[pl.BlockSpec((tm, d), row), pl.BlockSpec((tm, d), row)],
        compiler_params=_params("arbitrary"),
        name="merge",
    )(ya, yb_slabs, gates, gates, x2, w_a, w_glu, w_o, mods3, nw_next.reshape(1, d))


CAST_STEPS = 16


def _cast_kernel(*refs, jobs):
    chunk = pl.program_id(0)
    src = refs[:len(jobs)]
    dst = refs[len(jobs):]
    k = 0
    for (job, rows), src_ref in zip(jobs, src):
        _cast_chunk(job, rows, chunk, src_ref, dst[k:k + len(job.cols)])
        k += len(job.cols)


def _cast_weights(cast_jobs):
    in_specs, out_specs, out_shape, sized = [], [], [], []
    for job in cast_jobs:
        r = job.chunk_rows(CAST_STEPS)
        sized.append((job._replace(src=jax.ShapeDtypeStruct(job.src.shape, F32)), r))
        last_out = job.out_rows // r - 1
        last_in = (job.src.shape[0] - 1) // r
        in_specs.append(pl.BlockSpec(
            (r, job.src.shape[1]), lambda c, lo=last_out, li=last_in: (jnp.minimum(jnp.minimum(c, lo), li), 0)))
        for _, _, width in job.cols:
            out_shape.append(jax.ShapeDtypeStruct((job.out_rows, width), BF16))
            out_specs.append(pl.BlockSpec((r, width), lambda c, lo=last_out: (jnp.minimum(c, lo), 0)))
    return pl.pallas_call(
        functools.partial(_cast_kernel, jobs=tuple(sized)),
        out_shape=out_shape,
        grid=(CAST_STEPS,),
        in_specs=in_specs,
        out_specs=out_specs,
        compiler_params=_params("arbitrary"),
        name="castw",
    )(*[job.src for job in cast_jobs])


def _ffn_cast_jobs(w_in, w_out):
    ff = w_out.shape[0]
    ffp = -(-ff // FFN_HIDDEN) * FFN_HIDDEN
    return (_CastJob(w_in, ((0, ff, ffp), (ff, 2 * ff, ffp)), w_in.shape[0]),
            _CastJob(w_out, ((0, w_out.shape[1], w_out.shape[1]),), ffp))


def kernel(x, c, w_ada, b_ada, norm_ffn1, w_ffn1_in, w_ffn1_out, norm_mix, w_in, conv_w, conv_b, dt_bias, a_log, d_ssd, ssd_norm_w, w_a_proj, s5_lambda_re, s5_lambda_im, s5_b_re, s5_b_im, s5_c_re, s5_c_im, s5_d, s5_log_dt, w_b_glu, w_out, norm_ffn2, w_ffn2_in, w_ffn2_out, norm_final):
    bsz, seq, d = x.shape
    depth = w_ada.shape[0]
    m = bsz * seq
    d_inner = ssd_norm_w.shape[1]
    conv_dim = conv_w.shape[2]
    heads = dt_bias.shape[1]
    s5_width = w_b_glu.shape[1]
    off_xbc = d_inner
    off_dt = off_xbc + conv_dim
    off_u = off_dt + heads
    off_g = off_u + s5_width

    assert depth == 1, "the epilogue fusion below is written for a single layer"
    l = 0
    x2 = x.reshape(m, d)
    mods3 = _mods(c, w_ada[l], b_ada[l]).reshape(bsz, N_ADA, d)
    h = _prenorm(x2, norm_ffn1[l], mods3, 0, seq)

    whole = lambda w: ((0, w.shape[1], w.shape[1]),)
    jobs = (*_ffn_cast_jobs(w_ffn2_in[l], w_ffn2_out[l]),
            _CastJob(w_a_proj[l], whole(w_a_proj[l]), d_inner),
            _CastJob(w_out[l], whole(w_out[l]), d),
            _CastJob(w_b_glu[l], whole(w_b_glu[l]), s5_width),
            _CastJob(w_in[l].T, ((0, d, d),), w_in.shape[2]))
    x2, h, wa2, wb2, wo2, w_a, w_o, w_glu, wi = _ffn(
        h, x2, *_cast_weights(_ffn_cast_jobs(w_ffn1_in[l], w_ffn1_out[l])), mods3, norm_mix[l],
        2, 3, seq, jobs)

    off_bc = off_xbc + d_inner
    zs = _proj(h, wi, 0, d_inner, 1024, act=_silu)
    xs = _proj_conv(h, wi, off_xbc, conv_w[l][:, :d_inner], conv_b[l][:d_inner], 1024, seq, F32)
    bc, dtp = _proj_conv(h, wi, off_bc, conv_w[l][:, d_inner:], conv_b[l][d_inner:], 1024, seq, BF16,
                         plain=LANES)
    u = _proj_chunked(h, wi, off_u, s5_width)
    gates = _proj(h, wi, off_g, 2 * d, 1024, act=_sigmoid, out_dtype=BF16)

    ya = _ssd(xs, bc, dtp, zs, dt_bias[l], a_log[l], d_ssd[l], ssd_norm_w[l], bsz, seq)
    prep = _s5_prep(s5_lambda_re[l], s5_lambda_im[l], s5_log_dt[l],
                    s5_b_re[l], s5_b_im[l], s5_c_re[l], s5_c_im[l])
    yb = _s5(u, prep, s5_d[l], bsz, seq)

    x2, h = _merge(ya, yb, gates, x2, w_a, w_glu.reshape(yb.shape[0], yb.shape[2], 2 * d), w_o,
                   mods3, norm_ffn2[l], 5, 6, seq)

    (x2,) = _ffn(h, x2, wa2, wb2, wo2, mods3, norm_final, 8, None, seq)
    return x2.reshape(bsz, seq, d)
```

```python
import functools
import math
from typing import NamedTuple

import jax
import jax.numpy as jnp
from jax import lax
from jax.experimental import pallas as pl
from jax.experimental.pallas import tpu as pltpu

F32 = jnp.float32
BF16 = jnp.bfloat16
EPS = 1e-6

VMEM_LIMIT_BYTES = 56 * 1024 * 1024
LANES = 128
SUBLANES = 8
BF16_ROWS = 16

SSD_HEAD_DIM = 64
SSD_GROUPS = 4
SSD_STATE = 128
SSD_CONV = 4
SSD_CHUNK = 128
S5_GROUP_SIZE = 16
S5_STATE = 64
MXU_WIDTH = 256
S5_SLAB = 128
S5_GROUPS_PER_SLAB = S5_SLAB // S5_GROUP_SIZE
S5_SLAB_STATE = S5_GROUPS_PER_SLAB * S5_STATE
S5_CHUNK = 16
N_ADA = 9


def _params(*sem):
    return pltpu.CompilerParams(dimension_semantics=sem, vmem_limit_bytes=VMEM_LIMIT_BYTES)


def _sigmoid(v):
    return 0.5 * jnp.tanh(0.5 * v) + 0.5


def _silu(v):
    return v * _sigmoid(v)


def _dot(a, b):
    return jnp.dot(a, b, preferred_element_type=F32)


def _dot_nt(a, b):
    return lax.dot_general(a, b, (((1,), (1,)), ((), ())), preferred_element_type=F32)


def _mod_rms(x, nw, shift, scale):
    ms = jnp.mean(x * x, axis=-1, keepdims=True)
    return (x * lax.rsqrt(ms + EPS) * nw) * (1.0 + scale) + shift


def _mods_kernel(ct_ref, w_ref, b_ref, o_ref):
    ca = _silu(ct_ref[...])
    w = w_ref[...]
    rows = [jnp.sum(ca[:, b:b + 1] * w, axis=0, keepdims=True) for b in range(ca.shape[1])]
    o_ref[...] = jnp.concatenate(rows, axis=0) + b_ref[...]


def _mods(c, w_ada, b_ada):
    bsz, d = c.shape
    n = w_ada.shape[1]
    tn = 1024
    return pl.pallas_call(
        _mods_kernel,
        out_shape=jax.ShapeDtypeStruct((bsz, n), F32),
        grid=(n // tn,),
        in_specs=[pl.BlockSpec((d, bsz), lambda j: (0, 0)),
                  pl.BlockSpec((d, tn), lambda j: (0, j)),
                  pl.BlockSpec((1, tn), lambda j: (0, j))],
        out_specs=pl.BlockSpec((bsz, tn), lambda j: (0, j)),
        compiler_params=_params("arbitrary"),
        name="mods",
    )(c.T, w_ada, b_ada.reshape(1, n))


def _prenorm_kernel(x_ref, nw_ref, mods_ref, h_ref, *, k):
    h = _mod_rms(x_ref[...], nw_ref[...], mods_ref[k:k + 1, :], mods_ref[k + 1:k + 2, :])
    h_ref[...] = h.astype(h_ref.dtype)


def _prenorm(x2, nw, mods3, k, seq):
    m, d = x2.shape
    tm = min(1024, seq)
    per_b = seq // tm
    return pl.pallas_call(
        functools.partial(_prenorm_kernel, k=k),
        out_shape=jax.ShapeDtypeStruct((m, d), BF16),
        grid=(m // tm,),
        in_specs=[pl.BlockSpec((tm, d), lambda i: (i, 0)),
                  pl.BlockSpec((1, d), lambda i: (0, 0)),
                  pl.BlockSpec((None, N_ADA, d), lambda i: (i // per_b, 0, 0))],
        out_specs=pl.BlockSpec((tm, d), lambda i: (i, 0)),
        compiler_params=_params("arbitrary"),
        name="prenorm",
    )(x2, nw.reshape(1, d), mods3)


FFN_ROWS = 1024
FFN_HIDDEN = 512
FFN_EPI_CHUNKS = 8
FFN_SUB_BLOCKS = 2


class _CastJob(NamedTuple):
    src: jax.Array
    cols: tuple
    out_rows: int

    def chunk_rows(self, slots):
        need = -(-self.out_rows // slots)
        rows = next(r for r in range(BF16_ROWS, self.out_rows + 1, BF16_ROWS)
                    if r >= need and self.out_rows % r == 0)
        return rows


def _cast_chunk(job, rows, chunk, src_ref, dst_refs):
    src_rows = job.src.shape[0]
    out_chunk = jnp.minimum(chunk, job.out_rows // rows - 1)
    partial = src_rows % rows != 0 or job.out_rows != src_rows
    if partial:
        r = lax.broadcasted_iota(jnp.int32, (rows, 1), 0) + out_chunk * rows
        keep = r < src_rows
    for (c0, c1, width), dst in zip(job.cols, dst_refs):
        v = src_ref[:, c0:c1]
        if partial:
            v = jnp.where(keep, v, 0.0)
        dst[:, :c1 - c0] = v.astype(BF16)
        if width > c1 - c0:
            dst[:, c1 - c0:] = jnp.zeros((rows, width - (c1 - c0)), BF16)


def _ffn_kernel(h_ref, x_ref, wa_ref, wb_ref, wo_ref, mods_ref, nw_ref, *refs, k_gate, k_next, n_tiles,
                jobs):
    n_job_out = sum(len(job.cols) for job, _ in jobs)
    job_src = refs[:len(jobs)]
    refs = refs[len(jobs):]
    acc = refs[-1]
    out_refs = refs[:len(refs) - 1 - n_job_out]
    job_dst = refs[len(out_refs):-1]
    i = pl.program_id(0)
    j = pl.program_id(1)
    slot = i % 2

    def casts():
        chunk = i * FFN_EPI_CHUNKS + jnp.minimum(j, FFN_EPI_CHUNKS - 1)
        k = 0
        for (job, rows), src_ref in zip(jobs, job_src):
            _cast_chunk(job, rows, chunk, src_ref, job_dst[k:k + len(job.cols)])
            k += len(job.cols)

    @pl.when((i == 0) & (j == 0))
    def _():
        acc[...] = jnp.zeros_like(acc)

    def epilogue():
        rows = x_ref.shape[0]
        r0 = pl.multiple_of(jnp.minimum(j, FFN_EPI_CHUNKS - 1) * rows, rows)
        done = acc[1 - slot, pl.ds(r0, rows), :]
        xn = x_ref[...] + 0.5 * mods_ref[k_gate:k_gate + 1, :] * done
        if k_next is None:
            ms = jnp.mean(xn * xn, axis=-1, keepdims=True)
            out_refs[0][...] = xn * lax.rsqrt(ms + EPS) * nw_ref[...]
        else:
            out_refs[0][...] = xn
            hn = _mod_rms(xn, nw_ref[...], mods_ref[k_next:k_next + 1, :],
                          mods_ref[k_next + 1:k_next + 2, :])
            out_refs[1][...] = hn.astype(BF16)

    def matmuls():
        sub = h_ref.shape[0] // FFN_SUB_BLOCKS
        for r in range(FFN_SUB_BLOCKS):
            rows = pl.ds(r * sub, sub)
            h = h_ref[rows, :]
            a = _dot(h, wa_ref[...])
            b = _dot(h, wb_ref[...])
            act = (_silu(a) * b).astype(BF16)
            acc[slot, rows, :] = jnp.where(j == 0, 0.0, acc[slot, rows, :]) + _dot(act, wo_ref[...])

    @pl.when(i == 0)
    def _():
        casts()
        matmuls()

    @pl.when((i > 0) & (i < n_tiles))
    def _():
        epilogue()
        casts()
        matmuls()

    @pl.when(i == n_tiles)
    def _():
        epilogue()


def _ffn(h, x2, wa, wb, wo, mods3, nw_next, k_gate, k_next, seq, cast_jobs=()):
    m, d = x2.shape
    ffp = wa.shape[1]
    tm = min(FFN_ROWS, seq)
    tf = FFN_HIDDEN
    n_tiles = m // tm
    n_hidden = ffp // tf
    ec = FFN_EPI_CHUNKS
    rows = tm // ec
    per_b = seq // tm
    assert n_hidden >= ec

    def done_chunk(i, j):
        return (jnp.where(i == 0, 0, (i - 1) * ec + jnp.minimum(j, ec - 1)), 0)

    def hidden(i, j):
        return jnp.where(i == n_tiles, n_hidden - 1, j)

    def cast_chunk(i, j):
        return jnp.where(i == n_tiles, n_tiles * ec - 1, i * ec + jnp.minimum(j, ec - 1))

    out_shape = [jax.ShapeDtypeStruct((m, d), F32)]
    out_specs = [pl.BlockSpec((rows, d), done_chunk)]
    if k_next is not None:
        out_shape.append(jax.ShapeDtypeStruct((m, d), BF16))
        out_specs.append(pl.BlockSpec((rows, d), done_chunk))
    in_specs = [pl.BlockSpec((tm, d), lambda i, j: (jnp.minimum(i, n_tiles - 1), 0)),
                pl.BlockSpec((rows, d), done_chunk),
                pl.BlockSpec((d, tf), lambda i, j: (0, hidden(i, j))),
                pl.BlockSpec((d, tf), lambda i, j: (0, hidden(i, j))),
                pl.BlockSpec((tf, d), lambda i, j: (hidden(i, j), 0)),
                pl.BlockSpec((None, N_ADA, d), lambda i, j: (jnp.maximum(i - 1, 0) // per_b, 0, 0)),
                pl.BlockSpec((1, d), lambda i, j: (0, 0))]
    sized_jobs = []
    for job in cast_jobs:
        r = job.chunk_rows(n_tiles * ec)
        sized_jobs.append((job._replace(src=jax.ShapeDtypeStruct(job.src.shape, F32)), r))
        last_out = job.out_rows // r - 1
        last_in = (job.src.shape[0] - 1) // r
        in_specs.append(pl.BlockSpec(
            (r, job.src.shape[1]),
            lambda i, j, lo=last_out, li=last_in: (jnp.minimum(jnp.minimum(cast_chunk(i, j), lo), li), 0)))
        for _, _, width in job.cols:
            out_shape.append(jax.ShapeDtypeStruct((job.out_rows, width), BF16))
            out_specs.append(pl.BlockSpec(
                (r, width), lambda i, j, lo=last_out: (jnp.minimum(cast_chunk(i, j), lo), 0)))
    return pl.pallas_call(
        functools.partial(_ffn_kernel, k_gate=k_gate, k_next=k_next, n_tiles=n_tiles,
                          jobs=tuple(sized_jobs)),
        out_shape=out_shape,
        grid=(n_tiles + 1, n_hidden),
        in_specs=in_specs,
        out_specs=out_specs,
        scratch_shapes=[pltpu.VMEM((2, tm, d), F32)],
        compiler_params=_params("arbitrary", "arbitrary"),
        name="ffn",
    )(h, x2, wa, wb, wo, mods3, nw_next.reshape(1, d), *[job.src for job in cast_jobs])


PROJ_ROWS = 1024
PROJ_SUB = 256


def _weight_rows(rows, d, row0):
    return pl.BlockSpec((pl.Element(rows), pl.Element(d)),
                        lambda *idx: (pl.multiple_of(row0(*idx), BF16_ROWS), 0))


def _proj_kernel(h_ref, w_ref, o_ref, *, act):
    y = _dot_nt(h_ref[...], w_ref[...])
    if act is not None:
        y = act(y)
    o_ref[...] = y.astype(o_ref.dtype)


def _proj(h, wt, col0, n, tn, act=None, out_dtype=F32):
    m, d = h.shape
    tm = min(PROJ_ROWS, m)
    return pl.pallas_call(
        functools.partial(_proj_kernel, act=act),
        out_shape=jax.ShapeDtypeStruct((m, n), out_dtype),
        grid=(m // tm, n // tn),
        in_specs=[pl.BlockSpec((tm, d), lambda i, j: (i, 0)),
                  _weight_rows(tn, d, lambda i, j: col0 + j * tn)],
        out_specs=pl.BlockSpec((tm, tn), lambda i, j: (i, j)),
        compiler_params=_params("arbitrary", "arbitrary"),
        name="proj",
    )(h, wt)


def _proj_conv_kernel(h_ref, w_ref, cw_ref, cb_ref, o_ref, *refs, tiles_per_seq):
    plain_ref = refs[0] if len(refs) == 2 else None
    halo = refs[-1]
    tn = o_ref.shape[1]
    i = pl.program_id(0)
    j = pl.program_id(1)
    tm = h_ref.shape[0]
    sub = min(PROJ_SUB, tm)
    taps = SSD_CONV - 1

    @pl.when(i % tiles_per_seq == 0)
    def _():
        halo[j] = jnp.zeros(halo.shape[1:], F32)

    prev = halo[j]
    w = w_ref[...]
    cw = cw_ref[...]
    cb = cb_ref[...]
    for r in range(tm // sub):
        raw = _dot_nt(h_ref[r * sub:(r + 1) * sub, :], w)
        if plain_ref is not None:
            plain_ref[r * sub:(r + 1) * sub, :] = raw[:, tn:]
            raw = raw[:, :tn]
        ext = jnp.concatenate([prev, raw], axis=0)
        part = cw[0:1, :] * ext
        for k in range(1, taps):
            part = pltpu.roll(part, 1, axis=0) + cw[k:k + 1, :] * ext
        conv = pltpu.roll(part, 1, axis=0)[SUBLANES:, :] + (cb + cw[taps:taps + 1, :] * raw)
        o_ref[r * sub:(r + 1) * sub, :] = _silu(conv).astype(o_ref.dtype)
        prev = raw[sub - SUBLANES:, :]
    halo[j] = prev


def _proj_conv(h, wt, col0, conv_w, conv_b, tn, seq, out_dtype, plain=0):
    m, d = h.shape
    n = conv_w.shape[1]
    tm = min(PROJ_ROWS, seq)
    out_shape = [jax.ShapeDtypeStruct((m, n), out_dtype)]
    out_specs = [pl.BlockSpec((tm, tn), lambda i, j: (i, j))]
    if plain:
        assert n == tn
        out_shape.append(jax.ShapeDtypeStruct((m, plain), F32))
        out_specs.append(pl.BlockSpec((tm, plain), lambda i, j: (i, 0)))
    w_spec = _weight_rows(tn + plain, d, lambda i, j: col0 + j * tn)
    out = pl.pallas_call(
        functools.partial(_proj_conv_kernel, tiles_per_seq=seq // tm),
        out_shape=out_shape,
        grid=(m // tm, n // tn),
        in_specs=[pl.BlockSpec((tm, d), lambda i, j: (i, 0)),
                  w_spec,
                  pl.BlockSpec((SSD_CONV, tn), lambda i, j: (0, j)),
                  pl.BlockSpec((1, tn), lambda i, j: (0, j))],
        out_specs=out_specs,
        scratch_shapes=[pltpu.VMEM((n // tn, SUBLANES, tn), F32)],
        compiler_params=_params("arbitrary", "arbitrary"),
        name="projconv",
    )(h, wt, conv_w, conv_b.reshape(1, n))
    return out if plain else out[0]


def _proj_chunked_kernel(h_ref, w_ref, o_ref, scr):
    n_slabs, rows, lq = o_ref.shape
    q = S5_SLAB
    L = lq // q
    y = _dot_nt(h_ref[...], w_ref[...])
    for c in range(scr.shape[0]):
        scr[c] = y[:, c * LANES:(c + 1) * LANES]
    for c in range(scr.shape[0]):
        s, off = divmod(c * LANES, q)
        for l in range(L):
            o_ref[s, :, l * q + off:l * q + off + LANES] = scr[c, pl.ds(l, rows, stride=L), :]


def _proj_chunked(h, wt, row0, n):
    m, d = h.shape
    tm = min(PROJ_ROWS, m)
    L = S5_CHUNK
    n_slabs = n // S5_SLAB
    return pl.pallas_call(
        _proj_chunked_kernel,
        out_shape=jax.ShapeDtypeStruct((n_slabs, m // L, L * S5_SLAB), F32),
        grid=(m // tm,),
        in_specs=[pl.BlockSpec((tm, d), lambda i: (i, 0)),
                  _weight_rows(n, d, lambda i: row0)],
        out_specs=pl.BlockSpec((n_slabs, tm // L, L * S5_SLAB), lambda i: (0, i, 0)),
        scratch_shapes=[pltpu.VMEM((n // LANES, tm, LANES), F32)],
        compiler_params=_params("arbitrary"),
        name="projchunk",
    )(h, wt)


def _split3(v):
    hi = v.astype(BF16)
    r = v - hi.astype(F32)
    mid = r.astype(BF16)
    lo = (r - mid.astype(F32)).astype(BF16)
    return hi, mid, lo


def _pair(v, j, first):
    return jnp.where(first, v[:, 2 * j:2 * j + 1], v[:, 2 * j + 1:2 * j + 2])


def _ssd_kernel(xs_ref, bc_ref, dt_ref, zs_ref, dtb_ref, alog_ref, dsk_ref, nw_ref, y_ref, state):
    L = SSD_CHUNK
    d_inner = xs_ref.shape[1]
    n_state = SSD_STATE
    gw = d_inner // SSD_GROUPS
    pairs_per_group = gw // LANES

    @pl.when(pl.program_id(1) == 0)
    def _():
        state[...] = jnp.zeros_like(state)

    dtr = dt_ref[...] + dtb_ref[...]
    dt = jnp.maximum(dtr, 0.0) + jnp.log1p(jnp.exp(-jnp.abs(dtr)))
    da = dt * (-jnp.exp(alog_ref[...]))
    row = lax.broadcasted_iota(jnp.int32, (L, L), 0)
    col = lax.broadcasted_iota(jnp.int32, (L, L), 1)
    causal = row >= col
    tril = jnp.where(causal, 1.0, 0.0).astype(BF16)
    cs = sum(_dot(tril, part) for part in _split3(da))
    cs_last = cs[L - 1:L, :]
    ecs = jnp.exp(cs)
    ecl = jnp.exp(cs_last)
    cs_t = cs.T
    dt_t = dt.T
    wst_t = (dt * jnp.exp(cs_last - cs)).T

    lane = lax.broadcasted_iota(jnp.int32, (L, LANES), 1)
    first = lane < SSD_HEAD_DIM
    first_row = first[0:1, :]

    y_parts = []
    for g in range(SSD_GROUPS):
        bm_g = bc_ref[:, g * n_state:(g + 1) * n_state]
        cm_g = bc_ref[:, (SSD_GROUPS + g) * n_state:(SSD_GROUPS + g + 1) * n_state]
        cb = _dot_nt(cm_g, bm_g)
        bm_t = bm_g.astype(F32).T
        y_off = _dot(cm_g, state[g].astype(BF16))
        for jj in range(pairs_per_group):
            j = g * pairs_per_group + jj
            h0, h1 = 2 * j, 2 * j + 1
            cols = slice(j * LANES, (j + 1) * LANES)
            gcols = slice(jj * LANES, (jj + 1) * LANES)
            xs_p = xs_ref[:, cols]
            xb = xs_p.astype(BF16)
            zero = jnp.zeros_like(xb)
            rhs = jnp.concatenate([jnp.where(first, xb, zero), jnp.where(first, zero, xb)], axis=0)
            lhs_y = []
            lhs_s = []
            for h in (h0, h1):
                dec = jnp.exp(jnp.where(causal, cs[:, h:h + 1] - cs_t[h:h + 1, :], -jnp.inf))
                lhs_y.append((cb * dec * dt_t[h:h + 1, :]).astype(BF16))
                lhs_s.append((bm_t * wst_t[h:h + 1, :]).astype(BF16))
            y_p = (_dot(jnp.concatenate(lhs_y, axis=1), rhs)
                   + _pair(ecs, j, first) * y_off[:, gcols]
                   + dsk_ref[:, cols] * xs_p)
            y_parts.append(y_p)
            st_new = _dot(jnp.concatenate(lhs_s, axis=1), rhs)
            state[g, :, gcols] = _pair(ecl, j, first_row) * state[g, :, gcols] + st_new

    yz = jnp.concatenate(y_parts, axis=1) * zs_ref[...]
    outs = []
    for g in range(SSD_GROUPS):
        seg = yz[:, g * gw:(g + 1) * gw]
        ms = jnp.mean(seg * seg, axis=-1, keepdims=True)
        outs.append(seg * lax.rsqrt(ms + EPS))
    y_ref[...] = (jnp.concatenate(outs, axis=1) * nw_ref[...]).astype(y_ref.dtype)


def _ssd(xs, bc, dtp, zs, dt_bias, a_log, d_ssd, norm_w, bsz, seq):
    m, d_inner = xs.shape
    L = SSD_CHUNK
    nc = seq // L
    heads = d_inner // SSD_HEAD_DIM
    pad = LANES - heads
    row = lambda b, c: (b * nc + c, 0)
    const = lambda b, c: (0, 0)
    return pl.pallas_call(
        _ssd_kernel,
        out_shape=jax.ShapeDtypeStruct((m, d_inner), BF16),
        grid=(bsz, nc),
        in_specs=[pl.BlockSpec((L, d_inner), row),
                  pl.BlockSpec((L, bc.shape[1]), row),
                  pl.BlockSpec((L, LANES), row),
                  pl.BlockSpec((L, d_inner), row),
                  pl.BlockSpec((1, LANES), const),
                  pl.BlockSpec((1, LANES), const),
                  pl.BlockSpec((1, d_inner), const),
                  pl.BlockSpec((1, d_inner), const)],
        out_specs=pl.BlockSpec((L, d_inner), row),
        scratch_shapes=[pltpu.VMEM((SSD_GROUPS, SSD_STATE, d_inner // SSD_GROUPS), F32)],
        compiler_params=_params("arbitrary", "arbitrary"),
        name="ssd",
    )(xs, bc, dtp, zs,
      jnp.pad(dt_bias, (0, pad)).reshape(1, LANES), jnp.pad(a_log, (0, pad)).reshape(1, LANES),
      jnp.repeat(d_ssd, SSD_HEAD_DIM).reshape(1, d_inner), norm_w.reshape(1, d_inner))


def _s5_prep_kernel(lre_ref, lim_ref, ldt_ref, btr_ref, bti_ref, cnr_ref, cni_ref,
                    wend_ref, wcorr_ref, krev_ref, al_ref):
    L = S5_CHUNK
    q = S5_SLAB
    ns = S5_SLAB_STATE
    lr = jnp.minimum(lre_ref[...], -1e-4)
    li = lim_ref[...]
    dt = jnp.exp(ldt_ref[...])

    def power(k):
        mag = jnp.exp(lr * dt * float(k))
        ang = li * dt * float(k)
        return mag * jnp.cos(ang), mag * jnp.sin(ang)

    ar, ai = power(1)
    den = lr * lr + li * li
    nr = ar - 1.0
    kr = (nr * lr + ai * li) / den
    ki = (ai * lr - nr * li) / den

    rows = lax.broadcasted_iota(jnp.int32, (q, ns), 0) // S5_GROUP_SIZE
    cols = lax.broadcasted_iota(jnp.int32, (q, ns), 1) // S5_STATE
    same = rows == cols

    def block_diag(ref):
        return jnp.where(same, jnp.concatenate([ref[...]] * (ns // LANES), axis=1), 0.0)

    btr, bti = block_diag(btr_ref), block_diag(bti_ref)
    bbr = btr * kr - bti * ki
    bbi = btr * ki + bti * kr
    cr, ci = block_diag(cnr_ref), block_diag(cni_ref)

    for k in range(L):
        pr, pi = power(k)
        blk = jnp.concatenate([bbr * pr - bbi * pi, bbr * pi + bbi * pr], axis=1)
        wend_ref[(L - 1 - k) * q:(L - k) * q, :] = blk.astype(BF16)
    for l in range(L):
        pr, pi = power(l + 1)
        blk = jnp.concatenate([cr * pr - ci * pi, -(cr * pi) - ci * pr], axis=1)
        wcorr_ref[l * q:(l + 1) * q, :] = blk.astype(BF16)
    cstack = jnp.concatenate([cr, -ci], axis=1).astype(BF16)
    kw = _dot_nt(wend_ref[...], cstack)
    tb = MXU_WIDTH // q
    for b in range(tb):
        up = (tb - 1 - b) * q
        shifted = kw if up == 0 else jnp.concatenate([kw[up:], jnp.zeros((up, q), F32)], axis=0)
        krev_ref[:, b * q:(b + 1) * q] = shifted.astype(BF16)
    pr, pi = power(L)
    al_ref[...] = jnp.concatenate([pr, pi], axis=1)


def _s5_prep_call(lam_re, lam_im, log_dt, b_re, b_im, c_re, c_im, slab_of):
    g, p = lam_re.shape
    width = g * S5_GROUP_SIZE
    n_slabs = width // S5_SLAB
    ns = S5_SLAB_STATE
    lq = S5_CHUNK * S5_SLAB
    row_vec = lambda v: v.reshape(1, g * p)
    twice = lambda v: jnp.concatenate([v, v], axis=1)
    bt = lambda v: twice(jnp.swapaxes(v, 1, 2).reshape(width, p))
    cn = lambda v: twice(v.reshape(width, p))
    vec = pl.BlockSpec((None, 1, ns), lambda c: (slab_of(c), 0, 0))
    mat = pl.BlockSpec((S5_SLAB, LANES), lambda c: (slab_of(c), 0))
    vec3 = lambda v: row_vec(v).reshape(n_slabs, 1, ns)
    slab3 = lambda rows, cols: pl.BlockSpec((None, rows, cols), lambda c: (slab_of(c), 0, 0))
    operands = [vec3(lam_re), vec3(lam_im), vec3(jnp.repeat(log_dt, p)),
                bt(b_re), bt(b_im), cn(c_re), cn(c_im)]
    out_shape = [jax.ShapeDtypeStruct((n_slabs, lq, 2 * ns), BF16),
                 jax.ShapeDtypeStruct((n_slabs, lq, 2 * ns), BF16),
                 jax.ShapeDtypeStruct((n_slabs, lq, MXU_WIDTH), BF16),
                 jax.ShapeDtypeStruct((n_slabs, 1, 2 * ns), F32)]
    out_specs = [slab3(lq, 2 * ns), slab3(lq, 2 * ns), slab3(lq, MXU_WIDTH), slab3(1, 2 * ns)]
    return operands, [vec, vec, vec, mat, mat, mat, mat], out_shape, out_specs, n_slabs


def _gelu_tanh(v):
    return 0.5 * v * (1.0 + jnp.tanh(math.sqrt(2.0 / math.pi) * (v + 0.044715 * (v * v * v))))


def _s5_kernel(u_ref, wend_ref, wcorr_ref, krev_ref, al_ref, d_ref, y_ref, e_scr, cin_scr, st_scr,
               tok_scr, *, blocks_per_seq):
    L = S5_CHUNK
    q = S5_SLAB
    ns = S5_SLAB_STATE
    rows = u_ref.shape[0]

    @pl.when(pl.program_id(1) % blocks_per_seq == 0)
    def _():
        st_scr[...] = jnp.zeros_like(st_scr)

    u = u_ref[...]
    ub = u.astype(BF16)
    e_scr[...] = _dot(ub, wend_ref[...])
    alr = al_ref[:, :ns]
    ali = al_ref[:, ns:]

    def tile_step(t, st):
        base = pl.multiple_of(t * SUBLANES, SUBLANES)
        e = e_scr[pl.ds(base, SUBLANES), :]
        sr, si = st
        carried = []
        for r in range(SUBLANES):
            carried.append(jnp.concatenate([sr, si], axis=1))
            er = e[r:r + 1, :ns]
            ei = e[r:r + 1, ns:]
            sr, si = alr * sr - ali * si + er, alr * si + ali * sr + ei
        cin_scr[pl.ds(base, SUBLANES), :] = jnp.concatenate(carried, axis=0)
        return sr, si

    sr, si = lax.fori_loop(0, rows // SUBLANES, tile_step, (st_scr[:, :ns], st_scr[:, ns:]))
    st_scr[...] = jnp.concatenate([sr, si], axis=1)

    corr = _dot_nt(cin_scr[...].astype(BF16), wcorr_ref[...])
    w = MXU_WIDTH
    nb = L * q // w
    intra = [_dot(ub[:, :(b + 1) * w], krev_ref[(nb - 1 - b) * w:, :]) for b in range(nb)]
    y = _gelu_tanh(jnp.concatenate(intra, axis=1) + corr + d_ref[...] * u)
    for c in range(tok_scr.shape[0]):
        for l in range(L):
            lo = l * q + c * LANES
            tok_scr[c, pl.ds(l, rows, stride=L), :] = y[:, lo:lo + LANES]
        y_ref[:, c * LANES:(c + 1) * LANES] = tok_scr[c].astype(y_ref.dtype)


def _s5(u2, prep, d_skip, bsz, seq):
    wend, wcorr, krev, al = prep
    n_slabs, chunks, lq = u2.shape
    L = S5_CHUNK
    q = S5_SLAB
    ns2 = 2 * S5_SLAB_STATE
    chunks_per_seq = seq // L
    rb = min(256, chunks_per_seq)
    blocks_per_seq = chunks_per_seq // rb
    per_w = MXU_WIDTH // q
    d_t = jnp.tile(d_skip.reshape(n_slabs, 1, q), (1, 1, L))
    slab = lambda s, r: (s, 0, 0)
    blk = lambda s, r: (s, r, 0)
    return pl.pallas_call(
        functools.partial(_s5_kernel, blocks_per_seq=blocks_per_seq),
        out_shape=jax.ShapeDtypeStruct((n_slabs // per_w, chunks * L, MXU_WIDTH), BF16),
        grid=(n_slabs, chunks // rb),
        in_specs=[pl.BlockSpec((None, rb, lq), blk),
                  pl.BlockSpec((None, lq, ns2), slab),
                  pl.BlockSpec((None, lq, ns2), slab),
                  pl.BlockSpec((None, lq, MXU_WIDTH), slab),
                  pl.BlockSpec((None, 1, ns2), slab),
                  pl.BlockSpec((None, 1, lq), slab)],
        out_specs=pl.BlockSpec((None, rb * L, q), lambda s, r: (s // per_w, r, s % per_w)),
        scratch_shapes=[pltpu.VMEM((rb, ns2), F32), pltpu.VMEM((rb, ns2), F32),
                        pltpu.VMEM((1, ns2), F32), pltpu.VMEM((q // LANES, rb * L, LANES), F32)],
        compiler_params=_params("arbitrary", "arbitrary"),
        name="s5",
    )(u2, wend, wcorr, krev, al, d_t)


def _merge_kernel(ya_ref, yb_ref, ga_ref, gb_ref, x_ref, wa_ref, wg_ref, wo_ref, mods_ref, nw_ref,
                  xo_ref, h_ref, *, k_gate, k_next):
    d = x_ref.shape[1]
    p_a = _dot(ya_ref[...], wa_ref[...])
    glu = sum(_dot(yb_ref[s], wg_ref[s]) for s in range(yb_ref.shape[0]))
    p_b = glu[:, :d] * _sigmoid(glu[:, d:])
    merged = ga_ref[...] * p_a + gb_ref[...] * p_b
    xn = x_ref[...] + mods_ref[k_gate:k_gate + 1, :] * _dot(merged.astype(BF16), wo_ref[...])
    xo_ref[...] = xn
    hn = _mod_rms(xn, nw_ref[...], mods_ref[k_next:k_next + 1, :], mods_ref[k_next + 1:k_next + 2, :])
    h_ref[...] = hn.astype(BF16)


def _merge(ya, yb_slabs, gates, x2, w_a, w_glu, w_o, mods3, nw_next, k_gate, k_next, seq):
    m, d = x2.shape
    n_slabs, _, q = yb_slabs.shape
    tm = min(256, seq)
    per_b = seq // tm
    row = lambda i: (i, 0)
    const = lambda i: (0, 0)
    once = pl.Buffered(1)
    return pl.pallas_call(
        functools.partial(_merge_kernel, k_gate=k_gate, k_next=k_next),
        out_shape=[jax.ShapeDtypeStruct((m, d), F32), jax.ShapeDtypeStruct((m, d), BF16)],
        grid=(m // tm,),
        in_specs=[pl.BlockSpec((tm, ya.shape[1]), row),
                  pl.BlockSpec((n_slabs, tm, q), lambda i: (0, i, 0)),
                  pl.BlockSpec((tm, d), row),
                  pl.BlockSpec((tm, d), lambda i: (i, 1)),
                  pl.BlockSpec((tm, d), row),
                  pl.BlockSpec(w_a.shape, const, pipeline_mode=once),
                  pl.BlockSpec(w_glu.shape, lambda i: (0, 0, 0), pipeline_mode=once),
                  pl.BlockSpec(w_o.shape, const, pipeline_mode=once),
                  pl.BlockSpec((None, N_ADA, d), lambda i: (i // per_b, 0, 0)),
                  pl.BlockSpec((1, d), const)],
        out_specs=[pl.BlockSpec((tm, d), row), pl.BlockSpec((tm, d), row)],
        compiler_params=_params("arbitrary"),
        name="merge",
    )(ya, yb_slabs, gates, gates, x2, w_a, w_glu, w_o, mods3, nw_next.reshape(1, d))


CAST_STEPS = 16


S5_PREP_INPUTS = 7


def _cast_kernel(*refs, jobs, s5_slabs):
    chunk = pl.program_id(0)
    n_in = len(jobs) + (S5_PREP_INPUTS if s5_slabs else 0)
    n_dst = sum(len(job.cols) for job, _ in jobs)
    src, s5_in = refs[:len(jobs)], refs[len(jobs):n_in]
    dst, s5_out = refs[n_in:n_in + n_dst], refs[n_in + n_dst:]
    k = 0
    for (job, rows), src_ref in zip(jobs, src):
        _cast_chunk(job, rows, chunk, src_ref, dst[k:k + len(job.cols)])
        k += len(job.cols)
    if s5_slabs:
        @pl.when(chunk < s5_slabs)
        def _():
            _s5_prep_kernel(*s5_in, *s5_out)


def _cast_weights(cast_jobs, s5_params=None):
    in_specs, out_specs, out_shape, sized = [], [], [], []
    for job in cast_jobs:
        r = job.chunk_rows(CAST_STEPS)
        sized.append((job._replace(src=jax.ShapeDtypeStruct(job.src.shape, F32)), r))
        last_out = job.out_rows // r - 1
        last_in = (job.src.shape[0] - 1) // r
        in_specs.append(pl.BlockSpec(
            (r, job.src.shape[1]), lambda c, lo=last_out, li=last_in: (jnp.minimum(jnp.minimum(c, lo), li), 0)))
        for _, _, width in job.cols:
            out_shape.append(jax.ShapeDtypeStruct((job.out_rows, width), BF16))
            out_specs.append(pl.BlockSpec((r, width), lambda c, lo=last_out: (jnp.minimum(c, lo), 0)))
    operands = [job.src for job in cast_jobs]
    s5_slabs = 0
    if s5_params is not None:
        width = s5_params[0].shape[0] * S5_GROUP_SIZE
        last = width // S5_SLAB - 1
        ops, ins, shapes, outs, s5_slabs = _s5_prep_call(*s5_params, lambda c: jnp.minimum(c, last))
        assert s5_slabs <= CAST_STEPS and len(ops) == S5_PREP_INPUTS
        operands += ops
        in_specs += ins
        out_shape += shapes
        out_specs += outs
    return pl.pallas_call(
        functools.partial(_cast_kernel, jobs=tuple(sized), s5_slabs=s5_slabs),
        out_shape=out_shape,
        grid=(CAST_STEPS,),
        in_specs=in_specs,
        out_specs=out_specs,
        compiler_params=_params("arbitrary"),
        name="castw",
    )(*operands)


def _ffn_cast_jobs(w_in, w_out):
    ff = w_out.shape[0]
    ffp = -(-ff // FFN_HIDDEN) * FFN_HIDDEN
    return (_CastJob(w_in, ((0, ff, ffp), (ff, 2 * ff, ffp)), w_in.shape[0]),
            _CastJob(w_out, ((0, w_out.shape[1], w_out.shape[1]),), ffp))


def kernel(x, c, w_ada, b_ada, norm_ffn1, w_ffn1_in, w_ffn1_out, norm_mix, w_in, conv_w, conv_b, dt_bias, a_log, d_ssd, ssd_norm_w, w_a_proj, s5_lambda_re, s5_lambda_im, s5_b_re, s5_b_im, s5_c_re, s5_c_im, s5_d, s5_log_dt, w_b_glu, w_out, norm_ffn2, w_ffn2_in, w_ffn2_out, norm_final):
    bsz, seq, d = x.shape
    depth = w_ada.shape[0]
    m = bsz * seq
    d_inner = ssd_norm_w.shape[1]
    conv_dim = conv_w.shape[2]
    heads = dt_bias.shape[1]
    s5_width = w_b_glu.shape[1]
    off_xbc = d_inner
    off_dt = off_xbc + conv_dim
    off_u = off_dt + heads
    off_g = off_u + s5_width

    assert depth == 1, "the epilogue fusion below is written for a single layer"
    l = 0
    x2 = x.reshape(m, d)
    mods3 = _mods(c, w_ada[l], b_ada[l]).reshape(bsz, N_ADA, d)
    h = _prenorm(x2, norm_ffn1[l], mods3, 0, seq)

    whole = lambda w: ((0, w.shape[1], w.shape[1]),)
    jobs = (*_ffn_cast_jobs(w_ffn2_in[l], w_ffn2_out[l]),
            _CastJob(w_a_proj[l], whole(w_a_proj[l]), d_inner),
            _CastJob(w_out[l], whole(w_out[l]), d),
            _CastJob(w_b_glu[l], whole(w_b_glu[l]), s5_width),
            _CastJob(w_in[l].T, ((0, d, d),), w_in.shape[2]))
    wa1, wb1, wo1, *prep = _cast_weights(
        _ffn_cast_jobs(w_ffn1_in[l], w_ffn1_out[l]),
        (s5_lambda_re[l], s5_lambda_im[l], s5_log_dt[l], s5_b_re[l], s5_b_im[l], s5_c_re[l], s5_c_im[l]))
    x2, h, wa2, wb2, wo2, w_a, w_o, w_glu, wi = _ffn(
        h, x2, wa1, wb1, wo1, mods3, norm_mix[l], 2, 3, seq, jobs)

    off_bc = off_xbc + d_inner
    zs = _proj(h, wi, 0, d_inner, 1024, act=_silu)
    xs = _proj_conv(h, wi, off_xbc, conv_w[l][:, :d_inner], conv_b[l][:d_inner], 1024, seq, F32)
    bc, dtp = _proj_conv(h, wi, off_bc, conv_w[l][:, d_inner:], conv_b[l][d_inner:], 1024, seq, BF16,
                         plain=LANES)
    u = _proj_chunked(h, wi, off_u, s5_width)
    gates = _proj(h, wi, off_g, 2 * d, 1024, act=_sigmoid, out_dtype=BF16)

    ya = _ssd(xs, bc, dtp, zs, dt_bias[l], a_log[l], d_ssd[l], ssd_norm_w[l], bsz, seq)
    yb = _s5(u, prep, s5_d[l], bsz, seq)

    x2, h = _merge(ya, yb, gates, x2, w_a, w_glu.reshape(yb.shape[0], yb.shape[2], 2 * d), w_o,
                   mods3, norm_ffn2[l], 5, 6, seq)

    (x2,) = _ffn(h, x2, wa2, wb2, wo2, mods3, norm_final, 8, None, seq)
    return x2.reshape(bsz, seq, d)
```

```python
import functools
import math
from typing import NamedTuple

import jax
import jax.numpy as jnp
from jax import lax
from jax.experimental import pallas as pl
from jax.experimental.pallas import tpu as pltpu

F32 = jnp.float32
BF16 = jnp.bfloat16
EPS = 1e-6

VMEM_LIMIT_BYTES = 56 * 1024 * 1024
LANES = 128
SUBLANES = 8
BF16_ROWS = 16

SSD_HEAD_DIM = 64
SSD_GROUPS = 4
SSD_STATE = 128
SSD_CONV = 4
SSD_CHUNK = 128
S5_GROUP_SIZE = 16
S5_STATE = 64
MXU_WIDTH = 256
S5_SLAB = 128
S5_GROUPS_PER_SLAB = S5_SLAB // S5_GROUP_SIZE
S5_SLAB_STATE = S5_GROUPS_PER_SLAB * S5_STATE
S5_CHUNK = 16
N_ADA = 9


def _params(*sem):
    return pltpu.CompilerParams(dimension_semantics=sem, vmem_limit_bytes=VMEM_LIMIT_BYTES)


def _sigmoid(v):
    return 0.5 * jnp.tanh(0.5 * v) + 0.5


def _silu(v):
    return v * _sigmoid(v)


def _dot(a, b):
    return jnp.dot(a, b, preferred_element_type=F32)


def _dot_nt(a, b):
    return lax.dot_general(a, b, (((1,), (1,)), ((), ())), preferred_element_type=F32)


def _mod_rms(x, nw, shift, scale):
    ms = jnp.mean(x * x, axis=-1, keepdims=True)
    return (x * lax.rsqrt(ms + EPS) * nw) * (1.0 + scale) + shift


def _mods_kernel(ct_ref, w_ref, b_ref, o_ref):
    ca = _silu(ct_ref[...])
    w = w_ref[...]
    rows = [jnp.sum(ca[:, b:b + 1] * w, axis=0, keepdims=True) for b in range(ca.shape[1])]
    o_ref[...] = jnp.concatenate(rows, axis=0) + b_ref[...]


def _mods(c, w_ada, b_ada):
    bsz, d = c.shape
    n = w_ada.shape[1]
    tn = 1024
    return pl.pallas_call(
        _mods_kernel,
        out_shape=jax.ShapeDtypeStruct((bsz, n), F32),
        grid=(n // tn,),
        in_specs=[pl.BlockSpec((d, bsz), lambda j: (0, 0)),
                  pl.BlockSpec((d, tn), lambda j: (0, j)),
                  pl.BlockSpec((1, tn), lambda j: (0, j))],
        out_specs=pl.BlockSpec((bsz, tn), lambda j: (0, j)),
        compiler_params=_params("arbitrary"),
        name="mods",
    )(c.T, w_ada, b_ada.reshape(1, n))


def _prenorm_kernel(x_ref, nw_ref, mods_ref, h_ref, *, k):
    h = _mod_rms(x_ref[...], nw_ref[...], mods_ref[k:k + 1, :], mods_ref[k + 1:k + 2, :])
    h_ref[...] = h.astype(h_ref.dtype)


def _prenorm(x2, nw, mods3, k, seq):
    m, d = x2.shape
    tm = min(1024, seq)
    per_b = seq // tm
    return pl.pallas_call(
        functools.partial(_prenorm_kernel, k=k),
        out_shape=jax.ShapeDtypeStruct((m, d), BF16),
        grid=(m // tm,),
        in_specs=[pl.BlockSpec((tm, d), lambda i: (i, 0)),
                  pl.BlockSpec((1, d), lambda i: (0, 0)),
                  pl.BlockSpec((None, N_ADA, d), lambda i: (i // per_b, 0, 0))],
        out_specs=pl.BlockSpec((tm, d), lambda i: (i, 0)),
        compiler_params=_params("arbitrary"),
        name="prenorm",
    )(x2, nw.reshape(1, d), mods3)


FFN_ROWS = 1024
FFN_HIDDEN = 512
FFN_EPI_CHUNKS = 8
FFN_SUB_BLOCKS = 2


class _CastJob(NamedTuple):
    src: jax.Array
    cols: tuple
    out_rows: int

    def chunk_rows(self, slots):
        need = -(-self.out_rows // slots)
        rows = next(r for r in range(BF16_ROWS, self.out_rows + 1, BF16_ROWS)
                    if r >= need and self.out_rows % r == 0)
        return rows


def _cast_chunk(job, rows, chunk, src_ref, dst_refs):
    src_rows = job.src.shape[0]
    out_chunk = jnp.minimum(chunk, job.out_rows // rows - 1)
    partial = src_rows % rows != 0 or job.out_rows != src_rows
    if partial:
        r = lax.broadcasted_iota(jnp.int32, (rows, 1), 0) + out_chunk * rows
        keep = r < src_rows
    for (c0, c1, width), dst in zip(job.cols, dst_refs):
        v = src_ref[:, c0:c1]
        if partial:
            v = jnp.where(keep, v, 0.0)
        dst[:, :c1 - c0] = v.astype(BF16)
        if width > c1 - c0:
            dst[:, c1 - c0:] = jnp.zeros((rows, width - (c1 - c0)), BF16)


def _ffn_kernel(h_ref, x_ref, wa_ref, wb_ref, wo_ref, mods_ref, nw_ref, *refs, k_gate, k_next, n_tiles,
                jobs):
    n_job_out = sum(len(job.cols) for job, _ in jobs)
    job_src = refs[:len(jobs)]
    refs = refs[len(jobs):]
    acc = refs[-1]
    out_refs = refs[:len(refs) - 1 - n_job_out]
    job_dst = refs[len(out_refs):-1]
    i = pl.program_id(0)
    j = pl.program_id(1)
    slot = i % 2

    def casts():
        chunk = i * FFN_EPI_CHUNKS + jnp.minimum(j, FFN_EPI_CHUNKS - 1)
        k = 0
        for (job, rows), src_ref in zip(jobs, job_src):
            _cast_chunk(job, rows, chunk, src_ref, job_dst[k:k + len(job.cols)])
            k += len(job.cols)

    @pl.when((i == 0) & (j == 0))
    def _():
        acc[...] = jnp.zeros_like(acc)

    def epilogue():
        rows = x_ref.shape[0]
        r0 = pl.multiple_of(jnp.minimum(j, FFN_EPI_CHUNKS - 1) * rows, rows)
        done = acc[1 - slot, pl.ds(r0, rows), :]
        xn = x_ref[...] + 0.5 * mods_ref[k_gate:k_gate + 1, :] * done
        if k_next is None:
            ms = jnp.mean(xn * xn, axis=-1, keepdims=True)
            out_refs[0][...] = xn * lax.rsqrt(ms + EPS) * nw_ref[...]
        else:
            out_refs[0][...] = xn
            hn = _mod_rms(xn, nw_ref[...], mods_ref[k_next:k_next + 1, :],
                          mods_ref[k_next + 1:k_next + 2, :])
            out_refs[1][...] = hn.astype(BF16)

    def matmuls():
        sub = h_ref.shape[0] // FFN_SUB_BLOCKS
        for r in range(FFN_SUB_BLOCKS):
            rows = pl.ds(r * sub, sub)
            h = h_ref[rows, :]
            a = _dot(h, wa_ref[...])
            b = _dot(h, wb_ref[...])
            act = (_silu(a) * b).astype(BF16)
            acc[slot, rows, :] = jnp.where(j == 0, 0.0, acc[slot, rows, :]) + _dot(act, wo_ref[...])

    @pl.when(i == 0)
    def _():
        casts()
        matmuls()

    @pl.when((i > 0) & (i < n_tiles))
    def _():
        epilogue()
        casts()
        matmuls()

    @pl.when(i == n_tiles)
    def _():
        epilogue()


def _ffn(h, x2, wa, wb, wo, mods3, nw_next, k_gate, k_next, seq, cast_jobs=()):
    m, d = x2.shape
    ffp = wa.shape[1]
    tm = min(FFN_ROWS, seq)
    tf = FFN_HIDDEN
    n_tiles = m // tm
    n_hidden = ffp // tf
    ec = FFN_EPI_CHUNKS
    rows = tm // ec
    per_b = seq // tm
    assert n_hidden >= ec

    def done_chunk(i, j):
        return (jnp.where(i == 0, 0, (i - 1) * ec + jnp.minimum(j, ec - 1)), 0)

    def hidden(i, j):
        return jnp.where(i == n_tiles, n_hidden - 1, j)

    def cast_chunk(i, j):
        return jnp.where(i == n_tiles, n_tiles * ec - 1, i * ec + jnp.minimum(j, ec - 1))

    out_shape = [jax.ShapeDtypeStruct((m, d), F32)]
    out_specs = [pl.BlockSpec((rows, d), done_chunk)]
    if k_next is not None:
        out_shape.append(jax.ShapeDtypeStruct((m, d), BF16))
        out_specs.append(pl.BlockSpec((rows, d), done_chunk))
    in_specs = [pl.BlockSpec((tm, d), lambda i, j: (jnp.minimum(i, n_tiles - 1), 0)),
                pl.BlockSpec((rows, d), done_chunk),
                pl.BlockSpec((d, tf), lambda i, j: (0, hidden(i, j))),
                pl.BlockSpec((d, tf), lambda i, j: (0, hidden(i, j))),
                pl.BlockSpec((tf, d), lambda i, j: (hidden(i, j), 0)),
                pl.BlockSpec((None, N_ADA, d), lambda i, j: (jnp.maximum(i - 1, 0) // per_b, 0, 0)),
                pl.BlockSpec((1, d), lambda i, j: (0, 0))]
    sized_jobs = []
    for job in cast_jobs:
        r = job.chunk_rows(n_tiles * ec)
        sized_jobs.append((job._replace(src=jax.ShapeDtypeStruct(job.src.shape, F32)), r))
        last_out = job.out_rows // r - 1
        last_in = (job.src.shape[0] - 1) // r
        in_specs.append(pl.BlockSpec(
            (r, job.src.shape[1]),
            lambda i, j, lo=last_out, li=last_in: (jnp.minimum(jnp.minimum(cast_chunk(i, j), lo), li), 0)))
        for _, _, width in job.cols:
            out_shape.append(jax.ShapeDtypeStruct((job.out_rows, width), BF16))
            out_specs.append(pl.BlockSpec(
                (r, width), lambda i, j, lo=last_out: (jnp.minimum(cast_chunk(i, j), lo), 0)))
    return pl.pallas_call(
        functools.partial(_ffn_kernel, k_gate=k_gate, k_next=k_next, n_tiles=n_tiles,
                          jobs=tuple(sized_jobs)),
        out_shape=out_shape,
        grid=(n_tiles + 1, n_hidden),
        in_specs=in_specs,
        out_specs=out_specs,
        scratch_shapes=[pltpu.VMEM((2, tm, d), F32)],
        compiler_params=_params("arbitrary", "arbitrary"),
        name="ffn",
    )(h, x2, wa, wb, wo, mods3, nw_next.reshape(1, d), *[job.src for job in cast_jobs])


PROJ_ROWS = 1024
PROJ_SUB = 256


def _weight_rows(rows, d, row0):
    return pl.BlockSpec((pl.Element(rows), pl.Element(d)),
                        lambda *idx: (pl.multiple_of(row0(*idx), BF16_ROWS), 0))


def _proj_kernel(h_ref, w_ref, o_ref, *, act):
    y = _dot_nt(h_ref[...], w_ref[...])
    if act is not None:
        y = act(y)
    o_ref[...] = y.astype(o_ref.dtype)


def _proj(h, wt, col0, n, tn, act=None, out_dtype=F32):
    m, d = h.shape
    tm = min(2 * PROJ_ROWS, m)
    return pl.pallas_call(
        functools.partial(_proj_kernel, act=act),
        out_shape=jax.ShapeDtypeStruct((m, n), out_dtype),
        grid=(m // tm, n // tn),
        in_specs=[pl.BlockSpec((tm, d), lambda i, j: (i, 0)),
                  _weight_rows(tn, d, lambda i, j: col0 + j * tn)],
        out_specs=pl.BlockSpec((tm, tn), lambda i, j: (i, j)),
        compiler_params=_params("arbitrary", "arbitrary"),
        name="proj",
    )(h, wt)


def _proj_conv_kernel(h_ref, w_ref, cw_ref, cb_ref, o_ref, *refs, tiles_per_seq):
    plain_ref = refs[0] if len(refs) == 2 else None
    halo = refs[-1]
    tn = o_ref.shape[1]
    i = pl.program_id(0)
    j = pl.program_id(1)
    tm = h_ref.shape[0]
    sub = min(PROJ_SUB, tm)
    taps = SSD_CONV - 1

    @pl.when(i % tiles_per_seq == 0)
    def _():
        halo[j] = jnp.zeros(halo.shape[1:], F32)

    prev = halo[j]
    w = w_ref[...]
    cw = cw_ref[...]
    cb = cb_ref[...]
    for r in range(tm // sub):
        raw = _dot_nt(h_ref[r * sub:(r + 1) * sub, :], w)
        if plain_ref is not None:
            plain_ref[r * sub:(r + 1) * sub, :] = raw[:, tn:]
            raw = raw[:, :tn]
        ext = jnp.concatenate([prev, raw], axis=0)
        part = cw[0:1, :] * ext
        for k in range(1, taps):
            part = pltpu.roll(part, 1, axis=0) + cw[k:k + 1, :] * ext
        conv = pltpu.roll(part, 1, axis=0)[SUBLANES:, :] + (cb + cw[taps:taps + 1, :] * raw)
        o_ref[r * sub:(r + 1) * sub, :] = _silu(conv).astype(o_ref.dtype)
        prev = raw[sub - SUBLANES:, :]
    halo[j] = prev


def _proj_conv(h, wt, col0, conv_w, conv_b, tn, seq, out_dtype, plain=0):
    m, d = h.shape
    n = conv_w.shape[1]
    tm = min(PROJ_ROWS, seq)
    out_shape = [jax.ShapeDtypeStruct((m, n), out_dtype)]
    out_specs = [pl.BlockSpec((tm, tn), lambda i, j: (i, j))]
    if plain:
        assert n == tn
        out_shape.append(jax.ShapeDtypeStruct((m, plain), F32))
        out_specs.append(pl.BlockSpec((tm, plain), lambda i, j: (i, 0)))
    w_spec = _weight_rows(tn + plain, d, lambda i, j: col0 + j * tn)
    out = pl.pallas_call(
        functools.partial(_proj_conv_kernel, tiles_per_seq=seq // tm),
        out_shape=out_shape,
        grid=(m // tm, n // tn),
        in_specs=[pl.BlockSpec((tm, d), lambda i, j: (i, 0)),
                  w_spec,
                  pl.BlockSpec((SSD_CONV, tn), lambda i, j: (0, j)),
                  pl.BlockSpec((1, tn), lambda i, j: (0, j))],
        out_specs=out_specs,
        scratch_shapes=[pltpu.VMEM((n // tn, SUBLANES, tn), F32)],
        compiler_params=_params("arbitrary", "arbitrary"),
        name="projconv",
    )(h, wt, conv_w, conv_b.reshape(1, n))
    return out if plain else out[0]


def _proj_chunked_kernel(h_ref, w_ref, o_ref, scr):
    n_slabs, rows, lq = o_ref.shape
    q = S5_SLAB
    L = lq // q
    y = _dot_nt(h_ref[...], w_ref[...])
    for c in range(scr.shape[0]):
        scr[c] = y[:, c * LANES:(c + 1) * LANES]
    for c in range(scr.shape[0]):
        s, off = divmod(c * LANES, q)
        for l in range(L):
            o_ref[s, :, l * q + off:l * q + off + LANES] = scr[c, pl.ds(l, rows, stride=L), :]


def _proj_chunked(h, wt, row0, n):
    m, d = h.shape
    tm = min(PROJ_ROWS, m)
    L = S5_CHUNK
    n_slabs = n // S5_SLAB
    return pl.pallas_call(
        _proj_chunked_kernel,
        out_shape=jax.ShapeDtypeStruct((n_slabs, m // L, L * S5_SLAB), F32),
        grid=(m // tm,),
        in_specs=[pl.BlockSpec((tm, d), lambda i: (i, 0)),
                  _weight_rows(n, d, lambda i: row0)],
        out_specs=pl.BlockSpec((n_slabs, tm // L, L * S5_SLAB), lambda i: (0, i, 0)),
        scratch_shapes=[pltpu.VMEM((n // LANES, tm, LANES), F32)],
        compiler_params=_params("arbitrary"),
        name="projchunk",
    )(h, wt)


def _split3(v):
    hi = v.astype(BF16)
    r = v - hi.astype(F32)
    mid = r.astype(BF16)
    lo = (r - mid.astype(F32)).astype(BF16)
    return hi, mid, lo


def _pair(v, j, first):
    return jnp.where(first, v[:, 2 * j:2 * j + 1], v[:, 2 * j + 1:2 * j + 2])


def _ssd_kernel(xs_ref, bc_ref, dt_ref, zs_ref, dtb_ref, alog_ref, dsk_ref, nw_ref, y_ref, state):
    L = SSD_CHUNK
    d_inner = xs_ref.shape[1]
    n_state = SSD_STATE
    gw = d_inner // SSD_GROUPS
    pairs_per_group = gw // LANES

    @pl.when(pl.program_id(1) == 0)
    def _():
        state[...] = jnp.zeros_like(state)

    dtr = dt_ref[...] + dtb_ref[...]
    dt = jnp.maximum(dtr, 0.0) + jnp.log1p(jnp.exp(-jnp.abs(dtr)))
    da = dt * (-jnp.exp(alog_ref[...]))
    row = lax.broadcasted_iota(jnp.int32, (L, L), 0)
    col = lax.broadcasted_iota(jnp.int32, (L, L), 1)
    causal = row >= col
    tril = jnp.where(causal, 1.0, 0.0).astype(BF16)
    cs = sum(_dot(tril, part) for part in _split3(da))
    cs_last = cs[L - 1:L, :]
    ecs = jnp.exp(cs)
    ecl = jnp.exp(cs_last)
    cs_t = cs.T
    dt_t = dt.T
    wst_t = (dt * jnp.exp(cs_last - cs)).T

    lane = lax.broadcasted_iota(jnp.int32, (L, LANES), 1)
    first = lane < SSD_HEAD_DIM
    first_row = first[0:1, :]

    y_parts = []
    for g in range(SSD_GROUPS):
        bm_g = bc_ref[:, g * n_state:(g + 1) * n_state]
        cm_g = bc_ref[:, (SSD_GROUPS + g) * n_state:(SSD_GROUPS + g + 1) * n_state]
        cb = _dot_nt(cm_g, bm_g)
        bm_t = bm_g.astype(F32).T
        y_off = _dot(cm_g, state[g].astype(BF16))
        for jj in range(pairs_per_group):
            j = g * pairs_per_group + jj
            h0, h1 = 2 * j, 2 * j + 1
            cols = slice(j * LANES, (j + 1) * LANES)
            gcols = slice(jj * LANES, (jj + 1) * LANES)
            xs_p = xs_ref[:, cols]
            xb = xs_p.astype(BF16)
            zero = jnp.zeros_like(xb)
            rhs = jnp.concatenate([jnp.where(first, xb, zero), jnp.where(first, zero, xb)], axis=0)
            lhs_y = []
            lhs_s = []
            for h in (h0, h1):
                dec = jnp.exp(jnp.where(causal, cs[:, h:h + 1] - cs_t[h:h + 1, :], -jnp.inf))
                lhs_y.append((cb * dec * dt_t[h:h + 1, :]).astype(BF16))
                lhs_s.append((bm_t * wst_t[h:h + 1, :]).astype(BF16))
            y_p = (_dot(jnp.concatenate(lhs_y, axis=1), rhs)
                   + _pair(ecs, j, first) * y_off[:, gcols]
                   + dsk_ref[:, cols] * xs_p)
            y_parts.append(y_p)
            st_new = _dot(jnp.concatenate(lhs_s, axis=1), rhs)
            state[g, :, gcols] = _pair(ecl, j, first_row) * state[g, :, gcols] + st_new

    yz = jnp.concatenate(y_parts, axis=1) * zs_ref[...]
    outs = []
    for g in range(SSD_GROUPS):
        seg = yz[:, g * gw:(g + 1) * gw]
        ms = jnp.mean(seg * seg, axis=-1, keepdims=True)
        outs.append(seg * lax.rsqrt(ms + EPS))
    y_ref[...] = (jnp.concatenate(outs, axis=1) * nw_ref[...]).astype(y_ref.dtype)


def _ssd(xs, bc, dtp, zs, dt_bias, a_log, d_ssd, norm_w, bsz, seq):
    m, d_inner = xs.shape
    L = SSD_CHUNK
    nc = seq // L
    heads = d_inner // SSD_HEAD_DIM
    pad = LANES - heads
    row = lambda b, c: (b * nc + c, 0)
    const = lambda b, c: (0, 0)
    return pl.pallas_call(
        _ssd_kernel,
        out_shape=jax.ShapeDtypeStruct((m, d_inner), BF16),
        grid=(bsz, nc),
        in_specs=[pl.BlockSpec((L, d_inner), row),
                  pl.BlockSpec((L, bc.shape[1]), row),
                  pl.BlockSpec((L, LANES), row),
                  pl.BlockSpec((L, d_inner), row),
                  pl.BlockSpec((1, LANES), const),
                  pl.BlockSpec((1, LANES), const),
                  pl.BlockSpec((1, d_inner), const),
                  pl.BlockSpec((1, d_inner), const)],
        out_specs=pl.BlockSpec((L, d_inner), row),
        scratch_shapes=[pltpu.VMEM((SSD_GROUPS, SSD_STATE, d_inner // SSD_GROUPS), F32)],
        compiler_params=_params("arbitrary", "arbitrary"),
        name="ssd",
    )(xs, bc, dtp, zs,
      jnp.pad(dt_bias, (0, pad)).reshape(1, LANES), jnp.pad(a_log, (0, pad)).reshape(1, LANES),
      jnp.repeat(d_ssd, SSD_HEAD_DIM).reshape(1, d_inner), norm_w.reshape(1, d_inner))


def _s5_prep_kernel(lre_ref, lim_ref, ldt_ref, btr_ref, bti_ref, cnr_ref, cni_ref,
                    wend_ref, wcorr_ref, krev_ref, al_ref):
    L = S5_CHUNK
    q = S5_SLAB
    ns = S5_SLAB_STATE
    lr = jnp.minimum(lre_ref[...], -1e-4)
    li = lim_ref[...]
    dt = jnp.exp(ldt_ref[...])

    def power(k):
        mag = jnp.exp(lr * dt * float(k))
        ang = li * dt * float(k)
        return mag * jnp.cos(ang), mag * jnp.sin(ang)

    ar, ai = power(1)
    den = lr * lr + li * li
    nr = ar - 1.0
    kr = (nr * lr + ai * li) / den
    ki = (ai * lr - nr * li) / den

    rows = lax.broadcasted_iota(jnp.int32, (q, ns), 0) // S5_GROUP_SIZE
    cols = lax.broadcasted_iota(jnp.int32, (q, ns), 1) // S5_STATE
    same = rows == cols

    def block_diag(ref):
        return jnp.where(same, jnp.concatenate([ref[...]] * (ns // LANES), axis=1), 0.0)

    btr, bti = block_diag(btr_ref), block_diag(bti_ref)
    bbr = btr * kr - bti * ki
    bbi = btr * ki + bti * kr
    cr, ci = block_diag(cnr_ref), block_diag(cni_ref)

    for k in range(L):
        pr, pi = power(k)
        blk = jnp.concatenate([bbr * pr - bbi * pi, bbr * pi + bbi * pr], axis=1)
        wend_ref[(L - 1 - k) * q:(L - k) * q, :] = blk.astype(BF16)
    for l in range(L):
        pr, pi = power(l + 1)
        blk = jnp.concatenate([cr * pr - ci * pi, -(cr * pi) - ci * pr], axis=1)
        wcorr_ref[l * q:(l + 1) * q, :] = blk.astype(BF16)
    cstack = jnp.concatenate([cr, -ci], axis=1).astype(BF16)
    kw = _dot_nt(wend_ref[...], cstack)
    tb = MXU_WIDTH // q
    for b in range(tb):
        up = (tb - 1 - b) * q
        shifted = kw if up == 0 else jnp.concatenate([kw[up:], jnp.zeros((up, q), F32)], axis=0)
        krev_ref[:, b * q:(b + 1) * q] = shifted.astype(BF16)
    pr, pi = power(L)
    al_ref[...] = jnp.concatenate([pr, pi], axis=1)


def _s5_prep_call(lam_re, lam_im, log_dt, b_re, b_im, c_re, c_im, slab_of):
    g, p = lam_re.shape
    width = g * S5_GROUP_SIZE
    n_slabs = width // S5_SLAB
    ns = S5_SLAB_STATE
    lq = S5_CHUNK * S5_SLAB
    row_vec = lambda v: v.reshape(1, g * p)
    twice = lambda v: jnp.concatenate([v, v], axis=1)
    bt = lambda v: twice(jnp.swapaxes(v, 1, 2).reshape(width, p))
    cn = lambda v: twice(v.reshape(width, p))
    vec = pl.BlockSpec((None, 1, ns), lambda c: (slab_of(c), 0, 0))
    mat = pl.BlockSpec((S5_SLAB, LANES), lambda c: (slab_of(c), 0))
    vec3 = lambda v: row_vec(v).reshape(n_slabs, 1, ns)
    slab3 = lambda rows, cols: pl.BlockSpec((None, rows, cols), lambda c: (slab_of(c), 0, 0))
    operands = [vec3(lam_re), vec3(lam_im), vec3(jnp.repeat(log_dt, p)),
                bt(b_re), bt(b_im), cn(c_re), cn(c_im)]
    out_shape = [jax.ShapeDtypeStruct((n_slabs, lq, 2 * ns), BF16),
                 jax.ShapeDtypeStruct((n_slabs, lq, 2 * ns), BF16),
                 jax.ShapeDtypeStruct((n_slabs, lq, MXU_WIDTH), BF16),
                 jax.ShapeDtypeStruct((n_slabs, 1, 2 * ns), F32)]
    out_specs = [slab3(lq, 2 * ns), slab3(lq, 2 * ns), slab3(lq, MXU_WIDTH), slab3(1, 2 * ns)]
    return operands, [vec, vec, vec, mat, mat, mat, mat], out_shape, out_specs, n_slabs


def _gelu_tanh(v):
    return 0.5 * v * (1.0 + jnp.tanh(math.sqrt(2.0 / math.pi) * (v + 0.044715 * (v * v * v))))


def _s5_kernel(u_ref, wend_ref, wcorr_ref, krev_ref, al_ref, d_ref, y_ref, e_scr, cin_scr, st_scr,
               tok_scr, *, blocks_per_seq):
    L = S5_CHUNK
    q = S5_SLAB
    ns = S5_SLAB_STATE
    rows = u_ref.shape[0]

    @pl.when(pl.program_id(1) % blocks_per_seq == 0)
    def _():
        st_scr[...] = jnp.zeros_like(st_scr)

    u = u_ref[...]
    ub = u.astype(BF16)
    e_scr[...] = _dot(ub, wend_ref[...])
    alr = al_ref[:, :ns]
    ali = al_ref[:, ns:]

    def tile_step(t, st):
        base = pl.multiple_of(t * SUBLANES, SUBLANES)
        e = e_scr[pl.ds(base, SUBLANES), :]
        sr, si = st
        carried = []
        for r in range(SUBLANES):
            carried.append(jnp.concatenate([sr, si], axis=1))
            er = e[r:r + 1, :ns]
            ei = e[r:r + 1, ns:]
            sr, si = alr * sr - ali * si + er, alr * si + ali * sr + ei
        cin_scr[pl.ds(base, SUBLANES), :] = jnp.concatenate(carried, axis=0)
        return sr, si

    sr, si = lax.fori_loop(0, rows // SUBLANES, tile_step, (st_scr[:, :ns], st_scr[:, ns:]))
    st_scr[...] = jnp.concatenate([sr, si], axis=1)

    corr = _dot_nt(cin_scr[...].astype(BF16), wcorr_ref[...])
    w = MXU_WIDTH
    nb = L * q // w
    intra = [_dot(ub[:, :(b + 1) * w], krev_ref[(nb - 1 - b) * w:, :]) for b in range(nb)]
    y = _gelu_tanh(jnp.concatenate(intra, axis=1) + corr + d_ref[...] * u)
    for c in range(tok_scr.shape[0]):
        for l in range(L):
            lo = l * q + c * LANES
            tok_scr[c, pl.ds(l, rows, stride=L), :] = y[:, lo:lo + LANES]
        y_ref[:, c * LANES:(c + 1) * LANES] = tok_scr[c].astype(y_ref.dtype)


def _s5(u2, prep, d_skip, bsz, seq):
    wend, wcorr, krev, al = prep
    n_slabs, chunks, lq = u2.shape
    L = S5_CHUNK
    q = S5_SLAB
    ns2 = 2 * S5_SLAB_STATE
    chunks_per_seq = seq // L
    rb = min(256, chunks_per_seq)
    blocks_per_seq = chunks_per_seq // rb
    per_w = MXU_WIDTH // q
    d_t = jnp.tile(d_skip.reshape(n_slabs, 1, q), (1, 1, L))
    slab = lambda s, r: (s, 0, 0)
    blk = lambda s, r: (s, r, 0)
    return pl.pallas_call(
        functools.partial(_s5_kernel, blocks_per_seq=blocks_per_seq),
        out_shape=jax.ShapeDtypeStruct((n_slabs // per_w, chunks * L, MXU_WIDTH), BF16),
        grid=(n_slabs, chunks // rb),
        in_specs=[pl.BlockSpec((None, rb, lq), blk),
                  pl.BlockSpec((None, lq, ns2), slab),
                  pl.BlockSpec((None, lq, ns2), slab),
                  pl.BlockSpec((None, lq, MXU_WIDTH), slab),
                  pl.BlockSpec((None, 1, ns2), slab),
                  pl.BlockSpec((None, 1, lq), slab)],
        out_specs=pl.BlockSpec((None, rb * L, q), lambda s, r: (s // per_w, r, s % per_w)),
        scratch_shapes=[pltpu.VMEM((rb, ns2), F32), pltpu.VMEM((rb, ns2), F32),
                        pltpu.VMEM((1, ns2), F32), pltpu.VMEM((q // LANES, rb * L, LANES), F32)],
        compiler_params=_params("arbitrary", "arbitrary"),
        name="s5",
    )(u2, wend, wcorr, krev, al, d_t)


def _merge_kernel(ya_ref, yb_ref, ga_ref, gb_ref, x_ref, wa_ref, wg_ref, wo_ref, mods_ref, nw_ref,
                  xo_ref, h_ref, *, k_gate, k_next):
    d = x_ref.shape[1]
    p_a = _dot(ya_ref[...], wa_ref[...])
    glu = sum(_dot(yb_ref[s], wg_ref[s]) for s in range(yb_ref.shape[0]))
    p_b = glu[:, :d] * _sigmoid(glu[:, d:])
    merged = ga_ref[...] * p_a + gb_ref[...] * p_b
    xn = x_ref[...] + mods_ref[k_gate:k_gate + 1, :] * _dot(merged.astype(BF16), wo_ref[...])
    xo_ref[...] = xn
    hn = _mod_rms(xn, nw_ref[...], mods_ref[k_next:k_next + 1, :], mods_ref[k_next + 1:k_next + 2, :])
    h_ref[...] = hn.astype(BF16)


def _merge(ya, yb_slabs, gates, x2, w_a, w_glu, w_o, mods3, nw_next, k_gate, k_next, seq):
    m, d = x2.shape
    n_slabs, _, q = yb_slabs.shape
    tm = min(256, seq)
    per_b = seq // tm
    row = lambda i: (i, 0)
    const = lambda i: (0, 0)
    once = pl.Buffered(1)
    return pl.pallas_call(
        functools.partial(_merge_kernel, k_gate=k_gate, k_next=k_next),
        out_shape=[jax.ShapeDtypeStruct((m, d), F32), jax.ShapeDtypeStruct((m, d), BF16)],
        grid=(m // tm,),
        in_specs=[pl.BlockSpec((tm, ya.shape[1]), row),
                  pl.BlockSpec((n_slabs, tm, q), lambda i: (0, i, 0)),
                  pl.BlockSpec((tm, d), row),
                  pl.BlockSpec((tm, d), lambda i: (i, 1)),
                  pl.BlockSpec((tm, d), row),
                  pl.BlockSpec(w_a.shape, const, pipeline_mode=once),
                  pl.BlockSpec(w_glu.shape, lambda i: (0, 0, 0), pipeline_mode=once),
                  pl.BlockSpec(w_o.shape, const, pipeline_mode=once),
                  pl.BlockSpec((None, N_ADA, d), lambda i: (i // per_b, 0, 0)),
                  pl.BlockSpec((1, d), const)],
        out_specs=[pl.BlockSpec((tm, d), row), pl.BlockSpec((tm, d), row)],
        compiler_params=_params("arbitrary"),
        name="merge",
    )(ya, yb_slabs, gates, gates, x2, w_a, w_glu, w_o, mods3, nw_next.reshape(1, d))


CAST_STEPS = 16


S5_PREP_INPUTS = 7


def _cast_kernel(*refs, jobs, s5_slabs):
    chunk = pl.program_id(0)
    n_in = len(jobs) + (S5_PREP_INPUTS if s5_slabs else 0)
    n_dst = sum(len(job.cols) for job, _ in jobs)
    src, s5_in = refs[:len(jobs)], refs[len(jobs):n_in]
    dst, s5_out = refs[n_in:n_in + n_dst], refs[n_in + n_dst:]
    k = 0
    for (job, rows), src_ref in zip(jobs, src):
        _cast_chunk(job, rows, chunk, src_ref, dst[k:k + len(job.cols)])
        k += len(job.cols)
    if s5_slabs:
        @pl.when(chunk < s5_slabs)
        def _():
            _s5_prep_kernel(*s5_in, *s5_out)


def _cast_weights(cast_jobs, s5_params=None):
    in_specs, out_specs, out_shape, sized = [], [], [], []
    for job in cast_jobs:
        r = job.chunk_rows(CAST_STEPS)
        sized.append((job._replace(src=jax.ShapeDtypeStruct(job.src.shape, F32)), r))
        last_out = job.out_rows // r - 1
        last_in = (job.src.shape[0] - 1) // r
        in_specs.append(pl.BlockSpec(
            (r, job.src.shape[1]), lambda c, lo=last_out, li=last_in: (jnp.minimum(jnp.minimum(c, lo), li), 0)))
        for _, _, width in job.cols:
            out_shape.append(jax.ShapeDtypeStruct((job.out_rows, width), BF16))
            out_specs.append(pl.BlockSpec((r, width), lambda c, lo=last_out: (jnp.minimum(c, lo), 0)))
    operands = [job.src for job in cast_jobs]
    s5_slabs = 0
    if s5_params is not None:
        width = s5_params[0].shape[0] * S5_GROUP_SIZE
        last = width // S5_SLAB - 1
        ops, ins, shapes, outs, s5_slabs = _s5_prep_call(*s5_params, lambda c: jnp.minimum(c, last))
        assert s5_slabs <= CAST_STEPS and len(ops) == S5_PREP_INPUTS
        operands += ops
        in_specs += ins
        out_shape += shapes
        out_specs += outs
    return pl.pallas_call(
        functools.partial(_cast_kernel, jobs=tuple(sized), s5_slabs=s5_slabs),
        out_shape=out_shape,
        grid=(CAST_STEPS,),
        in_specs=in_specs,
        out_specs=out_specs,
        compiler_params=_params("arbitrary"),
        name="castw",
    )(*operands)


def _ffn_cast_jobs(w_in, w_out):
    ff = w_out.shape[0]
    ffp = -(-ff // FFN_HIDDEN) * FFN_HIDDEN
    return (_CastJob(w_in, ((0, ff, ffp), (ff, 2 * ff, ffp)), w_in.shape[0]),
            _CastJob(w_out, ((0, w_out.shape[1], w_out.shape[1]),), ffp))


def kernel(x, c, w_ada, b_ada, norm_ffn1, w_ffn1_in, w_ffn1_out, norm_mix, w_in, conv_w, conv_b, dt_bias, a_log, d_ssd, ssd_norm_w, w_a_proj, s5_lambda_re, s5_lambda_im, s5_b_re, s5_b_im, s5_c_re, s5_c_im, s5_d, s5_log_dt, w_b_glu, w_out, norm_ffn2, w_ffn2_in, w_ffn2_out, norm_final):
    bsz, seq, d = x.shape
    depth = w_ada.shape[0]
    m = bsz * seq
    d_inner = ssd_norm_w.shape[1]
    conv_dim = conv_w.shape[2]
    heads = dt_bias.shape[1]
    s5_width = w_b_glu.shape[1]
    off_xbc = d_inner
    off_dt = off_xbc + conv_dim
    off_u = off_dt + heads
    off_g = off_u + s5_width

    assert depth == 1, "the epilogue fusion below is written for a single layer"
    l = 0
    x2 = x.reshape(m, d)
    mods3 = _mods(c, w_ada[l], b_ada[l]).reshape(bsz, N_ADA, d)
    h = _prenorm(x2, norm_ffn1[l], mods3, 0, seq)

    whole = lambda w: ((0, w.shape[1], w.shape[1]),)
    jobs = (*_ffn_cast_jobs(w_ffn2_in[l], w_ffn2_out[l]),
            _CastJob(w_a_proj[l], whole(w_a_proj[l]), d_inner),
            _CastJob(w_out[l], whole(w_out[l]), d),
            _CastJob(w_b_glu[l], whole(w_b_glu[l]), s5_width),
            _CastJob(w_in[l].T, ((0, d, d),), w_in.shape[2]))
    wa1, wb1, wo1, *prep = _cast_weights(
        _ffn_cast_jobs(w_ffn1_in[l], w_ffn1_out[l]),
        (s5_lambda_re[l], s5_lambda_im[l], s5_log_dt[l], s5_b_re[l], s5_b_im[l], s5_c_re[l], s5_c_im[l]))
    x2, h, wa2, wb2, wo2, w_a, w_o, w_glu, wi = _ffn(
        h, x2, wa1, wb1, wo1, mods3, norm_mix[l], 2, 3, seq, jobs)

    off_bc = off_xbc + d_inner
    zs = _proj(h, wi, 0, d_inner, 1024, act=_silu, out_dtype=BF16)
    xs = _proj_conv(h, wi, off_xbc, conv_w[l][:, :d_inner], conv_b[l][:d_inner], 1024, seq, BF16)
    bc, dtp = _proj_conv(h, wi, off_bc, conv_w[l][:, d_inner:], conv_b[l][d_inner:], 1024, seq, BF16,
                         plain=LANES)
    u = _proj_chunked(h, wi, off_u, s5_width)
    gates = _proj(h, wi, off_g, 2 * d, 1024, act=_sigmoid, out_dtype=BF16)

    ya = _ssd(xs, bc, dtp, zs, dt_bias[l], a_log[l], d_ssd[l], ssd_norm_w[l], bsz, seq)
    yb = _s5(u, prep, s5_d[l], bsz, seq)

    x2, h = _merge(ya, yb, gates, x2, w_a, w_glu.reshape(yb.shape[0], yb.shape[2], 2 * d), w_o,
                   mods3, norm_ffn2[l], 5, 6, seq)

    (x2,) = _ffn(h, x2, wa2, wb2, wo2, mods3, norm_final, 8, None, seq)
    return x2.reshape(bsz, seq, d)
```

```python
import functools
import math
from typing import NamedTuple

import jax
import jax.numpy as jnp
from jax import lax
from jax.experimental import pallas as pl
from jax.experimental.pallas import tpu as pltpu

F32 = jnp.float32
BF16 = jnp.bfloat16
EPS = 1e-6

VMEM_LIMIT_BYTES = 56 * 1024 * 1024
LANES = 128
SUBLANES = 8
BF16_ROWS = 16

SSD_HEAD_DIM = 64
SSD_GROUPS = 4
SSD_STATE = 128
SSD_CONV = 4
SSD_CHUNK = 128
S5_GROUP_SIZE = 16
S5_STATE = 64
MXU_WIDTH = 256
S5_SLAB = 128
S5_GROUPS_PER_SLAB = S5_SLAB // S5_GROUP_SIZE
S5_SLAB_STATE = S5_GROUPS_PER_SLAB * S5_STATE
S5_CHUNK = 16
N_ADA = 9


def _params(*sem):
    return pltpu.CompilerParams(dimension_semantics=sem, vmem_limit_bytes=VMEM_LIMIT_BYTES)


def _sigmoid(v):
    return 0.5 * jnp.tanh(0.5 * v) + 0.5


def _silu(v):
    return v * _sigmoid(v)


def _dot(a, b):
    return jnp.dot(a, b, preferred_element_type=F32)


def _dot_nt(a, b):
    return lax.dot_general(a, b, (((1,), (1,)), ((), ())), preferred_element_type=F32)


def _mod_rms(x, nw, shift, scale):
    ms = jnp.mean(x * x, axis=-1, keepdims=True)
    return (x * lax.rsqrt(ms + EPS) * nw) * (1.0 + scale) + shift


def _mods_kernel(ct_ref, w_ref, b_ref, o_ref):
    ca = _silu(ct_ref[...])
    w = w_ref[...]
    rows = [jnp.sum(ca[:, b:b + 1] * w, axis=0, keepdims=True) for b in range(ca.shape[1])]
    o_ref[...] = jnp.concatenate(rows, axis=0) + b_ref[...]


def _mods(c, w_ada, b_ada):
    bsz, d = c.shape
    n = w_ada.shape[1]
    tn = 1024
    return pl.pallas_call(
        _mods_kernel,
        out_shape=jax.ShapeDtypeStruct((bsz, n), F32),
        grid=(n // tn,),
        in_specs=[pl.BlockSpec((d, bsz), lambda j: (0, 0)),
                  pl.BlockSpec((d, tn), lambda j: (0, j)),
                  pl.BlockSpec((1, tn), lambda j: (0, j))],
        out_specs=pl.BlockSpec((bsz, tn), lambda j: (0, j)),
        compiler_params=_params("arbitrary"),
        name="mods",
    )(c.T, w_ada, b_ada.reshape(1, n))


def _prenorm_kernel(x_ref, nw_ref, mods_ref, h_ref, *, k):
    h = _mod_rms(x_ref[...], nw_ref[...], mods_ref[k:k + 1, :], mods_ref[k + 1:k + 2, :])
    h_ref[...] = h.astype(h_ref.dtype)


def _prenorm(x2, nw, mods3, k, seq):
    m, d = x2.shape
    tm = min(1024, seq)
    per_b = seq // tm
    return pl.pallas_call(
        functools.partial(_prenorm_kernel, k=k),
        out_shape=jax.ShapeDtypeStruct((m, d), BF16),
        grid=(m // tm,),
        in_specs=[pl.BlockSpec((tm, d), lambda i: (i, 0)),
                  pl.BlockSpec((1, d), lambda i: (0, 0)),
                  pl.BlockSpec((None, N_ADA, d), lambda i: (i // per_b, 0, 0))],
        out_specs=pl.BlockSpec((tm, d), lambda i: (i, 0)),
        compiler_params=_params("arbitrary"),
        name="prenorm",
    )(x2, nw.reshape(1, d), mods3)


FFN_ROWS = 1024
FFN_HIDDEN = 512
FFN_EPI_CHUNKS = 8
FFN_SUB_BLOCKS = 2


class _CastJob(NamedTuple):
    src: jax.Array
    cols: tuple
    out_rows: int

    def chunk_rows(self, slots):
        need = -(-self.out_rows // slots)
        rows = next(r for r in range(BF16_ROWS, self.out_rows + 1, BF16_ROWS)
                    if r >= need and self.out_rows % r == 0)
        return rows


def _cast_chunk(job, rows, chunk, src_ref, dst_refs):
    src_rows = job.src.shape[0]
    out_chunk = jnp.minimum(chunk, job.out_rows // rows - 1)
    partial = src_rows % rows != 0 or job.out_rows != src_rows
    if partial:
        r = lax.broadcasted_iota(jnp.int32, (rows, 1), 0) + out_chunk * rows
        keep = r < src_rows
    for (c0, c1, width), dst in zip(job.cols, dst_refs):
        v = src_ref[:, c0:c1]
        if partial:
            v = jnp.where(keep, v, 0.0)
        dst[:, :c1 - c0] = v.astype(BF16)
        if width > c1 - c0:
            dst[:, c1 - c0:] = jnp.zeros((rows, width - (c1 - c0)), BF16)


def _ffn_kernel(h_ref, x_ref, wa_ref, wb_ref, wo_ref, mods_ref, nw_ref, *refs, k_gate, k_next, n_tiles,
                jobs):
    n_job_out = sum(len(job.cols) for job, _ in jobs)
    job_src = refs[:len(jobs)]
    refs = refs[len(jobs):]
    acc = refs[-1]
    out_refs = refs[:len(refs) - 1 - n_job_out]
    job_dst = refs[len(out_refs):-1]
    i = pl.program_id(0)
    j = pl.program_id(1)
    slot = i % 2

    def casts():
        chunk = i * FFN_EPI_CHUNKS + jnp.minimum(j, FFN_EPI_CHUNKS - 1)
        k = 0
        for (job, rows), src_ref in zip(jobs, job_src):
            _cast_chunk(job, rows, chunk, src_ref, job_dst[k:k + len(job.cols)])
            k += len(job.cols)

    @pl.when((i == 0) & (j == 0))
    def _():
        acc[...] = jnp.zeros_like(acc)

    def epilogue():
        rows = x_ref.shape[0]
        r0 = pl.multiple_of(jnp.minimum(j, FFN_EPI_CHUNKS - 1) * rows, rows)
        done = acc[1 - slot, pl.ds(r0, rows), :]
        xn = x_ref[...] + 0.5 * mods_ref[k_gate:k_gate + 1, :] * done
        if k_next is None:
            ms = jnp.mean(xn * xn, axis=-1, keepdims=True)
            out_refs[0][...] = xn * lax.rsqrt(ms + EPS) * nw_ref[...]
        else:
            out_refs[0][...] = xn
            hn = _mod_rms(xn, nw_ref[...], mods_ref[k_next:k_next + 1, :],
                          mods_ref[k_next + 1:k_next + 2, :])
            out_refs[1][...] = hn.astype(BF16)

    def matmuls():
        sub = h_ref.shape[0] // FFN_SUB_BLOCKS
        for r in range(FFN_SUB_BLOCKS):
            rows = pl.ds(r * sub, sub)
            h = h_ref[rows, :]
            a = _dot(h, wa_ref[...])
            b = _dot(h, wb_ref[...])
            act = (_silu(a) * b).astype(BF16)
            acc[slot, rows, :] = jnp.where(j == 0, 0.0, acc[slot, rows, :]) + _dot(act, wo_ref[...])

    chunked = j < FFN_EPI_CHUNKS
    first = i == 0
    steady = (i > 0) & (i < n_tiles)

    @pl.when(first & chunked)
    def _():
        casts()
        matmuls()

    @pl.when(steady & chunked)
    def _():
        epilogue()
        casts()
        matmuls()

    @pl.when((first | steady) & jnp.logical_not(chunked))
    def _():
        matmuls()

    @pl.when((i == n_tiles) & chunked)
    def _():
        epilogue()


def _ffn(h, x2, wa, wb, wo, mods3, nw_next, k_gate, k_next, seq, cast_jobs=()):
    m, d = x2.shape
    ffp = wa.shape[1]
    tm = min(FFN_ROWS, seq)
    tf = FFN_HIDDEN
    n_tiles = m // tm
    n_hidden = ffp // tf
    ec = FFN_EPI_CHUNKS
    rows = tm // ec
    per_b = seq // tm
    assert n_hidden >= ec

    def done_chunk(i, j):
        return (jnp.where(i == 0, 0, (i - 1) * ec + jnp.minimum(j, ec - 1)), 0)

    def hidden(i, j):
        return jnp.where(i == n_tiles, n_hidden - 1, j)

    def cast_chunk(i, j):
        return jnp.where(i == n_tiles, n_tiles * ec - 1, i * ec + jnp.minimum(j, ec - 1))

    out_shape = [jax.ShapeDtypeStruct((m, d), F32)]
    out_specs = [pl.BlockSpec((rows, d), done_chunk)]
    if k_next is not None:
        out_shape.append(jax.ShapeDtypeStruct((m, d), BF16))
        out_specs.append(pl.BlockSpec((rows, d), done_chunk))
    in_specs = [pl.BlockSpec((tm, d), lambda i, j: (jnp.minimum(i, n_tiles - 1), 0)),
                pl.BlockSpec((rows, d), done_chunk),
                pl.BlockSpec((d, tf), lambda i, j: (0, hidden(i, j))),
                pl.BlockSpec((d, tf), lambda i, j: (0, hidden(i, j))),
                pl.BlockSpec((tf, d), lambda i, j: (hidden(i, j), 0)),
                pl.BlockSpec((None, N_ADA, d), lambda i, j: (jnp.maximum(i - 1, 0) // per_b, 0, 0)),
                pl.BlockSpec((1, d), lambda i, j: (0, 0))]
    sized_jobs = []
    for job in cast_jobs:
        r = job.chunk_rows(n_tiles * ec)
        sized_jobs.append((job._replace(src=jax.ShapeDtypeStruct(job.src.shape, F32)), r))
        last_out = job.out_rows // r - 1
        last_in = (job.src.shape[0] - 1) // r
        in_specs.append(pl.BlockSpec(
            (r, job.src.shape[1]),
            lambda i, j, lo=last_out, li=last_in: (jnp.minimum(jnp.minimum(cast_chunk(i, j), lo), li), 0)))
        for _, _, width in job.cols:
            out_shape.append(jax.ShapeDtypeStruct((job.out_rows, width), BF16))
            out_specs.append(pl.BlockSpec(
                (r, width), lambda i, j, lo=last_out: (jnp.minimum(cast_chunk(i, j), lo), 0)))
    return pl.pallas_call(
        functools.partial(_ffn_kernel, k_gate=k_gate, k_next=k_next, n_tiles=n_tiles,
                          jobs=tuple(sized_jobs)),
        out_shape=out_shape,
        grid=(n_tiles + 1, n_hidden),
        in_specs=in_specs,
        out_specs=out_specs,
        scratch_shapes=[pltpu.VMEM((2, tm, d), F32)],
        compiler_params=_params("arbitrary", "arbitrary"),
        name="ffn",
    )(h, x2, wa, wb, wo, mods3, nw_next.reshape(1, d), *[job.src for job in cast_jobs])


PROJ_ROWS = 1024
PROJ_SUB = 256


def _weight_rows(rows, d, row0):
    return pl.BlockSpec((pl.Element(rows), pl.Element(d)),
                        lambda *idx: (pl.multiple_of(row0(*idx), BF16_ROWS), 0))


def _proj_kernel(h_ref, w_ref, o_ref, *, act):
    y = _dot_nt(h_ref[...], w_ref[...])
    if act is not None:
        y = act(y)
    o_ref[...] = y.astype(o_ref.dtype)


def _proj(h, wt, col0, n, tn, act=None, out_dtype=F32):
    m, d = h.shape
    tm = min(PROJ_ROWS, m)
    return pl.pallas_call(
        functools.partial(_proj_kernel, act=act),
        out_shape=jax.ShapeDtypeStruct((m, n), out_dtype),
        grid=(m // tm, n // tn),
        in_specs=[pl.BlockSpec((tm, d), lambda i, j: (i, 0)),
                  _weight_rows(tn, d, lambda i, j: col0 + j * tn)],
        out_specs=pl.BlockSpec((tm, tn), lambda i, j: (i, j)),
        compiler_params=_params("arbitrary", "arbitrary"),
        name="proj",
    )(h, wt)


def _proj_conv_kernel(h_ref, w_ref, cw_ref, cb_ref, o_ref, *refs, tiles_per_seq):
    plain_ref = refs[0] if len(refs) == 2 else None
    halo = refs[-1]
    tn = o_ref.shape[1]
    i = pl.program_id(0)
    j = pl.program_id(1)
    tm = h_ref.shape[0]
    sub = min(PROJ_SUB, tm)
    taps = SSD_CONV - 1

    @pl.when(i % tiles_per_seq == 0)
    def _():
        halo[j] = jnp.zeros(halo.shape[1:], F32)

    prev = halo[j]
    w = w_ref[...]
    cw = cw_ref[...]
    cb = cb_ref[...]
    for r in range(tm // sub):
        raw = _dot_nt(h_ref[r * sub:(r + 1) * sub, :], w)
        if plain_ref is not None:
            plain_ref[r * sub:(r + 1) * sub, :] = raw[:, tn:]
            raw = raw[:, :tn]
        ext = jnp.concatenate([prev, raw], axis=0)
        part = cw[0:1, :] * ext
        for k in range(1, taps):
            part = pltpu.roll(part, 1, axis=0) + cw[k:k + 1, :] * ext
        conv = pltpu.roll(part, 1, axis=0)[SUBLANES:, :] + (cb + cw[taps:taps + 1, :] * raw)
        o_ref[r * sub:(r + 1) * sub, :] = _silu(conv).astype(o_ref.dtype)
        prev = raw[sub - SUBLANES:, :]
    halo[j] = prev


def _proj_conv(h, wt, col0, conv_w, conv_b, tn, seq, out_dtype, plain=0):
    m, d = h.shape
    n = conv_w.shape[1]
    tm = min(PROJ_ROWS, seq)
    out_shape = [jax.ShapeDtypeStruct((m, n), out_dtype)]
    out_specs = [pl.BlockSpec((tm, tn), lambda i, j: (i, j))]
    if plain:
        assert n == tn
        out_shape.append(jax.ShapeDtypeStruct((m, plain), F32))
        out_specs.append(pl.BlockSpec((tm, plain), lambda i, j: (i, 0)))
    w_spec = _weight_rows(tn + plain, d, lambda i, j: col0 + j * tn)
    out = pl.pallas_call(
        functools.partial(_proj_conv_kernel, tiles_per_seq=seq // tm),
        out_shape=out_shape,
        grid=(m // tm, n // tn),
        in_specs=[pl.BlockSpec((tm, d), lambda i, j: (i, 0)),
                  w_spec,
                  pl.BlockSpec((SSD_CONV, tn), lambda i, j: (0, j)),
                  pl.BlockSpec((1, tn), lambda i, j: (0, j))],
        out_specs=out_specs,
        scratch_shapes=[pltpu.VMEM((n // tn, SUBLANES, tn), F32)],
        compiler_params=_params("arbitrary", "arbitrary"),
        name="projconv",
    )(h, wt, conv_w, conv_b.reshape(1, n))
    return out if plain else out[0]


def _proj_chunked_kernel(h_ref, w_ref, o_ref, scr):
    n_slabs, rows, lq = o_ref.shape
    q = S5_SLAB
    L = lq // q
    y = _dot_nt(h_ref[...], w_ref[...])
    for c in range(scr.shape[0]):
        scr[c] = y[:, c * LANES:(c + 1) * LANES]
    for c in range(scr.shape[0]):
        s, off = divmod(c * LANES, q)
        for l in range(L):
            o_ref[s, :, l * q + off:l * q + off + LANES] = scr[c, pl.ds(l, rows, stride=L), :]


def _proj_chunked(h, wt, row0, n):
    m, d = h.shape
    tm = min(PROJ_ROWS, m)
    L = S5_CHUNK
    n_slabs = n // S5_SLAB
    return pl.pallas_call(
        _proj_chunked_kernel,
        out_shape=jax.ShapeDtypeStruct((n_slabs, m // L, L * S5_SLAB), F32),
        grid=(m // tm,),
        in_specs=[pl.BlockSpec((tm, d), lambda i: (i, 0)),
                  _weight_rows(n, d, lambda i: row0)],
        out_specs=pl.BlockSpec((n_slabs, tm // L, L * S5_SLAB), lambda i: (0, i, 0)),
        scratch_shapes=[pltpu.VMEM((n // LANES, tm, LANES), F32)],
        compiler_params=_params("arbitrary"),
        name="projchunk",
    )(h, wt)


def _split3(v):
    hi = v.astype(BF16)
    r = v - hi.astype(F32)
    mid = r.astype(BF16)
    lo = (r - mid.astype(F32)).astype(BF16)
    return hi, mid, lo


def _pair(v, j, first):
    return jnp.where(first, v[:, 2 * j:2 * j + 1], v[:, 2 * j + 1:2 * j + 2])


def _ssd_kernel(xs_ref, bc_ref, dt_ref, zs_ref, dtb_ref, alog_ref, dsk_ref, nw_ref, y_ref, state):
    L = SSD_CHUNK
    d_inner = xs_ref.shape[1]
    n_state = SSD_STATE
    gw = d_inner // SSD_GROUPS
    pairs_per_group = gw // LANES

    @pl.when(pl.program_id(1) == 0)
    def _():
        state[...] = jnp.zeros_like(state)

    dtr = dt_ref[...] + dtb_ref[...]
    dt = jnp.maximum(dtr, 0.0) + jnp.log1p(jnp.exp(-jnp.abs(dtr)))
    da = dt * (-jnp.exp(alog_ref[...]))
    row = lax.broadcasted_iota(jnp.int32, (L, L), 0)
    col = lax.broadcasted_iota(jnp.int32, (L, L), 1)
    causal = row >= col
    tril = jnp.where(causal, 1.0, 0.0).astype(BF16)
    cs = sum(_dot(tril, part) for part in _split3(da))
    cs_last = cs[L - 1:L, :]
    ecs = jnp.exp(cs)
    ecl = jnp.exp(cs_last)
    cs_t = cs.T
    dt_t = dt.T
    wst_t = (dt * jnp.exp(cs_last - cs)).T

    lane = lax.broadcasted_iota(jnp.int32, (L, LANES), 1)
    first = lane < SSD_HEAD_DIM
    first_row = first[0:1, :]

    y_parts = []
    for g in range(SSD_GROUPS):
        bm_g = bc_ref[:, g * n_state:(g + 1) * n_state]
        cm_g = bc_ref[:, (SSD_GROUPS + g) * n_state:(SSD_GROUPS + g + 1) * n_state]
        cb = _dot_nt(cm_g, bm_g)
        bm_t = bm_g.astype(F32).T
        y_off = _dot(cm_g, state[g].astype(BF16))
        for jj in range(pairs_per_group):
            j = g * pairs_per_group + jj
            h0, h1 = 2 * j, 2 * j + 1
            cols = slice(j * LANES, (j + 1) * LANES)
            gcols = slice(jj * LANES, (jj + 1) * LANES)
            xs_p = xs_ref[:, cols]
            xb = xs_p.astype(BF16)
            zero = jnp.zeros_like(xb)
            rhs = jnp.concatenate([jnp.where(first, xb, zero), jnp.where(first, zero, xb)], axis=0)
            lhs_y = []
            lhs_s = []
            for h in (h0, h1):
                dec = jnp.exp(jnp.where(causal, cs[:, h:h + 1] - cs_t[h:h + 1, :], -jnp.inf))
                lhs_y.append((cb * dec * dt_t[h:h + 1, :]).astype(BF16))
                lhs_s.append((bm_t * wst_t[h:h + 1, :]).astype(BF16))
            y_p = (_dot(jnp.concatenate(lhs_y, axis=1), rhs)
                   + _pair(ecs, j, first) * y_off[:, gcols]
                   + dsk_ref[:, cols] * xs_p)
            y_parts.append(y_p)
            st_new = _dot(jnp.concatenate(lhs_s, axis=1), rhs)
            state[g, :, gcols] = _pair(ecl, j, first_row) * state[g, :, gcols] + st_new

    yz = jnp.concatenate(y_parts, axis=1) * zs_ref[...]
    outs = []
    for g in range(SSD_GROUPS):
        seg = yz[:, g * gw:(g + 1) * gw]
        ms = jnp.mean(seg * seg, axis=-1, keepdims=True)
        outs.append(seg * lax.rsqrt(ms + EPS))
    y_ref[...] = (jnp.concatenate(outs, axis=1) * nw_ref[...]).astype(y_ref.dtype)


def _ssd(xs, bc, dtp, zs, dt_bias, a_log, d_ssd, norm_w, bsz, seq):
    m, d_inner = xs.shape
    L = SSD_CHUNK
    nc = seq // L
    heads = d_inner // SSD_HEAD_DIM
    pad = LANES - heads
    row = lambda b, c: (b * nc + c, 0)
    const = lambda b, c: (0, 0)
    return pl.pallas_call(
        _ssd_kernel,
        out_shape=jax.ShapeDtypeStruct((m, d_inner), BF16),
        grid=(bsz, nc),
        in_specs=[pl.BlockSpec((L, d_inner), row),
                  pl.BlockSpec((L, bc.shape[1]), row),
                  pl.BlockSpec((L, LANES), row),
                  pl.BlockSpec((L, d_inner), row),
                  pl.BlockSpec((1, LANES), const),
                  pl.BlockSpec((1, LANES), const),
                  pl.BlockSpec((1, d_inner), const),
                  pl.BlockSpec((1, d_inner), const)],
        out_specs=pl.BlockSpec((L, d_inner), row),
        scratch_shapes=[pltpu.VMEM((SSD_GROUPS, SSD_STATE, d_inner // SSD_GROUPS), F32)],
        compiler_params=_params("arbitrary", "arbitrary"),
        name="ssd",
    )(xs, bc, dtp, zs,
      jnp.pad(dt_bias, (0, pad)).reshape(1, LANES), jnp.pad(a_log, (0, pad)).reshape(1, LANES),
      jnp.repeat(d_ssd, SSD_HEAD_DIM).reshape(1, d_inner), norm_w.reshape(1, d_inner))


def _s5_prep_kernel(lre_ref, lim_ref, ldt_ref, btr_ref, bti_ref, cnr_ref, cni_ref,
                    wend_ref, wcorr_ref, krev_ref, al_ref):
    L = S5_CHUNK
    q = S5_SLAB
    ns = S5_SLAB_STATE
    lr = jnp.minimum(lre_ref[...], -1e-4)
    li = lim_ref[...]
    dt = jnp.exp(ldt_ref[...])

    def power(k):
        mag = jnp.exp(lr * dt * float(k))
        ang = li * dt * float(k)
        return mag * jnp.cos(ang), mag * jnp.sin(ang)

    ar, ai = power(1)
    den = lr * lr + li * li
    nr = ar - 1.0
    kr = (nr * lr + ai * li) / den
    ki = (ai * lr - nr * li) / den

    rows = lax.broadcasted_iota(jnp.int32, (q, ns), 0) // S5_GROUP_SIZE
    cols = lax.broadcasted_iota(jnp.int32, (q, ns), 1) // S5_STATE
    same = rows == cols

    def block_diag(ref):
        return jnp.where(same, jnp.concatenate([ref[...]] * (ns // LANES), axis=1), 0.0)

    btr, bti = block_diag(btr_ref), block_diag(bti_ref)
    bbr = btr * kr - bti * ki
    bbi = btr * ki + bti * kr
    cr, ci = block_diag(cnr_ref), block_diag(cni_ref)

    for k in range(L):
        pr, pi = power(k)
        blk = jnp.concatenate([bbr * pr - bbi * pi, bbr * pi + bbi * pr], axis=1)
        wend_ref[(L - 1 - k) * q:(L - k) * q, :] = blk.astype(BF16)
    for l in range(L):
        pr, pi = power(l + 1)
        blk = jnp.concatenate([cr * pr - ci * pi, -(cr * pi) - ci * pr], axis=1)
        wcorr_ref[l * q:(l + 1) * q, :] = blk.astype(BF16)
    cstack = jnp.concatenate([cr, -ci], axis=1).astype(BF16)
    kw = _dot_nt(wend_ref[...], cstack)
    tb = MXU_WIDTH // q
    for b in range(tb):
        up = (tb - 1 - b) * q
        shifted = kw if up == 0 else jnp.concatenate([kw[up:], jnp.zeros((up, q), F32)], axis=0)
        krev_ref[:, b * q:(b + 1) * q] = shifted.astype(BF16)
    pr, pi = power(L)
    al_ref[...] = jnp.concatenate([pr, pi], axis=1)


def _s5_prep_call(lam_re, lam_im, log_dt, b_re, b_im, c_re, c_im, slab_of):
    g, p = lam_re.shape
    width = g * S5_GROUP_SIZE
    n_slabs = width // S5_SLAB
    ns = S5_SLAB_STATE
    lq = S5_CHUNK * S5_SLAB
    row_vec = lambda v: v.reshape(1, g * p)
    twice = lambda v: jnp.concatenate([v, v], axis=1)
    bt = lambda v: twice(jnp.swapaxes(v, 1, 2).reshape(width, p))
    cn = lambda v: twice(v.reshape(width, p))
    vec = pl.BlockSpec((None, 1, ns), lambda c: (slab_of(c), 0, 0))
    mat = pl.BlockSpec((S5_SLAB, LANES), lambda c: (slab_of(c), 0))
    vec3 = lambda v: row_vec(v).reshape(n_slabs, 1, ns)
    slab3 = lambda rows, cols: pl.BlockSpec((None, rows, cols), lambda c: (slab_of(c), 0, 0))
    operands = [vec3(lam_re), vec3(lam_im), vec3(jnp.repeat(log_dt, p)),
                bt(b_re), bt(b_im), cn(c_re), cn(c_im)]
    out_shape = [jax.ShapeDtypeStruct((n_slabs, lq, 2 * ns), BF16),
                 jax.ShapeDtypeStruct((n_slabs, lq, 2 * ns), BF16),
                 jax.ShapeDtypeStruct((n_slabs, lq, MXU_WIDTH), BF16),
                 jax.ShapeDtypeStruct((n_slabs, 1, 2 * ns), F32)]
    out_specs = [slab3(lq, 2 * ns), slab3(lq, 2 * ns), slab3(lq, MXU_WIDTH), slab3(1, 2 * ns)]
    return operands, [vec, vec, vec, mat, mat, mat, mat], out_shape, out_specs, n_slabs


def _gelu_tanh(v):
    return 0.5 * v * (1.0 + jnp.tanh(math.sqrt(2.0 / math.pi) * (v + 0.044715 * (v * v * v))))


def _s5_kernel(u_ref, wend_ref, wcorr_ref, krev_ref, al_ref, d_ref, y_ref, e_scr, cin_scr, st_scr,
               tok_scr, *, blocks_per_seq):
    L = S5_CHUNK
    q = S5_SLAB
    ns = S5_SLAB_STATE
    rows = u_ref.shape[0]

    @pl.when(pl.program_id(1) % blocks_per_seq == 0)
    def _():
        st_scr[...] = jnp.zeros_like(st_scr)

    u = u_ref[...]
    ub = u.astype(BF16)
    e_scr[...] = _dot(ub, wend_ref[...])
    alr = al_ref[:, :ns]
    ali = al_ref[:, ns:]

    def tile_step(t, st):
        base = pl.multiple_of(t * SUBLANES, SUBLANES)
        e = e_scr[pl.ds(base, SUBLANES), :]
        sr, si = st
        carried = []
        for r in range(SUBLANES):
            carried.append(jnp.concatenate([sr, si], axis=1))
            er = e[r:r + 1, :ns]
            ei = e[r:r + 1, ns:]
            sr, si = alr * sr - ali * si + er, alr * si + ali * sr + ei
        cin_scr[pl.ds(base, SUBLANES), :] = jnp.concatenate(carried, axis=0)
        return sr, si

    sr, si = lax.fori_loop(0, rows // SUBLANES, tile_step, (st_scr[:, :ns], st_scr[:, ns:]))
    st_scr[...] = jnp.concatenate([sr, si], axis=1)

    corr = _dot_nt(cin_scr[...].astype(BF16), wcorr_ref[...])
    w = MXU_WIDTH
    nb = L * q // w
    intra = [_dot(ub[:, :(b + 1) * w], krev_ref[(nb - 1 - b) * w:, :]) for b in range(nb)]
    y = _gelu_tanh(jnp.concatenate(intra, axis=1) + corr + d_ref[...] * u)
    for c in range(tok_scr.shape[0]):
        for l in range(L):
            lo = l * q + c * LANES
            tok_scr[c, pl.ds(l, rows, stride=L), :] = y[:, lo:lo + LANES]
        y_ref[:, c * LANES:(c + 1) * LANES] = tok_scr[c].astype(y_ref.dtype)


def _s5(u2, prep, d_skip, bsz, seq):
    wend, wcorr, krev, al = prep
    n_slabs, chunks, lq = u2.shape
    L = S5_CHUNK
    q = S5_SLAB
    ns2 = 2 * S5_SLAB_STATE
    chunks_per_seq = seq // L
    rb = min(256, chunks_per_seq)
    blocks_per_seq = chunks_per_seq // rb
    per_w = MXU_WIDTH // q
    d_t = jnp.tile(d_skip.reshape(n_slabs, 1, q), (1, 1, L))
    slab = lambda s, r: (s, 0, 0)
    blk = lambda s, r: (s, r, 0)
    return pl.pallas_call(
        functools.partial(_s5_kernel, blocks_per_seq=blocks_per_seq),
        out_shape=jax.ShapeDtypeStruct((n_slabs // per_w, chunks * L, MXU_WIDTH), BF16),
        grid=(n_slabs, chunks // rb),
        in_specs=[pl.BlockSpec((None, rb, lq), blk),
                  pl.BlockSpec((None, lq, ns2), slab),
                  pl.BlockSpec((None, lq, ns2), slab),
                  pl.BlockSpec((None, lq, MXU_WIDTH), slab),
                  pl.BlockSpec((None, 1, ns2), slab),
                  pl.BlockSpec((None, 1, lq), slab)],
        out_specs=pl.BlockSpec((None, rb * L, q), lambda s, r: (s // per_w, r, s % per_w)),
        scratch_shapes=[pltpu.VMEM((rb, ns2), F32), pltpu.VMEM((rb, ns2), F32),
                        pltpu.VMEM((1, ns2), F32), pltpu.VMEM((q // LANES, rb * L, LANES), F32)],
        compiler_params=_params("arbitrary", "arbitrary"),
        name="s5",
    )(u2, wend, wcorr, krev, al, d_t)


def _merge_kernel(ya_ref, yb_ref, ga_ref, gb_ref, x_ref, wa_ref, wg_ref, wo_ref, mods_ref, nw_ref,
                  xo_ref, h_ref, *, k_gate, k_next):
    d = x_ref.shape[1]
    p_a = _dot(ya_ref[...], wa_ref[...])
    glu = sum(_dot(yb_ref[s], wg_ref[s]) for s in range(yb_ref.shape[0]))
    p_b = glu[:, :d] * _sigmoid(glu[:, d:])
    merged = ga_ref[...] * p_a + gb_ref[...] * p_b
    xn = x_ref[...] + mods_ref[k_gate:k_gate + 1, :] * _dot(merged.astype(BF16), wo_ref[...])
    xo_ref[...] = xn
    hn = _mod_rms(xn, nw_ref[...], mods_ref[k_next:k_next + 1, :], mods_ref[k_next + 1:k_next + 2, :])
    h_ref[...] = hn.astype(BF16)


def _merge(ya, yb_slabs, gates, x2, w_a, w_glu, w_o, mods3, nw_next, k_gate, k_next, seq):
    m, d = x2.shape
    n_slabs, _, q = yb_slabs.shape
    tm = min(256, seq)
    per_b = seq // tm
    row = lambda i: (i, 0)
    const = lambda i: (0, 0)
    once = pl.Buffered(1)
    return pl.pallas_call(
        functools.partial(_merge_kernel, k_gate=k_gate, k_next=k_next),
        out_shape=[jax.ShapeDtypeStruct((m, d), F32), jax.ShapeDtypeStruct((m, d), BF16)],
        grid=(m // tm,),
        in_specs=[pl.BlockSpec((tm, ya.shape[1]), row),
                  pl.BlockSpec((n_slabs, tm, q), lambda i: (0, i, 0)),
                  pl.BlockSpec((tm, d), row),
                  pl.BlockSpec((tm, d), lambda i: (i, 1)),
                  pl.BlockSpec((tm, d), row),
                  pl.BlockSpec(w_a.shape, const, pipeline_mode=once),
                  pl.BlockSpec(w_glu.shape, lambda i: (0, 0, 0), pipeline_mode=once),
                  pl.BlockSpec(w_o.shape, const, pipeline_mode=once),
                  pl.BlockSpec((None, N_ADA, d), lambda i: (i // per_b, 0, 0)),
                  pl.BlockSpec((1, d), const)],
        out_specs=[pl.BlockSpec((tm, d), row), pl.BlockSpec((tm, d), row)],
        compiler_params=_params("arbitrary"),
        name="merge",
    )(ya, yb_slabs, gates, gates, x2, w_a, w_glu, w_o, mods3, nw_next.reshape(1, d))


CAST_STEPS = 16


S5_PREP_INPUTS = 7


def _cast_kernel(*refs, jobs, s5_slabs):
    chunk = pl.program_id(0)
    n_in = len(jobs) + (S5_PREP_INPUTS if s5_slabs else 0)
    n_dst = sum(len(job.cols) for job, _ in jobs)
    src, s5_in = refs[:len(jobs)], refs[len(jobs):n_in]
    dst, s5_out = refs[n_in:n_in + n_dst], refs[n_in + n_dst:]
    k = 0
    for (job, rows), src_ref in zip(jobs, src):
        _cast_chunk(job, rows, chunk, src_ref, dst[k:k + len(job.cols)])
        k += len(job.cols)
    if s5_slabs:
        @pl.when(chunk < s5_slabs)
        def _():
            _s5_prep_kernel(*s5_in, *s5_out)


def _cast_weights(cast_jobs, s5_params=None):
    in_specs, out_specs, out_shape, sized = [], [], [], []
    for job in cast_jobs:
        r = job.chunk_rows(CAST_STEPS)
        sized.append((job._replace(src=jax.ShapeDtypeStruct(job.src.shape, F32)), r))
        last_out = job.out_rows // r - 1
        last_in = (job.src.shape[0] - 1) // r
        in_specs.append(pl.BlockSpec(
            (r, job.src.shape[1]), lambda c, lo=last_out, li=last_in: (jnp.minimum(jnp.minimum(c, lo), li), 0)))
        for _, _, width in job.cols:
            out_shape.append(jax.ShapeDtypeStruct((job.out_rows, width), BF16))
            out_specs.append(pl.BlockSpec((r, width), lambda c, lo=last_out: (jnp.minimum(c, lo), 0)))
    operands = [job.src for job in cast_jobs]
    s5_slabs = 0
    if s5_params is not None:
        width = s5_params[0].shape[0] * S5_GROUP_SIZE
        last = width // S5_SLAB - 1
        ops, ins, shapes, outs, s5_slabs = _s5_prep_call(*s5_params, lambda c: jnp.minimum(c, last))
        assert s5_slabs <= CAST_STEPS and len(ops) == S5_PREP_INPUTS
        operands += ops
        in_specs += ins
        out_shape += shapes
        out_specs += outs
    return pl.pallas_call(
        functools.partial(_cast_kernel, jobs=tuple(sized), s5_slabs=s5_slabs),
        out_shape=out_shape,
        grid=(CAST_STEPS,),
        in_specs=in_specs,
        out_specs=out_specs,
        compiler_params=_params("arbitrary"),
        name="castw",
    )(*operands)


def _ffn_cast_jobs(w_in, w_out):
    ff = w_out.shape[0]
    ffp = -(-ff // FFN_HIDDEN) * FFN_HIDDEN
    return (_CastJob(w_in, ((0, ff, ffp), (ff, 2 * ff, ffp)), w_in.shape[0]),
            _CastJob(w_out, ((0, w_out.shape[1], w_out.shape[1]),), ffp))


def kernel(x, c, w_ada, b_ada, norm_ffn1, w_ffn1_in, w_ffn1_out, norm_mix, w_in, conv_w, conv_b, dt_bias, a_log, d_ssd, ssd_norm_w, w_a_proj, s5_lambda_re, s5_lambda_im, s5_b_re, s5_b_im, s5_c_re, s5_c_im, s5_d, s5_log_dt, w_b_glu, w_out, norm_ffn2, w_ffn2_in, w_ffn2_out, norm_final):
    bsz, seq, d = x.shape
    depth = w_ada.shape[0]
    m = bsz * seq
    d_inner = ssd_norm_w.shape[1]
    conv_dim = conv_w.shape[2]
    heads = dt_bias.shape[1]
    s5_width = w_b_glu.shape[1]
    off_xbc = d_inner
    off_dt = off_xbc + conv_dim
    off_u = off_dt + heads
    off_g = off_u + s5_width

    assert depth == 1, "the epilogue fusion below is written for a single layer"
    l = 0
    x2 = x.reshape(m, d)
    mods3 = _mods(c, w_ada[l], b_ada[l]).reshape(bsz, N_ADA, d)
    h = _prenorm(x2, norm_ffn1[l], mods3, 0, seq)

    whole = lambda w: ((0, w.shape[1], w.shape[1]),)
    jobs = (*_ffn_cast_jobs(w_ffn2_in[l], w_ffn2_out[l]),
            _CastJob(w_a_proj[l], whole(w_a_proj[l]), d_inner),
            _CastJob(w_out[l], whole(w_out[l]), d),
            _CastJob(w_b_glu[l], whole(w_b_glu[l]), s5_width),
            _CastJob(w_in[l].T, ((0, d, d),), w_in.shape[2]))
    wa1, wb1, wo1, *prep = _cast_weights(
        _ffn_cast_jobs(w_ffn1_in[l], w_ffn1_out[l]),
        (s5_lambda_re[l], s5_lambda_im[l], s5_log_dt[l], s5_b_re[l], s5_b_im[l], s5_c_re[l], s5_c_im[l]))
    x2, h, wa2, wb2, wo2, w_a, w_o, w_glu, wi = _ffn(
        h, x2, wa1, wb1, wo1, mods3, norm_mix[l], 2, 3, seq, jobs)

    off_bc = off_xbc + d_inner
    zs = _proj(h, wi, 0, d_inner, 1024, act=_silu)
    xs = _proj_conv(h, wi, off_xbc, conv_w[l][:, :d_inner], conv_b[l][:d_inner], 1024, seq, F32)
    bc, dtp = _proj_conv(h, wi, off_bc, conv_w[l][:, d_inner:], conv_b[l][d_inner:], 1024, seq, BF16,
                         plain=LANES)
    u = _proj_chunked(h, wi, off_u, s5_width)
    gates = _proj(h, wi, off_g, 2 * d, 1024, act=_sigmoid, out_dtype=BF16)

    ya = _ssd(xs, bc, dtp, zs, dt_bias[l], a_log[l], d_ssd[l], ssd_norm_w[l], bsz, seq)
    yb = _s5(u, prep, s5_d[l], bsz, seq)

    x2, h = _merge(ya, yb, gates, x2, w_a, w_glu.reshape(yb.shape[0], yb.shape[2], 2 * d), w_o,
                   mods3, norm_ffn2[l], 5, 6, seq)

    (x2,) = _ffn(h, x2, wa2, wb2, wo2, mods3, norm_final, 8, None, seq)
    return x2.reshape(bsz, seq, d)
```

```python
import functools
import math
from typing import NamedTuple

import jax
import jax.numpy as jnp
from jax import lax
from jax.experimental import pallas as pl
from jax.experimental.pallas import tpu as pltpu

F32 = jnp.float32
BF16 = jnp.bfloat16
EPS = 1e-6

VMEM_LIMIT_BYTES = 56 * 1024 * 1024
LANES = 128
SUBLANES = 8
BF16_ROWS = 16

SSD_HEAD_DIM = 64
SSD_GROUPS = 4
SSD_STATE = 128
SSD_CONV = 4
SSD_CHUNK = 128
S5_GROUP_SIZE = 16
S5_STATE = 64
MXU_WIDTH = 256
S5_SLAB = 128
S5_GROUPS_PER_SLAB = S5_SLAB // S5_GROUP_SIZE
S5_SLAB_STATE = S5_GROUPS_PER_SLAB * S5_STATE
S5_CHUNK = 16
N_ADA = 9


def _params(*sem):
    return pltpu.CompilerParams(dimension_semantics=sem, vmem_limit_bytes=VMEM_LIMIT_BYTES)


def _sigmoid(v):
    return 0.5 * jnp.tanh(0.5 * v) + 0.5


def _silu(v):
    return v * _sigmoid(v)


def _dot(a, b):
    return jnp.dot(a, b, preferred_element_type=F32)


def _dot_nt(a, b):
    return lax.dot_general(a, b, (((1,), (1,)), ((), ())), preferred_element_type=F32)


def _mod_rms(x, nw, shift, scale):
    ms = jnp.mean(x * x, axis=-1, keepdims=True)
    return (x * lax.rsqrt(ms + EPS) * nw) * (1.0 + scale) + shift


def _mods_kernel(ct_ref, w_ref, b_ref, o_ref):
    ca = _silu(ct_ref[...])
    w = w_ref[...]
    rows = [jnp.sum(ca[:, b:b + 1] * w, axis=0, keepdims=True) for b in range(ca.shape[1])]
    o_ref[...] = jnp.concatenate(rows, axis=0) + b_ref[...]


def _mods(c, w_ada, b_ada):
    bsz, d = c.shape
    n = w_ada.shape[1]
    tn = 1024
    return pl.pallas_call(
        _mods_kernel,
        out_shape=jax.ShapeDtypeStruct((bsz, n), F32),
        grid=(n // tn,),
        in_specs=[pl.BlockSpec((d, bsz), lambda j: (0, 0)),
                  pl.BlockSpec((d, tn), lambda j: (0, j)),
                  pl.BlockSpec((1, tn), lambda j: (0, j))],
        out_specs=pl.BlockSpec((bsz, tn), lambda j: (0, j)),
        compiler_params=_params("arbitrary"),
        name="mods",
    )(c.T, w_ada, b_ada.reshape(1, n))


def _prenorm_kernel(x_ref, nw_ref, mods_ref, h_ref, *, k):
    h = _mod_rms(x_ref[...], nw_ref[...], mods_ref[k:k + 1, :], mods_ref[k + 1:k + 2, :])
    h_ref[...] = h.astype(h_ref.dtype)


def _prenorm(x2, nw, mods3, k, seq):
    m, d = x2.shape
    tm = min(1024, seq)
    per_b = seq // tm
    return pl.pallas_call(
        functools.partial(_prenorm_kernel, k=k),
        out_shape=jax.ShapeDtypeStruct((m, d), BF16),
        grid=(m // tm,),
        in_specs=[pl.BlockSpec((tm, d), lambda i: (i, 0)),
                  pl.BlockSpec((1, d), lambda i: (0, 0)),
                  pl.BlockSpec((None, N_ADA, d), lambda i: (i // per_b, 0, 0))],
        out_specs=pl.BlockSpec((tm, d), lambda i: (i, 0)),
        compiler_params=_params("arbitrary"),
        name="prenorm",
    )(x2, nw.reshape(1, d), mods3)


FFN_ROWS = 1024
FFN_HIDDEN = 512
FFN_EPI_CHUNKS = 8
FFN_SUB_BLOCKS = 2


class _CastJob(NamedTuple):
    src: jax.Array
    cols: tuple
    out_rows: int

    def chunk_rows(self, slots):
        need = -(-self.out_rows // slots)
        rows = next(r for r in range(BF16_ROWS, self.out_rows + 1, BF16_ROWS)
                    if r >= need and self.out_rows % r == 0)
        return rows


def _cast_chunk(job, rows, chunk, src_ref, dst_refs):
    src_rows = job.src.shape[0]
    out_chunk = jnp.minimum(chunk, job.out_rows // rows - 1)
    partial = src_rows % rows != 0 or job.out_rows != src_rows
    if partial:
        r = lax.broadcasted_iota(jnp.int32, (rows, 1), 0) + out_chunk * rows
        keep = r < src_rows
    for (c0, c1, width), dst in zip(job.cols, dst_refs):
        v = src_ref[:, c0:c1]
        if partial:
            v = jnp.where(keep, v, 0.0)
        dst[:, :c1 - c0] = v.astype(BF16)
        if width > c1 - c0:
            dst[:, c1 - c0:] = jnp.zeros((rows, width - (c1 - c0)), BF16)


def _ffn_kernel(h_ref, x_ref, wa_ref, wb_ref, wo_ref, mods_ref, nw_ref, *refs, k_gate, k_next, n_tiles,
                jobs):
    n_job_out = sum(len(job.cols) for job, _ in jobs)
    job_src = refs[:len(jobs)]
    refs = refs[len(jobs):]
    acc = refs[-1]
    out_refs = refs[:len(refs) - 1 - n_job_out]
    job_dst = refs[len(out_refs):-1]
    i = pl.program_id(0)
    j = pl.program_id(1)
    slot = i % 2

    def casts():
        chunk = i * FFN_EPI_CHUNKS + jnp.minimum(j, FFN_EPI_CHUNKS - 1)
        k = 0
        for (job, rows), src_ref in zip(jobs, job_src):
            _cast_chunk(job, rows, chunk, src_ref, job_dst[k:k + len(job.cols)])
            k += len(job.cols)

    @pl.when((i == 0) & (j == 0))
    def _():
        acc[...] = jnp.zeros_like(acc)

    def epilogue():
        rows = x_ref.shape[0]
        r0 = pl.multiple_of(jnp.minimum(j, FFN_EPI_CHUNKS - 1) * rows, rows)
        done = acc[1 - slot, pl.ds(r0, rows), :]
        xn = x_ref[...] + 0.5 * mods_ref[k_gate:k_gate + 1, :] * done
        if k_next is None:
            ms = jnp.mean(xn * xn, axis=-1, keepdims=True)
            out_refs[0][...] = xn * lax.rsqrt(ms + EPS) * nw_ref[...]
        else:
            out_refs[0][...] = xn
            hn = _mod_rms(xn, nw_ref[...], mods_ref[k_next:k_next + 1, :],
                          mods_ref[k_next + 1:k_next + 2, :])
            out_refs[1][...] = hn.astype(BF16)

    def matmuls():
        sub = h_ref.shape[0] // FFN_SUB_BLOCKS
        for r in range(FFN_SUB_BLOCKS):
            rows = pl.ds(r * sub, sub)
            h = h_ref[rows, :]
            a = _dot(h, wa_ref[...])
            b = _dot(h, wb_ref[...])
            act = (_silu(a) * b).astype(BF16)
            acc[slot, rows, :] = jnp.where(j == 0, 0.0, acc[slot, rows, :]) + _dot(act, wo_ref[...])

    chunked = j < FFN_EPI_CHUNKS
    first = i == 0
    steady = (i > 0) & (i < n_tiles)

    @pl.when(first & chunked)
    def _():
        casts()
        matmuls()

    @pl.when(steady & chunked)
    def _():
        epilogue()
        casts()
        matmuls()

    @pl.when((first | steady) & jnp.logical_not(chunked))
    def _():
        matmuls()

    @pl.when((i == n_tiles) & chunked)
    def _():
        epilogue()


def _ffn(h, x2, wa, wb, wo, mods3, nw_next, k_gate, k_next, seq, cast_jobs=()):
    m, d = x2.shape
    ffp = wa.shape[1]
    tm = min(FFN_ROWS, seq)
    tf = FFN_HIDDEN
    n_tiles = m // tm
    n_hidden = ffp // tf
    ec = FFN_EPI_CHUNKS
    rows = tm // ec
    per_b = seq // tm
    assert n_hidden >= ec

    def done_chunk(i, j):
        return (jnp.where(i == 0, 0, (i - 1) * ec + jnp.minimum(j, ec - 1)), 0)

    def hidden(i, j):
        return jnp.where(i == n_tiles, n_hidden - 1, j)

    def cast_chunk(i, j):
        return jnp.where(i == n_tiles, n_tiles * ec - 1, i * ec + jnp.minimum(j, ec - 1))

    out_shape = [jax.ShapeDtypeStruct((m, d), F32)]
    out_specs = [pl.BlockSpec((rows, d), done_chunk)]
    if k_next is not None:
        out_shape.append(jax.ShapeDtypeStruct((m, d), BF16))
        out_specs.append(pl.BlockSpec((rows, d), done_chunk))
    in_specs = [pl.BlockSpec((tm, d), lambda i, j: (jnp.minimum(i, n_tiles - 1), 0)),
                pl.BlockSpec((rows, d), done_chunk),
                pl.BlockSpec((d, tf), lambda i, j: (0, hidden(i, j))),
                pl.BlockSpec((d, tf), lambda i, j: (0, hidden(i, j))),
                pl.BlockSpec((tf, d), lambda i, j: (hidden(i, j), 0)),
                pl.BlockSpec((None, N_ADA, d), lambda i, j: (jnp.maximum(i - 1, 0) // per_b, 0, 0)),
                pl.BlockSpec((1, d), lambda i, j: (0, 0))]
    sized_jobs = []
    for job in cast_jobs:
        r = job.chunk_rows(n_tiles * ec)
        sized_jobs.append((job._replace(src=jax.ShapeDtypeStruct(job.src.shape, F32)), r))
        last_out = job.out_rows // r - 1
        last_in = (job.src.shape[0] - 1) // r
        in_specs.append(pl.BlockSpec(
            (r, job.src.shape[1]),
            lambda i, j, lo=last_out, li=last_in: (jnp.minimum(jnp.minimum(cast_chunk(i, j), lo), li), 0)))
        for _, _, width in job.cols:
            out_shape.append(jax.ShapeDtypeStruct((job.out_rows, width), BF16))
            out_specs.append(pl.BlockSpec(
                (r, width), lambda i, j, lo=last_out: (jnp.minimum(cast_chunk(i, j), lo), 0)))
    return pl.pallas_call(
        functools.partial(_ffn_kernel, k_gate=k_gate, k_next=k_next, n_tiles=n_tiles,
                          jobs=tuple(sized_jobs)),
        out_shape=out_shape,
        grid=(n_tiles + 1, n_hidden),
        in_specs=in_specs,
        out_specs=out_specs,
        scratch_shapes=[pltpu.VMEM((2, tm, d), F32)],
        compiler_params=_params("arbitrary", "arbitrary"),
        name="ffn",
    )(h, x2, wa, wb, wo, mods3, nw_next.reshape(1, d), *[job.src for job in cast_jobs])


PROJ_ROWS = 1024
PROJ_SUB = 256


def _weight_rows(rows, d, row0):
    return pl.BlockSpec((pl.Element(rows), pl.Element(d)),
                        lambda *idx: (pl.multiple_of(row0(*idx), BF16_ROWS), 0))


def _proj_kernel(h_ref, w_ref, o_ref, *, act):
    y = _dot_nt(h_ref[...], w_ref[...])
    if act is not None:
        y = act(y)
    o_ref[...] = y.astype(o_ref.dtype)


def _proj(h, wt, col0, n, tn, act=None, out_dtype=F32):
    m, d = h.shape
    tm = min(PROJ_ROWS, m)
    return pl.pallas_call(
        functools.partial(_proj_kernel, act=act),
        out_shape=jax.ShapeDtypeStruct((m, n), out_dtype),
        grid=(m // tm, n // tn),
        in_specs=[pl.BlockSpec((tm, d), lambda i, j: (i, 0)),
                  _weight_rows(tn, d, lambda i, j: col0 + j * tn)],
        out_specs=pl.BlockSpec((tm, tn), lambda i, j: (i, j)),
        compiler_params=_params("arbitrary", "arbitrary"),
        name="proj",
    )(h, wt)


def _proj_conv_kernel(h_ref, w_ref, cw_ref, cb_ref, o_ref, *refs, tiles_per_seq):
    plain_ref = refs[0] if len(refs) == 2 else None
    halo = refs[-1]
    tn = o_ref.shape[1]
    i = pl.program_id(0)
    j = pl.program_id(1)
    tm = h_ref.shape[0]
    sub = min(PROJ_SUB, tm)
    taps = SSD_CONV - 1

    @pl.when(i % tiles_per_seq == 0)
    def _():
        halo[j] = jnp.zeros(halo.shape[1:], F32)

    prev = halo[j]
    w = w_ref[...]
    cw = cw_ref[...]
    cb = cb_ref[...]
    for r in range(tm // sub):
        raw = _dot_nt(h_ref[r * sub:(r + 1) * sub, :], w)
        if plain_ref is not None:
            plain_ref[r * sub:(r + 1) * sub, :] = raw[:, tn:]
            raw = raw[:, :tn]
        ext = jnp.concatenate([prev, raw], axis=0)
        part = cw[0:1, :] * ext
        for k in range(1, taps):
            part = pltpu.roll(part, 1, axis=0) + cw[k:k + 1, :] * ext
        conv = pltpu.roll(part, 1, axis=0)[SUBLANES:, :] + (cb + cw[taps:taps + 1, :] * raw)
        o_ref[r * sub:(r + 1) * sub, :] = _silu(conv).astype(o_ref.dtype)
        prev = raw[sub - SUBLANES:, :]
    halo[j] = prev


def _proj_conv(h, wt, col0, conv_w, conv_b, tn, seq, out_dtype, plain=0):
    m, d = h.shape
    n = conv_w.shape[1]
    tm = min(PROJ_ROWS, seq)
    out_shape = [jax.ShapeDtypeStruct((m, n), out_dtype)]
    out_specs = [pl.BlockSpec((tm, tn), lambda i, j: (i, j))]
    if plain:
        assert n == tn
        out_shape.append(jax.ShapeDtypeStruct((m, plain), F32))
        out_specs.append(pl.BlockSpec((tm, plain), lambda i, j: (i, 0)))
    w_spec = _weight_rows(tn + plain, d, lambda i, j: col0 + j * tn)
    out = pl.pallas_call(
        functools.partial(_proj_conv_kernel, tiles_per_seq=seq // tm),
        out_shape=out_shape,
        grid=(m // tm, n // tn),
        in_specs=[pl.BlockSpec((tm, d), lambda i, j: (i, 0)),
                  w_spec,
                  pl.BlockSpec((SSD_CONV, tn), lambda i, j: (0, j)),
                  pl.BlockSpec((1, tn), lambda i, j: (0, j))],
        out_specs=out_specs,
        scratch_shapes=[pltpu.VMEM((n // tn, SUBLANES, tn), F32)],
        compiler_params=_params("arbitrary", "arbitrary"),
        name="projconv",
    )(h, wt, conv_w, conv_b.reshape(1, n))
    return out if plain else out[0]


def _proj_chunked_kernel(h_ref, w_ref, o_ref, scr):
    n_slabs, rows, lq = o_ref.shape
    q = S5_SLAB
    L = lq // q
    y = _dot_nt(h_ref[...], w_ref[...])
    for c in range(scr.shape[0]):
        scr[c] = y[:, c * LANES:(c + 1) * LANES]
    for c in range(scr.shape[0]):
        s, off = divmod(c * LANES, q)
        for l in range(L):
            o_ref[s, :, l * q + off:l * q + off + LANES] = scr[c, pl.ds(l, rows, stride=L), :]


def _proj_chunked(h, wt, row0, n):
    m, d = h.shape
    tm = min(PROJ_ROWS, m)
    L = S5_CHUNK
    n_slabs = n // S5_SLAB
    return pl.pallas_call(
        _proj_chunked_kernel,
        out_shape=jax.ShapeDtypeStruct((n_slabs, m // L, L * S5_SLAB), F32),
        grid=(m // tm,),
        in_specs=[pl.BlockSpec((tm, d), lambda i: (i, 0)),
                  _weight_rows(n, d, lambda i: row0)],
        out_specs=pl.BlockSpec((n_slabs, tm // L, L * S5_SLAB), lambda i: (0, i, 0)),
        scratch_shapes=[pltpu.VMEM((n // LANES, tm, LANES), F32)],
        compiler_params=_params("arbitrary"),
        name="projchunk",
    )(h, wt)


def _split3(v):
    hi = v.astype(BF16)
    r = v - hi.astype(F32)
    mid = r.astype(BF16)
    lo = (r - mid.astype(F32)).astype(BF16)
    return hi, mid, lo


def _pair(v, j, first):
    return jnp.where(first, v[:, 2 * j:2 * j + 1], v[:, 2 * j + 1:2 * j + 2])


def _ssd_kernel(xs_ref, bc_ref, dt_ref, zs_ref, dtb_ref, alog_ref, dsk_ref, nw_ref, y_ref, state):
    L = SSD_CHUNK
    d_inner = xs_ref.shape[1]
    n_state = SSD_STATE
    gw = d_inner // SSD_GROUPS
    pairs_per_group = gw // LANES

    @pl.when(pl.program_id(1) == 0)
    def _():
        state[...] = jnp.zeros_like(state)

    dtr = dt_ref[...] + dtb_ref[...]
    dt = jnp.maximum(dtr, 0.0) + jnp.log1p(jnp.exp(-jnp.abs(dtr)))
    da = dt * (-jnp.exp(alog_ref[...]))
    row = lax.broadcasted_iota(jnp.int32, (L, L), 0)
    col = lax.broadcasted_iota(jnp.int32, (L, L), 1)
    causal = row >= col
    tril = jnp.where(causal, 1.0, 0.0).astype(BF16)
    cs = sum(_dot(tril, part) for part in _split3(da))
    cs_last = cs[L - 1:L, :]
    ecs = jnp.exp(cs)
    ecl = jnp.exp(cs_last)
    cs_t = cs.T
    dt_t = dt.T
    wst_t = (dt * jnp.exp(cs_last - cs)).T

    lane = lax.broadcasted_iota(jnp.int32, (L, LANES), 1)
    first = lane < SSD_HEAD_DIM
    first_row = first[0:1, :]

    y_parts = []
    for g in range(SSD_GROUPS):
        bm_g = bc_ref[:, g * n_state:(g + 1) * n_state]
        cm_g = bc_ref[:, (SSD_GROUPS + g) * n_state:(SSD_GROUPS + g + 1) * n_state]
        cb = _dot_nt(cm_g, bm_g)
        bm_t = bm_g.astype(F32).T
        y_off = _dot(cm_g, state[g].astype(BF16))
        for jj in range(pairs_per_group):
            j = g * pairs_per_group + jj
            h0, h1 = 2 * j, 2 * j + 1
            cols = slice(j * LANES, (j + 1) * LANES)
            gcols = slice(jj * LANES, (jj + 1) * LANES)
            xs_p = xs_ref[:, cols]
            xb = xs_p.astype(BF16)
            zero = jnp.zeros_like(xb)
            rhs = jnp.concatenate([jnp.where(first, xb, zero), jnp.where(first, zero, xb)], axis=0)
            lhs_y = []
            lhs_s = []
            for h in (h0, h1):
                dec = jnp.exp(jnp.where(causal, cs[:, h:h + 1] - cs_t[h:h + 1, :], -jnp.inf))
                lhs_y.append((cb * dec * dt_t[h:h + 1, :]).astype(BF16))
                lhs_s.append((bm_t * wst_t[h:h + 1, :]).astype(BF16))
            y_p = (_dot(jnp.concatenate(lhs_y, axis=1), rhs)
                   + _pair(ecs, j, first) * y_off[:, gcols]
                   + dsk_ref[:, cols] * xs_p)
            y_parts.append(y_p)
            st_new = _dot(jnp.concatenate(lhs_s, axis=1), rhs)
            state[g, :, gcols] = _pair(ecl, j, first_row) * state[g, :, gcols] + st_new

    yz = jnp.concatenate(y_parts, axis=1) * zs_ref[...]
    outs = []
    for g in range(SSD_GROUPS):
        seg = yz[:, g * gw:(g + 1) * gw]
        ms = jnp.mean(seg * seg, axis=-1, keepdims=True)
        outs.append(seg * lax.rsqrt(ms + EPS))
    y_ref[...] = (jnp.concatenate(outs, axis=1) * nw_ref[...]).astype(y_ref.dtype)


def _ssd(xs, bc, dtp, zs, dt_bias, a_log, d_ssd, norm_w, bsz, seq):
    m, d_inner = xs.shape
    L = SSD_CHUNK
    nc = seq // L
    heads = d_inner // SSD_HEAD_DIM
    pad = LANES - heads
    row = lambda b, c: (b * nc + c, 0)
    const = lambda b, c: (0, 0)
    return pl.pallas_call(
        _ssd_kernel,
        out_shape=jax.ShapeDtypeStruct((m, d_inner), BF16),
        grid=(bsz, nc),
        in_specs=[pl.BlockSpec((L, d_inner), row),
                  pl.BlockSpec((L, bc.shape[1]), row),
                  pl.BlockSpec((L, LANES), row),
                  pl.BlockSpec((L, d_inner), row),
                  pl.BlockSpec((1, LANES), const),
                  pl.BlockSpec((1, LANES), const),
                  pl.BlockSpec((1, d_inner), const),
                  pl.BlockSpec((1, d_inner), const)],
        out_specs=pl.BlockSpec((L, d_inner), row),
        scratch_shapes=[pltpu.VMEM((SSD_GROUPS, SSD_STATE, d_inner // SSD_GROUPS), F32)],
        compiler_params=_params("arbitrary", "arbitrary"),
        name="ssd",
    )(xs, bc, dtp, zs,
      jnp.pad(dt_bias, (0, pad)).reshape(1, LANES), jnp.pad(a_log, (0, pad)).reshape(1, LANES),
      jnp.repeat(d_ssd, SSD_HEAD_DIM).reshape(1, d_inner), norm_w.reshape(1, d_inner))


def _s5_prep_kernel(lre_ref, lim_ref, ldt_ref, btr_ref, bti_ref, cnr_ref, cni_ref,
                    wend_ref, wcorr_ref, krev_ref, al_ref):
    L = S5_CHUNK
    q = S5_SLAB
    ns = S5_SLAB_STATE
    lr = jnp.minimum(lre_ref[...], -1e-4)
    li = lim_ref[...]
    dt = jnp.exp(ldt_ref[...])

    def power(k):
        mag = jnp.exp(lr * dt * float(k))
        ang = li * dt * float(k)
        return mag * jnp.cos(ang), mag * jnp.sin(ang)

    ar, ai = power(1)
    den = lr * lr + li * li
    nr = ar - 1.0
    kr = (nr * lr + ai * li) / den
    ki = (ai * lr - nr * li) / den

    rows = lax.broadcasted_iota(jnp.int32, (q, ns), 0) // S5_GROUP_SIZE
    cols = lax.broadcasted_iota(jnp.int32, (q, ns), 1) // S5_STATE
    same = rows == cols

    def block_diag(ref):
        return jnp.where(same, jnp.concatenate([ref[...]] * (ns // LANES), axis=1), 0.0)

    btr, bti = block_diag(btr_ref), block_diag(bti_ref)
    bbr = btr * kr - bti * ki
    bbi = btr * ki + bti * kr
    cr, ci = block_diag(cnr_ref), block_diag(cni_ref)

    for k in range(L):
        pr, pi = power(k)
        blk = jnp.concatenate([bbr * pr - bbi * pi, bbr * pi + bbi * pr], axis=1)
        wend_ref[(L - 1 - k) * q:(L - k) * q, :] = blk.astype(BF16)
    for l in range(L):
        pr, pi = power(l + 1)
        blk = jnp.concatenate([cr * pr - ci * pi, -(cr * pi) - ci * pr], axis=1)
        wcorr_ref[l * q:(l + 1) * q, :] = blk.astype(BF16)
    cstack = jnp.concatenate([cr, -ci], axis=1).astype(BF16)
    kw = _dot_nt(wend_ref[...], cstack)
    tb = MXU_WIDTH // q
    for b in range(tb):
        up = (tb - 1 - b) * q
        shifted = kw if up == 0 else jnp.concatenate([kw[up:], jnp.zeros((up, q), F32)], axis=0)
        krev_ref[:, b * q:(b + 1) * q] = shifted.astype(BF16)
    pr, pi = power(L)
    al_ref[...] = jnp.concatenate([pr, pi], axis=1)


def _s5_prep_call(lam_re, lam_im, log_dt, b_re, b_im, c_re, c_im, slab_of):
    g, p = lam_re.shape
    width = g * S5_GROUP_SIZE
    n_slabs = width // S5_SLAB
    ns = S5_SLAB_STATE
    lq = S5_CHUNK * S5_SLAB
    row_vec = lambda v: v.reshape(1, g * p)
    twice = lambda v: jnp.concatenate([v, v], axis=1)
    bt = lambda v: twice(jnp.swapaxes(v, 1, 2).reshape(width, p))
    cn = lambda v: twice(v.reshape(width, p))
    vec = pl.BlockSpec((None, 1, ns), lambda c: (slab_of(c), 0, 0))
    mat = pl.BlockSpec((S5_SLAB, LANES), lambda c: (slab_of(c), 0))
    vec3 = lambda v: row_vec(v).reshape(n_slabs, 1, ns)
    slab3 = lambda rows, cols: pl.BlockSpec((None, rows, cols), lambda c: (slab_of(c), 0, 0))
    operands = [vec3(lam_re), vec3(lam_im), vec3(jnp.repeat(log_dt, p)),
                bt(b_re), bt(b_im), cn(c_re), cn(c_im)]
    out_shape = [jax.ShapeDtypeStruct((n_slabs, lq, 2 * ns), BF16),
                 jax.ShapeDtypeStruct((n_slabs, lq, 2 * ns), BF16),
                 jax.ShapeDtypeStruct((n_slabs, lq, MXU_WIDTH), BF16),
                 jax.ShapeDtypeStruct((n_slabs, 1, 2 * ns), F32)]
    out_specs = [slab3(lq, 2 * ns), slab3(lq, 2 * ns), slab3(lq, MXU_WIDTH), slab3(1, 2 * ns)]
    return operands, [vec, vec, vec, mat, mat, mat, mat], out_shape, out_specs, n_slabs


def _gelu_tanh(v):
    return 0.5 * v * (1.0 + jnp.tanh(math.sqrt(2.0 / math.pi) * (v + 0.044715 * (v * v * v))))


def _s5_kernel(u_ref, wend_ref, wcorr_ref, krev_ref, al_ref, d_ref, y_ref, e_scr, cin_scr, st_scr,
               tok_scr, *, blocks_per_seq):
    L = S5_CHUNK
    q = S5_SLAB
    ns = S5_SLAB_STATE
    rows = u_ref.shape[0]

    @pl.when(pl.program_id(1) % blocks_per_seq == 0)
    def _():
        st_scr[...] = jnp.zeros_like(st_scr)

    u = u_ref[...]
    ub = u.astype(BF16)
    e_scr[...] = _dot(ub, wend_ref[...])
    alr = al_ref[:, :ns]
    ali = al_ref[:, ns:]

    def tile_step(t, st):
        base = pl.multiple_of(t * SUBLANES, SUBLANES)
        e = e_scr[pl.ds(base, SUBLANES), :]
        sr, si = st
        carried = []
        for r in range(SUBLANES):
            carried.append(jnp.concatenate([sr, si], axis=1))
            er = e[r:r + 1, :ns]
            ei = e[r:r + 1, ns:]
            sr, si = alr * sr - ali * si + er, alr * si + ali * sr + ei
        cin_scr[pl.ds(base, SUBLANES), :] = jnp.concatenate(carried, axis=0)
        return sr, si

    sr, si = lax.fori_loop(0, rows // SUBLANES, tile_step, (st_scr[:, :ns], st_scr[:, ns:]))
    st_scr[...] = jnp.concatenate([sr, si], axis=1)

    corr = _dot_nt(cin_scr[...].astype(BF16), wcorr_ref[...])
    w = MXU_WIDTH
    nb = L * q // w
    intra = [_dot(ub[:, :(b + 1) * w], krev_ref[(nb - 1 - b) * w:, :]) for b in range(nb)]
    y = _gelu_tanh(jnp.concatenate(intra, axis=1) + corr + d_ref[...] * u)
    for c in range(tok_scr.shape[0]):
        for l in range(L):
            lo = l * q + c * LANES
            tok_scr[c, pl.ds(l, rows, stride=L), :] = y[:, lo:lo + LANES]
        y_ref[:, c * LANES:(c + 1) * LANES] = tok_scr[c].astype(y_ref.dtype)


def _s5(u2, prep, d_skip, bsz, seq):
    wend, wcorr, krev, al = prep
    n_slabs, chunks, lq = u2.shape
    L = S5_CHUNK
    q = S5_SLAB
    ns2 = 2 * S5_SLAB_STATE
    chunks_per_seq = seq // L
    rb = min(512, chunks_per_seq)
    blocks_per_seq = chunks_per_seq // rb
    per_w = MXU_WIDTH // q
    d_t = jnp.tile(d_skip.reshape(n_slabs, 1, q), (1, 1, L))
    slab = lambda s, r: (s, 0, 0)
    blk = lambda s, r: (s, r, 0)
    return pl.pallas_call(
        functools.partial(_s5_kernel, blocks_per_seq=blocks_per_seq),
        out_shape=jax.ShapeDtypeStruct((n_slabs // per_w, chunks * L, MXU_WIDTH), BF16),
        grid=(n_slabs, chunks // rb),
        in_specs=[pl.BlockSpec((None, rb, lq), blk),
                  pl.BlockSpec((None, lq, ns2), slab),
                  pl.BlockSpec((None, lq, ns2), slab),
                  pl.BlockSpec((None, lq, MXU_WIDTH), slab),
                  pl.BlockSpec((None, 1, ns2), slab),
                  pl.BlockSpec((None, 1, lq), slab)],
        out_specs=pl.BlockSpec((None, rb * L, q), lambda s, r: (s // per_w, r, s % per_w)),
        scratch_shapes=[pltpu.VMEM((rb, ns2), F32), pltpu.VMEM((rb, ns2), F32),
                        pltpu.VMEM((1, ns2), F32), pltpu.VMEM((q // LANES, rb * L, LANES), F32)],
        compiler_params=_params("arbitrary", "arbitrary"),
        name="s5",
    )(u2, wend, wcorr, krev, al, d_t)


def _merge_kernel(ya_ref, yb_ref, ga_ref, gb_ref, x_ref, wa_ref, wg_ref, wo_ref, mods_ref, nw_ref,
                  xo_ref, h_ref, *, k_gate, k_next):
    d = x_ref.shape[1]
    p_a = _dot(ya_ref[...], wa_ref[...])
    glu = sum(_dot(yb_ref[s], wg_ref[s]) for s in range(yb_ref.shape[0]))
    p_b = glu[:, :d] * _sigmoid(glu[:, d:])
    merged = ga_ref[...] * p_a + gb_ref[...] * p_b
    xn = x_ref[...] + mods_ref[k_gate:k_gate + 1, :] * _dot(merged.astype(BF16), wo_ref[...])
    xo_ref[...] = xn
    hn = _mod_rms(xn, nw_ref[...], mods_ref[k_next:k_next + 1, :], mods_ref[k_next + 1:k_next + 2, :])
    h_ref[...] = hn.astype(BF16)


def _merge(ya, yb_slabs, gates, x2, w_a, w_glu, w_o, mods3, nw_next, k_gate, k_next, seq):
    m, d = x2.shape
    n_slabs, _, q = yb_slabs.shape
    tm = min(256, seq)
    per_b = seq // tm
    row = lambda i: (i, 0)
    const = lambda i: (0, 0)
    once = pl.Buffered(1)
    return pl.pallas_call(
        functools.partial(_merge_kernel, k_gate=k_gate, k_next=k_next),
        out_shape=[jax.ShapeDtypeStruct((m, d), F32), jax.ShapeDtypeStruct((m, d), BF16)],
        grid=(m // tm,),
        in_specs=[pl.BlockSpec((tm, ya.shape[1]), row),
                  pl.BlockSpec((n_slabs, tm, q), lambda i: (0, i, 0)),
                  pl.BlockSpec((tm, d), row),
                  pl.BlockSpec((tm, d), lambda i: (i, 1)),
                  pl.BlockSpec((tm, d), row),
                  pl.BlockSpec(w_a.shape, const, pipeline_mode=once),
                  pl.BlockSpec(w_glu.shape, lambda i: (0, 0, 0), pipeline_mode=once),
                  pl.BlockSpec(w_o.shape, const, pipeline_mode=once),
                  pl.BlockSpec((None, N_ADA, d), lambda i: (i // per_b, 0, 0)),
                  pl.BlockSpec((1, d), const)],
        out_specs=[pl.BlockSpec((tm, d), row), pl.BlockSpec((tm, d), row)],
        compiler_params=_params("arbitrary"),
        name="merge",
    )(ya, yb_slabs, gates, gates, x2, w_a, w_glu, w_o, mods3, nw_next.reshape(1, d))


CAST_STEPS = 16


S5_PREP_INPUTS = 7


def _cast_kernel(*refs, jobs, s5_slabs):
    chunk = pl.program_id(0)
    n_in = len(jobs) + (S5_PREP_INPUTS if s5_slabs else 0)
    n_dst = sum(len(job.cols) for job, _ in jobs)
    src, s5_in = refs[:len(jobs)], refs[len(jobs):n_in]
    dst, s5_out = refs[n_in:n_in + n_dst], refs[n_in + n_dst:]
    k = 0
    for (job, rows), src_ref in zip(jobs, src):
        _cast_chunk(job, rows, chunk, src_ref, dst[k:k + len(job.cols)])
        k += len(job.cols)
    if s5_slabs:
        @pl.when(chunk < s5_slabs)
        def _():
            _s5_prep_kernel(*s5_in, *s5_out)


def _cast_weights(cast_jobs, s5_params=None):
    in_specs, out_specs, out_shape, sized = [], [], [], []
    for job in cast_jobs:
        r = job.chunk_rows(CAST_STEPS)
        sized.append((job._replace(src=jax.ShapeDtypeStruct(job.src.shape, F32)), r))
        last_out = job.out_rows // r - 1
        last_in = (job.src.shape[0] - 1) // r
        in_specs.append(pl.BlockSpec(
            (r, job.src.shape[1]), lambda c, lo=last_out, li=last_in: (jnp.minimum(jnp.minimum(c, lo), li), 0)))
        for _, _, width in job.cols:
            out_shape.append(jax.ShapeDtypeStruct((job.out_rows, width), BF16))
            out_specs.append(pl.BlockSpec((r, width), lambda c, lo=last_out: (jnp.minimum(c, lo), 0)))
    operands = [job.src for job in cast_jobs]
    s5_slabs = 0
    if s5_params is not None:
        width = s5_params[0].shape[0] * S5_GROUP_SIZE
        last = width // S5_SLAB - 1
        ops, ins, shapes, outs, s5_slabs = _s5_prep_call(*s5_params, lambda c: jnp.minimum(c, last))
        assert s5_slabs <= CAST_STEPS and len(ops) == S5_PREP_INPUTS
        operands += ops
        in_specs += ins
        out_shape += shapes
        out_specs += outs
    return pl.pallas_call(
        functools.partial(_cast_kernel, jobs=tuple(sized), s5_slabs=s5_slabs),
        out_shape=out_shape,
        grid=(CAST_STEPS,),
        in_specs=in_specs,
        out_specs=out_specs,
        compiler_params=_params("arbitrary"),
        name="castw",
    )(*operands)


def _ffn_cast_jobs(w_in, w_out):
    ff = w_out.shape[0]
    ffp = -(-ff // FFN_HIDDEN) * FFN_HIDDEN
    return (_CastJob(w_in, ((0, ff, ffp), (ff, 2 * ff, ffp)), w_in.shape[0]),
            _CastJob(w_out, ((0, w_out.shape[1], w_out.shape[1]),), ffp))


def kernel(x, c, w_ada, b_ada, norm_ffn1, w_ffn1_in, w_ffn1_out, norm_mix, w_in, conv_w, conv_b, dt_bias, a_log, d_ssd, ssd_norm_w, w_a_proj, s5_lambda_re, s5_lambda_im, s5_b_re, s5_b_im, s5_c_re, s5_c_im, s5_d, s5_log_dt, w_b_glu, w_out, norm_ffn2, w_ffn2_in, w_ffn2_out, norm_final):
    bsz, seq, d = x.shape
    depth = w_ada.shape[0]
    m = bsz * seq
    d_inner = ssd_norm_w.shape[1]
    conv_dim = conv_w.shape[2]
    heads = dt_bias.shape[1]
    s5_width = w_b_glu.shape[1]
    off_xbc = d_inner
    off_dt = off_xbc + conv_dim
    off_u = off_dt + heads
    off_g = off_u + s5_width

    assert depth == 1, "the epilogue fusion below is written for a single layer"
    l = 0
    x2 = x.reshape(m, d)
    mods3 = _mods(c, w_ada[l], b_ada[l]).reshape(bsz, N_ADA, d)
    h = _prenorm(x2, norm_ffn1[l], mods3, 0, seq)

    whole = lambda w: ((0, w.shape[1], w.shape[1]),)
    jobs = (*_ffn_cast_jobs(w_ffn2_in[l], w_ffn2_out[l]),
            _CastJob(w_a_proj[l], whole(w_a_proj[l]), d_inner),
            _CastJob(w_out[l], whole(w_out[l]), d),
            _CastJob(w_b_glu[l], whole(w_b_glu[l]), s5_width),
            _CastJob(w_in[l].T, ((0, d, d),), w_in.shape[2]))
    wa1, wb1, wo1, *prep = _cast_weights(
        _ffn_cast_jobs(w_ffn1_in[l], w_ffn1_out[l]),
        (s5_lambda_re[l], s5_lambda_im[l], s5_log_dt[l], s5_b_re[l], s5_b_im[l], s5_c_re[l], s5_c_im[l]))
    x2, h, wa2, wb2, wo2, w_a, w_o, w_glu, wi = _ffn(
        h, x2, wa1, wb1, wo1, mods3, norm_mix[l], 2, 3, seq, jobs)

    off_bc = off_xbc + d_inner
    zs = _proj(h, wi, 0, d_inner, 1024, act=_silu)
    xs = _proj_conv(h, wi, off_xbc, conv_w[l][:, :d_inner], conv_b[l][:d_inner], 1024, seq, F32)
    bc, dtp = _proj_conv(h, wi, off_bc, conv_w[l][:, d_inner:], conv_b[l][d_inner:], 1024, seq, BF16,
                         plain=LANES)
    u = _proj_chunked(h, wi, off_u, s5_width)
    gates = _proj(h, wi, off_g, 2 * d, 1024, act=_sigmoid, out_dtype=BF16)

    ya = _ssd(xs, bc, dtp, zs, dt_bias[l], a_log[l], d_ssd[l], ssd_norm_w[l], bsz, seq)
    yb = _s5(u, prep, s5_d[l], bsz, seq)

    x2, h = _merge(ya, yb, gates, x2, w_a, w_glu.reshape(yb.shape[0], yb.shape[2], 2 * d), w_o,
                   mods3, norm_ffn2[l], 5, 6, seq)

    (x2,) = _ffn(h, x2, wa2, wb2, wo2, mods3, norm_final, 8, None, seq)
    return x2.reshape(bsz, seq, d)
```
